```python
import math, functools
import jax, jax.numpy as jnp
from jax import lax
import numpy as np

D_MODEL = 1024
BATCH = 4
SEQ = 4096
DEPTH = 4
DEC_BATCH = 32
DEC_SEQ = 4
PAST_LEN = 8192
PAGE_SIZE = 128

D_RWKV = D_MODEL // 2
RWKV_HEAD = 64
RWKV_HEADS = D_RWKV // RWKV_HEAD
R_DECAY = 64
R_AAA = 64
R_GATE = 128
RWKV_LN_EPS = 64e-5
D_ATT = D_MODEL // 2
ATT_HEAD_DIM = 64
ATT_HEADS = D_ATT // ATT_HEAD_DIM
ATT_KV_HEADS = 2
IDX_HEADS = 4
IDX_DIM = 64
TOPK_MAX = 256
Q_BLOCK = 128
D_HGRN = D_MODEL // 2
HGRN_EXPAND = 128
HGRN_HEADS = D_HGRN // HGRN_EXPAND
HGRN_CHUNK = 64
LB_FLOOR = 1e-30
NUM_BUCKETS = 32
MAX_DISTANCE = 128
MEM_TOKENS = 256
MEM_HEADS = 4
MEM_HEAD_DIM = D_MODEL // MEM_HEADS
D_FF = 2816
N_BRANCH = 3
NORM_EPS = 1e-6
NEG_INF = -1e30

RWKV_SPLITS = (D_RWKV, D_RWKV, D_RWKV, R_DECAY, R_AAA, R_GATE)
ATT_SPLITS = (D_ATT, ATT_KV_HEADS * ATT_HEAD_DIM, ATT_KV_HEADS * ATT_HEAD_DIM,
              IDX_HEADS * IDX_DIM, IDX_DIM, IDX_HEADS)
HGRN_SPLITS = (D_HGRN, D_HGRN, D_HGRN, D_HGRN)
GATE_SPLITS = (N_BRANCH * D_MODEL,)
RWKV_COLS = sum(RWKV_SPLITS)
REST_SPLITS = ATT_SPLITS + HGRN_SPLITS + GATE_SPLITS
N_IN = RWKV_COLS + sum(REST_SPLITS)

kernel_name = 'hybrid_rwkv7_dsa_hgrn2_step'


def rms_norm(x, g):
    xf = x.astype(jnp.float32)
    y = xf * lax.rsqrt(jnp.mean(xf * xf, axis=-1, keepdims=True) + NORM_EPS)
    return (y * g.astype(jnp.float32)).astype(x.dtype)


def swiglu(h, w_up, w_down):
    gate, up = jnp.split(h @ w_up, 2, axis=-1)
    return (jax.nn.silu(gate) * up) @ w_down


def split_cols(c, sizes):
    return jnp.split(c, np.cumsum(sizes)[:-1].tolist(), axis=-1)


def t5_bucket(dist):
    n = jnp.maximum(dist, 0)
    max_exact = NUM_BUCKETS // 2
    nf = jnp.maximum(n, 1).astype(jnp.float32)
    large = max_exact + (jnp.log(nf / max_exact) / math.log(MAX_DISTANCE / max_exact)
                         * (NUM_BUCKETS - max_exact)).astype(jnp.int32)
    large = jnp.minimum(large, NUM_BUCKETS - 1)
    return jnp.where(n < max_exact, n, large)


def indexer_scores(qi, wi, ki, q_pos):
    L = ki.shape[1]
    dots = jnp.einsum('bqhd,bld->bqhl', qi, ki).astype(jnp.float32) * IDX_DIM ** -0.5
    s = jnp.einsum('bqh,bqhl->bql', wi.astype(jnp.float32) * IDX_HEADS ** -0.5, jax.nn.relu(dots))
    visible = jnp.arange(L)[None, :] <= q_pos[:, None]
    return jnp.where(visible[None], s, NEG_INF)


def sparse_attn(q, k_sel, v_sel, idx, q_pos, rel_bias):
    B, T, H, hd = q.shape
    n_sel = idx.shape[-1]
    rep = H // ATT_KV_HEADS
    qg = q.reshape(B, T, ATT_KV_HEADS, rep, hd)
    logits = jnp.einsum('btgrd,btkgd->btgrk', qg, k_sel).astype(jnp.float32) * hd ** -0.5
    dist = q_pos[None, :, None] - idx
    bias = rel_bias[t5_bucket(dist)].astype(jnp.float32)
    bias = jnp.transpose(bias.reshape(B, T, n_sel, ATT_KV_HEADS, rep), (0, 1, 3, 4, 2))
    logits = jnp.where((dist >= 0)[:, :, None, None, :], logits + bias, NEG_INF)
    p = jax.nn.softmax(logits, axis=-1).astype(v_sel.dtype)
    o = jnp.einsum('btgrk,btkgd->btgrd', p, v_sel)
    return o.reshape(B, T, H * hd)


def dsa_prompt(q, k, v, qi, ki, wi, rel_bias):
    B, T = q.shape[:2]
    qb = min(Q_BLOCK, T)
    nb = T // qb
    n_sel = min(TOPK_MAX, T // 4)
    bidx = jnp.arange(B)[:, None, None]

    def block(args):
        q_b, qi_b, wi_b, pos = args
        s = indexer_scores(qi_b, wi_b, ki, pos)
        _, idx = lax.top_k(s, n_sel)
        return sparse_attn(q_b, k[bidx, idx], v[bidx, idx], idx, pos, rel_bias)

    blocks = lambda t: jnp.swapaxes(t.reshape(B, nb, qb, *t.shape[2:]), 0, 1)
    o = lax.map(block, (blocks(q), blocks(qi), blocks(wi), jnp.arange(T).reshape(nb, qb)))
    return jnp.swapaxes(o, 0, 1).reshape(B, T, -1)


def dsa_sample(q, k, v, qi, ki, wi, rel_bias, k_pool, v_pool, ki_pool, page_table):
    DB, T = q.shape[:2]
    n_pages = page_table.shape[1]
    ki_past = ki_pool[page_table].reshape(DB, n_pages * PAGE_SIZE, IDX_DIM)
    ki_all = jnp.concatenate([ki_past.astype(ki.dtype), ki], axis=1)
    L = PAST_LEN + T
    q_pos = PAST_LEN + jnp.arange(T)
    s = indexer_scores(qi, wi, ki_all, q_pos)
    _, idx = lax.top_k(s, min(TOPK_MAX, L // 4))
    bidx = jnp.arange(DB)[:, None, None]
    page = page_table[bidx, jnp.minimum(idx, PAST_LEN - 1) // PAGE_SIZE]
    off = idx % PAGE_SIZE
    nidx = jnp.clip(idx - PAST_LEN, 0, T - 1)
    is_past = (idx < PAST_LEN)[..., None, None]
    k_sel = jnp.where(is_past, k_pool[page, off].astype(k.dtype), k[bidx, nidx])
    v_sel = jnp.where(is_past, v_pool[page, off].astype(v.dtype), v[bidx, nidx])
    return sparse_attn(q, k_sel, v_sel, idx, q_pos, rel_bias)


def rwkv7_recurrence(r, w, k, v, a, b, s0):
    def step(s, inp):
        rt, wt, kt, vt, at, bt = inp
        sa = jnp.einsum('bhij,bhj->bhi', s, at)
        s = s * wt[:, :, None, :] + sa[..., :, None] * bt[..., None, :] + vt[..., :, None] * kt[..., None, :]
        return s, jnp.einsum('bhij,bhj->bhi', s, rt)
    xs = tuple(jnp.moveaxis(t, 1, 0) for t in (r, w, k, v, a, b))
    s, y = lax.scan(step, s0, xs)
    return jnp.moveaxis(y, 0, 1), s


def hgrn2_chunked(q, k, v, logf, s0, chunk):
    B, T, H, K = q.shape
    n = T // chunk
    to_chunks = lambda t: jnp.moveaxis(t.reshape(B, n, chunk, H, t.shape[-1]), 1, 0)
    causal = jnp.tril(jnp.ones((chunk, chunk), dtype=bool))[None, :, :, None, None]

    def step(s, inp):
        qc, kc, vc, gc = inp
        b = jnp.cumsum(gc, axis=1)
        diff = jnp.where(causal, b[:, :, None] - b[:, None, :], NEG_INF)
        att = jnp.einsum('bthk,btshk,bshk->bhts', qc, jnp.exp(diff), kc)
        o = jnp.einsum('bhts,bshv->bthv', att, vc) + jnp.einsum('bthk,bhkv->bthv', qc * jnp.exp(b), s)
        b_end = b[:, -1]
        s = jnp.exp(b_end)[..., None] * s + jnp.einsum('bshk,bshv->bhkv', kc * jnp.exp(b_end[:, None] - b), vc)
        return s, o

    s, o = lax.scan(step, s0, (to_chunks(q), to_chunks(k), to_chunks(v), to_chunks(logf)))
    return jnp.moveaxis(o, 0, 1).reshape(B, T, H, v.shape[-1]), s


def hgrn_lower_bounds(lb_logits):
    p = jax.nn.softmax(lb_logits.astype(jnp.float32), axis=0)
    return jnp.cumsum(p, axis=0) - p[0:1]


def cross_attn(h, mk, mv, wq, wo):
    B, T, _ = h.shape
    q = (h @ wq).reshape(B, T, MEM_HEADS, MEM_HEAD_DIM)
    logits = jnp.einsum('bthd,bshd->bhts', q, mk.astype(q.dtype)).astype(jnp.float32) * MEM_HEAD_DIM ** -0.5
    p = jax.nn.softmax(logits, axis=-1).astype(q.dtype)
    return jnp.einsum('bhts,bshd->bthd', p, mv.astype(q.dtype)).reshape(B, T, D_MODEL) @ wo


def token_mix(h, lw, lb, shift_prev, s_rwkv0, s_hgrn0, attend):
    f32 = jnp.float32
    B, T, _ = h.shape
    c = h @ lw['mix_w_in']
    c_rw, c_rest = c[..., :RWKV_COLS], c[..., RWKV_COLS:]
    aq, ak, av, aqi, aki, awi, hq, hf, hi, hg, gates = split_cols(c_rest, REST_SPLITS)

    prev = jnp.concatenate([shift_prev[:, None].astype(c.dtype), c_rw[:, :-1]], axis=1)
    c_sh = c_rw + (prev - c_rw) * lw['rwkv_mu']
    r, k, v, xw, xa, xg = split_cols(c_sh, RWKV_SPLITS)
    w_log = -jax.nn.softplus(-(lw['rwkv_w0'] + jnp.tanh(xw) @ lw['rwkv_w_up']).astype(f32)) - 0.5
    decay = jnp.exp(-jnp.exp(w_log))
    a_lr = jax.nn.sigmoid((lw['rwkv_a0'] + xa @ lw['rwkv_a_up']).astype(f32))
    g_rw = (jax.nn.sigmoid(xg) @ lw['rwkv_g_up']).astype(f32)
    heads = lambda t: t.reshape(B, T, RWKV_HEADS, RWKV_HEAD)
    kk = heads(k.astype(f32) * lw['rwkv_k_k'].astype(f32))
    kk = kk / jnp.maximum(jnp.sqrt(jnp.sum(kk * kk, axis=-1, keepdims=True)), 1e-12)
    k_mod = k.astype(f32) * (1.0 + (a_lr - 1.0) * lw['rwkv_k_a'].astype(f32))
    rh, kh, vh, ah = heads(r.astype(f32)), heads(k_mod), heads(v.astype(f32)), heads(a_lr)
    y, s_rwkv = rwkv7_recurrence(rh, heads(decay), kh, vh, -kk, kk * ah, s_rwkv0.astype(f32))
    mu = jnp.mean(y, axis=-1, keepdims=True)
    var = jnp.mean(jnp.square(y - mu), axis=-1, keepdims=True)
    y = ((y - mu) * lax.rsqrt(var + RWKV_LN_EPS)).reshape(B, T, D_RWKV)
    y = heads(y * lw['rwkv_ln_w'].astype(f32) + lw['rwkv_ln_b'].astype(f32))
    y = y + jnp.sum(rh * kh * lw['rwkv_r_k'].astype(f32), axis=-1, keepdims=True) * vh
    o_a = (y.reshape(B, T, D_RWKV) * g_rw).astype(h.dtype)

    k_rows = ak.reshape(B, T, ATT_KV_HEADS, ATT_HEAD_DIM)
    v_rows = av.reshape(B, T, ATT_KV_HEADS, ATT_HEAD_DIM)
    o_b = attend(aq.reshape(B, T, ATT_HEADS, ATT_HEAD_DIM), k_rows, v_rows,
                 aqi.reshape(B, T, IDX_HEADS, IDX_DIM), aki, awi)

    z = hf.astype(f32)
    logf = jnp.logaddexp(jnp.log(jnp.maximum(lb, LB_FLOOR)), jnp.log1p(-lb) + jax.nn.log_sigmoid(z))
    k_h = (1.0 - lb) * jax.nn.sigmoid(-z)
    hh = lambda t: t.reshape(B, T, HGRN_HEADS, HGRN_EXPAND)
    o_c, s_hgrn = hgrn2_chunked(hh(jax.nn.silu(hq.astype(f32))), hh(k_h), hh(hi.astype(f32)), hh(logf),
                                s_hgrn0.astype(f32), math.gcd(T, HGRN_CHUNK))
    o_c = o_c * lax.rsqrt(jnp.mean(o_c * o_c, axis=-1, keepdims=True) + NORM_EPS)
    o_c = (o_c.reshape(B, T, D_HGRN) * lw['hgrn_norm_w'].astype(f32) * jax.nn.silu(hg.astype(f32))).astype(h.dtype)

    g = jax.nn.sigmoid(gates.astype(f32)).reshape(B, T, N_BRANCH, D_MODEL)
    m = (g[:, :, 0] * (o_a @ lw['mix_w_proj_a']).astype(f32)
         + g[:, :, 1] * (o_b @ lw['mix_w_proj_b']).astype(f32)
         + g[:, :, 2] * (o_c @ lw['mix_w_proj_c']).astype(f32))
    out = m.astype(h.dtype) @ lw['mix_w_out']
    dt = h.dtype
    return out, (c_rw[:, -1], s_rwkv.astype(dt), s_hgrn.astype(dt), k_rows, v_rows, aki)


def layer_forward(x, lw, lb, mem_k, mem_v, shift_prev, s_rwkv0, s_hgrn0, attend):
    x = x + 0.5 * rms_norm(swiglu(rms_norm(x, lw['ffn1_norm_pre']), lw['ffn1_w_up'], lw['ffn1_w_down']),
                           lw['ffn1_norm_post'])
    y, st = token_mix(rms_norm(x, lw['mix_norm_pre']), lw, lb, shift_prev, s_rwkv0, s_hgrn0, attend)
    x = x + rms_norm(y, lw['mix_norm_post'])
    x = x + rms_norm(cross_attn(rms_norm(x, lw['cross_norm_pre']), mem_k, mem_v, lw['cross_wq'], lw['cross_wo']),
                     lw['cross_norm_post'])
    x = x + 0.5 * rms_norm(swiglu(rms_norm(x, lw['ffn2_norm_pre']), lw['ffn2_w_up'], lw['ffn2_w_down']),
                           lw['ffn2_norm_post'])
    return x, st


def setup_inputs(seed: int = 0) -> dict:
    key = jax.random.key(seed)
    keys = iter(jax.random.split(key, 64))

    def normal(shape, scale=1.0):
        return scale * jax.random.normal(next(keys), shape, jnp.float32)

    def gain(shape, center=1.0):
        return center + 0.05 * jax.random.normal(next(keys), shape, jnp.float32)

    def unif(shape, lo, hi):
        return jax.random.uniform(next(keys), shape, jnp.float32, lo, hi)

    n_pages = PAST_LEN // PAGE_SIZE
    n_pool = (DEC_BATCH * n_pages * 5 + 3) // 4
    D = D_MODEL
    return {
        'x_prompt': normal((BATCH, SEQ, D)),
        'x_sample': normal((DEC_BATCH, DEC_SEQ, D)),
        'cache_k': normal((DEPTH, n_pool, PAGE_SIZE, ATT_KV_HEADS, ATT_HEAD_DIM)),
        'cache_v': normal((DEPTH, n_pool, PAGE_SIZE, ATT_KV_HEADS, ATT_HEAD_DIM)),
        'cache_kidx': normal((DEPTH, n_pool, PAGE_SIZE, IDX_DIM)),
        'cache_mem_k': normal((DEPTH, DEC_BATCH, MEM_TOKENS, MEM_HEADS, MEM_HEAD_DIM)),
        'cache_mem_v': normal((DEPTH, DEC_BATCH, MEM_TOKENS, MEM_HEADS, MEM_HEAD_DIM)),
        'state_rwkv': normal((DEPTH, DEC_BATCH, RWKV_HEADS, RWKV_HEAD, RWKV_HEAD), 0.5),
        'state_rwkv_shift': normal((DEPTH, DEC_BATCH, RWKV_COLS)),
        'state_hgrn': normal((DEPTH, DEC_BATCH, HGRN_HEADS, HGRN_EXPAND, HGRN_EXPAND), 0.5),
        'page_table': jax.random.permutation(next(keys), n_pool)[:DEC_BATCH * n_pages]
                      .reshape(DEC_BATCH, n_pages).astype(jnp.int32),
        'mem_prompt': normal((BATCH, MEM_TOKENS, D)),
        'rel_bias': normal((NUM_BUCKETS, ATT_HEADS), 0.5),
        'hgrn_lb_logits': normal((DEPTH, D_HGRN), 0.5),
        'ffn1_norm_pre': gain((DEPTH, D)),
        'ffn1_norm_post': gain((DEPTH, D)),
        'ffn1_w_up': normal((DEPTH, D, 2 * D_FF), D ** -0.5),
        'ffn1_w_down': normal((DEPTH, D_FF, D), D_FF ** -0.5),
        'mix_norm_pre': gain((DEPTH, D)),
        'mix_norm_post': gain((DEPTH, D)),
        'mix_w_in': normal((DEPTH, D, N_IN), D ** -0.5),
        'rwkv_mu': unif((DEPTH, RWKV_COLS), 0.0, 1.0),
        'rwkv_w0': unif((DEPTH, D_RWKV), -6.0, 1.0),
        'rwkv_w_up': normal((DEPTH, R_DECAY, D_RWKV), R_DECAY ** -0.5),
        'rwkv_a0': normal((DEPTH, D_RWKV), 0.1),
        'rwkv_a_up': normal((DEPTH, R_AAA, D_RWKV), R_AAA ** -0.5),
        'rwkv_g_up': normal((DEPTH, R_GATE, D_RWKV), R_GATE ** -0.5),
        'rwkv_k_k': gain((DEPTH, D_RWKV), 0.85),
        'rwkv_k_a': gain((DEPTH, D_RWKV)),
        'rwkv_r_k': normal((DEPTH, RWKV_HEADS, RWKV_HEAD), 0.1),
        'rwkv_ln_w': gain((DEPTH, D_RWKV)),
        'rwkv_ln_b': normal((DEPTH, D_RWKV), 0.02),
        'hgrn_norm_w': gain((DEPTH, D_HGRN)),
        'mix_w_proj_a': normal((DEPTH, D_RWKV, D), D_RWKV ** -0.5),
        'mix_w_proj_b': normal((DEPTH, D_ATT, D), D_ATT ** -0.5),
        'mix_w_proj_c': normal((DEPTH, D_HGRN, D), D_HGRN ** -0.5),
        'mix_w_out': normal((DEPTH, D, D), D ** -0.5),
        'cross_norm_pre': gain((DEPTH, D)),
        'cross_norm_post': gain((DEPTH, D)),
        'cross_wq': normal((DEPTH, D, D), D ** -0.5),
        'cross_wk': normal((DEPTH, D, D), D ** -0.5),
        'cross_wv': normal((DEPTH, D, D), D ** -0.5),
        'cross_wo': normal((DEPTH, D, D), D ** -0.5),
        'ffn2_norm_pre': gain((DEPTH, D)),
        'ffn2_norm_post': gain((DEPTH, D)),
        'ffn2_w_up': normal((DEPTH, D, 2 * D_FF), D ** -0.5),
        'ffn2_w_down': normal((DEPTH, D_FF, D), D_FF ** -0.5),
    }


def reference(x_prompt, x_sample, cache_k, cache_v, cache_kidx, cache_mem_k, cache_mem_v,
              state_rwkv, state_rwkv_shift, state_hgrn, page_table, mem_prompt,
              rel_bias, hgrn_lb_logits,
              ffn1_norm_pre, ffn1_norm_post, ffn1_w_up, ffn1_w_down,
              mix_norm_pre, mix_norm_post, mix_w_in,
              rwkv_mu, rwkv_w0, rwkv_w_up, rwkv_a0, rwkv_a_up, rwkv_g_up,
              rwkv_k_k, rwkv_k_a, rwkv_r_k, rwkv_ln_w, rwkv_ln_b, hgrn_norm_w,
              mix_w_proj_a, mix_w_proj_b, mix_w_proj_c, mix_w_out,
              cross_norm_pre, cross_norm_post, cross_wq, cross_wk, cross_wv, cross_wo,
              ffn2_norm_pre, ffn2_norm_post, ffn2_w_up, ffn2_w_down):
    layer_weights = dict(
        ffn1_norm_pre=ffn1_norm_pre, ffn1_norm_post=ffn1_norm_post, ffn1_w_up=ffn1_w_up, ffn1_w_down=ffn1_w_down,
        mix_norm_pre=mix_norm_pre, mix_norm_post=mix_norm_post, mix_w_in=mix_w_in,
        rwkv_mu=rwkv_mu, rwkv_w0=rwkv_w0, rwkv_w_up=rwkv_w_up, rwkv_a0=rwkv_a0, rwkv_a_up=rwkv_a_up,
        rwkv_g_up=rwkv_g_up, rwkv_k_k=rwkv_k_k, rwkv_k_a=rwkv_k_a, rwkv_r_k=rwkv_r_k,
        rwkv_ln_w=rwkv_ln_w, rwkv_ln_b=rwkv_ln_b, hgrn_norm_w=hgrn_norm_w,
        mix_w_proj_a=mix_w_proj_a, mix_w_proj_b=mix_w_proj_b, mix_w_proj_c=mix_w_proj_c, mix_w_out=mix_w_out,
        cross_norm_pre=cross_norm_pre, cross_norm_post=cross_norm_post, cross_wq=cross_wq, cross_wo=cross_wo,
        ffn2_norm_pre=ffn2_norm_pre, ffn2_norm_post=ffn2_norm_post, ffn2_w_up=ffn2_w_up, ffn2_w_down=ffn2_w_down)
    lower_bounds = hgrn_lower_bounds(hgrn_lb_logits)
    B = x_prompt.shape[0]
    dt = x_prompt.dtype
    zero_shift = jnp.zeros((B, RWKV_COLS), dt)
    zero_rwkv = jnp.zeros((B, RWKV_HEADS, RWKV_HEAD, RWKV_HEAD), dt)
    zero_hgrn = jnp.zeros((B, HGRN_HEADS, HGRN_EXPAND, HGRN_EXPAND), dt)
    attend_prompt = functools.partial(dsa_prompt, rel_bias=rel_bias)

    xp, xs = x_prompt, x_sample
    kp, vp, kip, mkp, mvp, rwp, shp, hgp = [], [], [], [], [], [], [], []
    ks, vs, kis, rws, shs, hgs = [], [], [], [], [], []
    for l in range(DEPTH):
        lw = {name: w[l] for name, w in layer_weights.items()}
        mem_k = (mem_prompt @ cross_wk[l]).reshape(B, -1, MEM_HEADS, MEM_HEAD_DIM)
        mem_v = (mem_prompt @ cross_wv[l]).reshape(B, -1, MEM_HEADS, MEM_HEAD_DIM)
        xp, st = layer_forward(xp, lw, lower_bounds[l], mem_k, mem_v,
                               zero_shift, zero_rwkv, zero_hgrn, attend_prompt)
        shp.append(st[0]); rwp.append(st[1]); hgp.append(st[2])
        kp.append(st[3]); vp.append(st[4]); kip.append(st[5]); mkp.append(mem_k); mvp.append(mem_v)
        attend_sample = functools.partial(dsa_sample, rel_bias=rel_bias, k_pool=cache_k[l], v_pool=cache_v[l],
                                          ki_pool=cache_kidx[l], page_table=page_table)
        xs, st = layer_forward(xs, lw, lower_bounds[l], cache_mem_k[l], cache_mem_v[l],
                               state_rwkv_shift[l], state_rwkv[l], state_hgrn[l], attend_sample)
        shs.append(st[0]); rws.append(st[1]); hgs.append(st[2])
        ks.append(st[3]); vs.append(st[4]); kis.append(st[5])

    stack = lambda rows: jnp.stack(rows, axis=0)
    return (xp, xs,
            stack(kp), stack(vp), stack(kip), stack(mkp), stack(mvp), stack(rwp), stack(shp), stack(hgp),
            stack(ks), stack(vs), stack(kis), stack(rws), stack(shs), stack(hgs))
```

```python
import functools
import math

import jax
import jax.numpy as jnp
import numpy as np
from jax import lax
from jax.experimental import pallas as pl
from jax.experimental.pallas import tpu as pltpu

F32 = jnp.float32
BF16 = jnp.bfloat16
I32 = jnp.int32

LANES = 128
SUBLANES = 8
VMEM_LIMIT_BYTES = 56 * 1024 * 1024

NORM_EPS = 1e-6
RWKV_LN_EPS = 64e-5
LB_FLOOR = 1e-30
NEG_INF = -1e30
INT_MIN = int(np.iinfo(np.int32).min)

RWKV_HEAD = 64
RWKV_HEADS = 8
RWKV_GROUP = 4
HGRN_HEADS = 4
HGRN_EXPAND = 128
ATT_HEADS = 8
ATT_KV_HEADS = 2
ATT_HEAD_DIM = 64
IDX_HEADS = 4
IDX_DIM = 64
TOPK = 256
Q_BLOCK = 128
MEM_HEADS = 4
NUM_BUCKETS = 32
MAX_DISTANCE = 128
PAGE = 128

ATT_Q0, ATT_K0, ATT_V0, ATT_QI0, ATT_KI0, ATT_WI0, ATT_W = 0, 512, 640, 768, 1024, 1088, 1152


def _cparams(sem):
    return pltpu.CompilerParams(dimension_semantics=sem, vmem_limit_bytes=VMEM_LIMIT_BYTES)


def _mm(a, b):
    return jnp.dot(a.astype(BF16), b.astype(BF16), preferred_element_type=F32)


def _mm_nt(a, b):
    return lax.dot_general(a.astype(BF16), b.astype(BF16), (((1,), (1,)), ((), ())),
                           preferred_element_type=F32)


def _mm_tn(a, b):
    return lax.dot_general(a.astype(BF16), b.astype(BF16), (((0,), (0,)), ((), ())),
                           preferred_element_type=F32)


def _bf16_round(x):
    return x.astype(BF16).astype(F32)


def _split3(x):
    hi = x.astype(BF16)
    r1 = x - hi.astype(F32)
    mid = r1.astype(BF16)
    lo = (r1 - mid.astype(F32)).astype(BF16)
    return hi, mid, lo


def _mm_exact_lhs(a01, x):
    a = a01.astype(BF16)
    hi, mid, lo = _split3(x)
    d = lambda p: jnp.dot(a, p, preferred_element_type=F32)
    return d(hi) + d(mid) + d(lo)


def _mm_exact_rhs(x, b01):
    b = b01.astype(BF16)
    hi, mid, lo = _split3(x)
    d = lambda p: jnp.dot(p, b, preferred_element_type=F32)
    return d(hi) + d(mid) + d(lo)


def _rms(x, g):
    return x * lax.rsqrt(jnp.mean(x * x, axis=-1, keepdims=True) + NORM_EPS) * g


def _sigmoid(x):
    return jax.nn.sigmoid(x)


def _softplus(x):
    return jnp.maximum(x, 0.0) + jnp.log1p(jnp.exp(-jnp.abs(x)))


def _full(shape):
    nd = len(shape)
    return pl.BlockSpec(shape, lambda *_: (0,) * nd)


def _row_tile(n, want):
    t = min(n, want)
    assert n % t == 0, (n, t)
    return t


def _ffn_kernel(x_ref, gpre_ref, wup_ref, wdn_ref, gpost_ref, o_ref, *, d_ff, tf):
    x = x_ref[...]
    h = _rms(x, gpre_ref[...]).astype(BF16)
    acc = jnp.zeros(x.shape, F32)
    for f0 in range(0, d_ff, tf):
        gate = jnp.dot(h, wup_ref[:, f0:f0 + tf], preferred_element_type=F32)
        up = jnp.dot(h, wup_ref[:, d_ff + f0:d_ff + f0 + tf], preferred_element_type=F32)
        act = (gate * _sigmoid(gate) * up).astype(BF16)
        acc = acc + jnp.dot(act, wdn_ref[f0:f0 + tf, :], preferred_element_type=F32)
    o_ref[...] = x + 0.5 * _rms(acc, gpost_ref[...])


def _ffn(x, gpre, wup, wdn, gpost, tm):
    n, d = x.shape
    d_ff = wdn.shape[0]
    tf = d_ff // 2 if (d_ff // 2) % LANES == 0 else d_ff
    tm = _row_tile(n, tm)
    return pl.pallas_call(
        functools.partial(_ffn_kernel, d_ff=d_ff, tf=tf),
        grid=(n // tm,),
        in_specs=[pl.BlockSpec((tm, d), lambda i: (i, 0)), _full((1, d)), _full(wup.shape), _full(wdn.shape),
                  _full((1, d))],
        out_specs=pl.BlockSpec((tm, d), lambda i: (i, 0)),
        out_shape=jax.ShapeDtypeStruct((n, d), F32),
        compiler_params=_cparams(("parallel",)),
        name="ffn",
    )(x, gpre, wup, wdn, gpost)


def _proj_kernel(*refs, n_out, norm, tn):
    x_ref, g_ref = refs[0], refs[1]
    w_refs = refs[2:2 + n_out]
    o_refs = refs[2 + n_out:]
    x = x_ref[...]
    h = (_rms(x, g_ref[...]) if norm else x).astype(BF16)
    for w_ref, o_ref in zip(w_refs, o_refs):
        n = w_ref.shape[1]
        step = tn if n % tn == 0 else n
        for n0 in range(0, n, step):
            o_ref[:, n0:n0 + step] = jnp.dot(h, w_ref[:, n0:n0 + step], preferred_element_type=F32)


def _proj(x, g, ws, norm, tm, name):
    n, d = x.shape
    tm = _row_tile(n, tm)
    return pl.pallas_call(
        functools.partial(_proj_kernel, n_out=len(ws), norm=norm, tn=512),
        grid=(n // tm,),
        in_specs=[pl.BlockSpec((tm, d), lambda i: (i, 0)), _full((1, d))] + [_full(w.shape) for w in ws],
        out_specs=[pl.BlockSpec((tm, w.shape[1]), lambda i: (i, 0)) for w in ws],
        out_shape=[jax.ShapeDtypeStruct((n, w.shape[1]), F32) for w in ws],
        compiler_params=_cparams(("parallel",)),
        name=name,
    )(x, g, *ws)


def _out_kernel(x_ref, a_ref, w_ref, g_ref, o_ref):
    y = jnp.dot(a_ref[...].astype(BF16), w_ref[...], preferred_element_type=F32)
    o_ref[...] = x_ref[...] + _rms(y, g_ref[...])


def _out_proj(x, a, w, g, tm):
    n, d = x.shape
    tm = _row_tile(n, tm)
    return pl.pallas_call(
        _out_kernel,
        grid=(n // tm,),
        in_specs=[pl.BlockSpec((tm, d), lambda i: (i, 0)), pl.BlockSpec((tm, a.shape[1]), lambda i: (i, 0)),
                  _full(w.shape), _full((1, d))],
        out_specs=pl.BlockSpec((tm, d), lambda i: (i, 0)),
        out_shape=jax.ShapeDtypeStruct((n, d), F32),
        compiler_params=_cparams(("parallel",)),
        name="out_proj",
    )(x, a, w, g)


def _merge_kernel(x_ref, oa_ref, ob_ref, oc_ref, gt_ref, wa_ref, wb_ref, wc_ref, wo_ref, g_ref, o_ref, *, d):
    m = jnp.zeros((x_ref.shape[0], d), F32)
    for j, (o_r, w_r) in enumerate(((oa_ref, wa_ref), (ob_ref, wb_ref), (oc_ref, wc_ref))):
        p = jnp.dot(o_r[...].astype(BF16), w_r[...], preferred_element_type=F32)
        m = m + _sigmoid(gt_ref[:, j * d:(j + 1) * d]) * p
    y = jnp.dot(m.astype(BF16), wo_ref[...], preferred_element_type=F32)
    o_ref[...] = x_ref[...] + _rms(y, g_ref[...])


def _merge(x, oa, ob, oc, gates, wa, wb, wc, wo, g, tm):
    n, d = x.shape
    tm = _row_tile(n, tm)
    row = lambda w: pl.BlockSpec((tm, w), lambda i: (i, 0))
    return pl.pallas_call(
        functools.partial(_merge_kernel, d=d),
        grid=(n // tm,),
        in_specs=[row(d), row(oa.shape[1]), row(ob.shape[1]), row(oc.shape[1]), row(gates.shape[1]),
                  _full(wa.shape), _full(wb.shape), _full(wc.shape), _full(wo.shape), _full((1, d))],
        out_specs=row(d),
        out_shape=jax.ShapeDtypeStruct((n, d), F32),
        compiler_params=_cparams(("parallel",)),
        name="merge",
    )(x, oa, ob, oc, gates, wa, wb, wc, wo, g)


def _xattn_kernel(q_ref, mk_ref, mv_ref, o_ref, *, heads):
    hd = q_ref.shape[2] // heads
    for h in range(heads):
        sl = slice(h * hd, (h + 1) * hd)
        lg = _mm_nt(q_ref[0, :, sl], mk_ref[0, :, sl]) * hd ** -0.5
        lg = lg - jnp.max(lg, axis=-1, keepdims=True)
        p = jnp.exp(lg)
        p = p / jnp.sum(p, axis=-1, keepdims=True)
        o_ref[0, :, sl] = _mm(p, mv_ref[0, :, sl])


def _xattn(q, mem_k, mem_v, k_blk, v_blk, tq):
    b, t, d = q.shape
    s = mem_k.shape[1]
    tq = _row_tile(t, tq)
    return pl.pallas_call(
        functools.partial(_xattn_kernel, heads=MEM_HEADS),
        grid=(b, t // tq),
        in_specs=[pl.BlockSpec((1, tq, d), lambda i, j: (i, j, 0)),
                  pl.BlockSpec((1, s, d), lambda i, j: (i, 0, k_blk)),
                  pl.BlockSpec((1, s, d), lambda i, j: (i, 0, v_blk))],
        out_specs=pl.BlockSpec((1, tq, d), lambda i, j: (i, j, 0)),
        out_shape=jax.ShapeDtypeStruct((b, t, d), F32),
        compiler_params=_cparams(("parallel", "parallel")),
        name="xattn",
    )(q, mem_k, mem_v)


def _rwkv_prep_kernel(c_ref, sh_ref, mu_ref, w0_ref, wup_ref, a0_ref, aup_ref, gup_ref, kk_ref, ka_ref, hs_ref,
                      r_o, lw_o, k_o, v_o, kk_o, b_o, g_o, carry_ref, *, t_valid, d_r):
    j = pl.program_id(1)
    tm = c_ref.shape[1]

    @pl.when(j == 0)
    def _():
        carry_ref[...] = sh_ref[0]

    c = c_ref[0]
    row = lax.broadcasted_iota(I32, c.shape, 0)
    prev = jnp.where(row == 0, carry_ref[...], pltpu.roll(c, 1, axis=0))
    carry_ref[...] = c[tm - 1:tm, :]
    csh = c + (prev - c) * mu_ref[...]
    r, k, v = csh[:, 0:d_r], csh[:, d_r:2 * d_r], csh[:, 2 * d_r:3 * d_r]
    o = 3 * d_r
    n_w, n_a, n_g = wup_ref.shape[0], aup_ref.shape[0], gup_ref.shape[0]
    xw, xa, xg = csh[:, o:o + n_w], csh[:, o + n_w:o + n_w + n_a], csh[:, o + n_w + n_a:o + n_w + n_a + n_g]
    w_log = -_softplus(-(w0_ref[...] + _mm(jnp.tanh(xw), wup_ref[...]))) - 0.5
    lw = -jnp.exp(w_log)
    a_lr = _sigmoid(a0_ref[...] + _mm(xa, aup_ref[...]))
    g = _mm(_sigmoid(xg), gup_ref[...])
    kk = k * kk_ref[...]
    ss = _mm_exact_rhs(kk * kk, hs_ref[...])
    kk = kk / jnp.maximum(jnp.sqrt(ss), 1e-12)
    k_mod = k * (1.0 + (a_lr - 1.0) * ka_ref[...])
    b = kk * a_lr
    if t_valid < tm:
        ok = (lax.broadcasted_iota(I32, (tm, d_r), 0) + j * tm) < t_valid
        z = lambda t: jnp.where(ok, t, 0.0)
        lw, k_mod, v, kk, b = z(lw), z(k_mod), z(v), z(kk), z(b)
    for h in range(RWKV_HEADS):
        sl = slice(h * RWKV_HEAD, (h + 1) * RWKV_HEAD)
        for o_ref, val in ((r_o, r), (lw_o, lw), (k_o, k_mod), (v_o, v), (kk_o, kk), (b_o, b), (g_o, g)):
            o_ref[0, h] = val[:, sl]


def _rwkv_prep(c_rw, shift, p, t_valid, tm):
    b, t, cols = c_rw.shape
    d_r = p["rwkv_w0"].shape[1]
    tm = _row_tile(t, tm)
    hm = jax.ShapeDtypeStruct((b, RWKV_HEADS, t, RWKV_HEAD), F32)
    hm_spec = pl.BlockSpec((1, RWKV_HEADS, tm, RWKV_HEAD), lambda i, j: (i, 0, j, 0))
    params = [p["rwkv_mu"], p["rwkv_w0"], p["rwkv_w_up"], p["rwkv_a0"], p["rwkv_a_up"], p["rwkv_g_up"],
              p["rwkv_k_k"], p["rwkv_k_a"], p["head_sum"]]
    return pl.pallas_call(
        functools.partial(_rwkv_prep_kernel, t_valid=t_valid, d_r=d_r),
        grid=(b, t // tm),
        in_specs=[pl.BlockSpec((1, tm, cols), lambda i, j: (i, j, 0)),
                  pl.BlockSpec((1, 1, cols), lambda i, j: (i, 0, 0))] + [_full(a.shape) for a in params],
        out_specs=[hm_spec] * 7,
        out_shape=[hm] * 7,
        scratch_shapes=[pltpu.VMEM((1, cols), F32)],
        compiler_params=_cparams(("parallel", "arbitrary")),
        name="rwkv_prep",
    )(c_rw, shift, *params)


def _rwkv_kernel(r_ref, lw_ref, k_ref, v_ref, kk_ref, b_ref, g_ref, s0_ref, lnw_ref, lnb_ref, rk_ref,
                 o_ref, so_ref, s_ref, *, chunk):
    t_idx = pl.program_id(2)
    nh, C, N = RWKV_GROUP, chunk, RWKV_HEAD
    R = nh * C

    @pl.when(t_idx == 0)
    def _():
        s_ref[...] = s0_ref[0]

    ld = lambda ref: ref[0].reshape(R, N)
    r, lw, k, v, kk, b, g = (ld(x) for x in (r_ref, lw_ref, k_ref, v_ref, kk_ref, b_ref, g_ref))
    row = lax.broadcasted_iota(I32, (R, R), 0)
    col = lax.broadcasted_iota(I32, (R, R), 1)
    same = (row // C) == (col // C)
    incl = jnp.logical_and(same, col <= row)
    strict = jnp.logical_and(same, col < row)
    cum = _mm_exact_lhs(jnp.where(incl, 1.0, 0.0), lw)
    e_neg = jnp.exp(-cum)
    at = -kk * jnp.exp(cum - lw)
    rt = r * jnp.exp(cum)
    bt = b * e_neg
    kt = k * e_neg
    gram = _mm_nt(jnp.concatenate([at, rt], axis=0), jnp.concatenate([bt, kt], axis=0))
    a_ab = jnp.where(strict, gram[:R, :R], 0.0)
    a_ak = jnp.where(strict, gram[:R, R:], 0.0)
    a_rb = jnp.where(incl, gram[R:, :R], 0.0)
    a_rk = jnp.where(incl, gram[R:, R:], 0.0)
    x = jnp.concatenate([at, _mm(a_ak, v)], axis=1)
    d_inv = jnp.where(row == col, 1.0, 0.0)
    m_blk = 1
    while m_blk < C:
        lower_left = jnp.logical_and(row // (2 * m_blk) == col // (2 * m_blk),
                                     jnp.logical_and((row // m_blk) % 2 == 1, (col // m_blk) % 2 == 0))
        a_off = jnp.where(lower_left, a_ab, 0.0)
        d_inv = d_inv + (a_off if m_blk == 1 else _mm(d_inv, _mm(a_off, d_inv)))
        m_blk *= 2
    x = _mm(d_inv, x)
    ax = _mm(a_rb, x)
    qt = rt + ax[:, :N]
    y0 = ax[:, N:] + _mm(a_rk, v)
    eye = jnp.where(lax.broadcasted_iota(I32, (N, N), 0) == lax.broadcasted_iota(I32, (N, N), 1), 1.0, 0.0)
    outs = []
    for h in range(nh):
        sl = slice(h * C, (h + 1) * C)
        p_c = jnp.exp(cum[h * C + C - 1:h * C + C, :])
        m = (eye + _mm_tn(x[sl, :N], bt[sl])) * p_c
        s_loc = _mm_tn(jnp.concatenate([x[sl, N:], v[sl]], axis=0),
                       jnp.concatenate([bt[sl], kt[sl]], axis=0)) * p_c
        s = s_ref[h]
        y = _mm_nt(qt[sl], s) + y0[sl]
        s_ref[h] = _mm(s, m) + s_loc
        mu = jnp.mean(y, axis=-1, keepdims=True)
        yc = y - mu
        var = jnp.mean(yc * yc, axis=-1, keepdims=True)
        yn = yc * lax.rsqrt(var + RWKV_LN_EPS) * lnw_ref[h] + lnb_ref[h]
        yn = yn + jnp.sum(r[sl] * k[sl] * rk_ref[h], axis=-1, keepdims=True) * v[sl]
        outs.append(yn * g[sl])
    o_ref[0] = jnp.concatenate(outs, axis=1)

    @pl.when(t_idx == pl.num_programs(2) - 1)
    def _():
        so_ref[0] = s_ref[...]


def _rwkv(streams, s0, lnw, lnb, rk, chunk):
    b, nh, t, n = streams[0].shape
    ng = nh // RWKV_GROUP
    st_spec = pl.BlockSpec((1, RWKV_GROUP, chunk, n), lambda i, gq, j: (i, gq, j, 0))
    s_spec = pl.BlockSpec((1, RWKV_GROUP, n, n), lambda i, gq, j: (i, gq, 0, 0))
    p_spec = pl.BlockSpec((RWKV_GROUP, 1, n), lambda i, gq, j: (gq, 0, 0))
    return pl.pallas_call(
        functools.partial(_rwkv_kernel, chunk=chunk),
        grid=(b, ng, t // chunk),
        in_specs=[st_spec] * 7 + [s_spec, p_spec, p_spec, p_spec],
        out_specs=[pl.BlockSpec((1, chunk, RWKV_GROUP * n), lambda i, gq, j: (i, j, gq)), s_spec],
        out_shape=[jax.ShapeDtypeStruct((b, t, nh * n), F32), jax.ShapeDtypeStruct((b, nh, n, n), F32)],
        scratch_shapes=[pltpu.VMEM((RWKV_GROUP, n, n), F32)],
        compiler_params=_cparams(("parallel", "parallel", "arbitrary")),
        name="rwkv",
    )(*streams, s0, lnw, lnb, rk)


def _hgrn_kernel(c_ref, lb_ref, nw_ref, s0_ref, o_ref, so_ref, st_ref, cum_ref, kh_ref, *, chunk, t_valid):
    j = pl.program_id(1)
    tb = c_ref.shape[1]
    d = lb_ref.shape[1]
    K = HGRN_EXPAND
    c = chunk

    @pl.when(j == 0)
    def _():
        for h in range(HGRN_HEADS):
            st_ref[h] = s0_ref[0, h].T

    z = c_ref[0, :, d:2 * d]
    lb = lb_ref[...]
    ls = -_softplus(-z)
    x1 = jnp.log(jnp.maximum(lb, LB_FLOOR))
    x2 = jnp.log1p(-lb) + ls
    logf = jnp.maximum(x1, x2) + jnp.log1p(jnp.exp(-jnp.abs(x1 - x2)))
    kh = (1.0 - lb) * _sigmoid(-z)
    if t_valid < tb:
        ok = (lax.broadcasted_iota(I32, (tb, d), 0) + j * tb) < t_valid
        logf = jnp.where(ok, logf, 0.0)
        kh = jnp.where(ok, kh, 0.0)
    row = lax.broadcasted_iota(I32, (tb, tb), 0)
    col = lax.broadcasted_iota(I32, (tb, tb), 1)
    tri = jnp.where(jnp.logical_and(row // c == col // c, col <= row), 1.0, 0.0)
    cum_ref[...] = _mm_exact_lhs(tri, logf)
    kh_ref[...] = kh

    ones = jnp.ones((K, K), BF16)
    rr = lax.broadcasted_iota(I32, (c * c, K), 0)
    causal = (rr % c) <= (rr // c)
    sel = jnp.where(lax.broadcasted_iota(I32, (c, c * c), 1) // c == lax.broadcasted_iota(I32, (c, c * c), 0),
                    1.0, 0.0).astype(BF16)

    def body(ci, carry):
        r0 = pl.multiple_of(ci * c, c)
        for h in range(HGRN_HEADS):
            hs = slice(h * K, (h + 1) * K)
            hq = c_ref[0, pl.ds(r0, c), h * K:(h + 1) * K]
            q = hq * _sigmoid(hq)
            v = c_ref[0, pl.ds(r0, c), 2 * d + h * K:2 * d + (h + 1) * K]
            hg = c_ref[0, pl.ds(r0, c), 3 * d + h * K:3 * d + (h + 1) * K]
            bq = cum_ref[pl.ds(r0, c), hs]
            kq = kh_ref[pl.ds(r0, c), hs]
            e = jnp.concatenate([q[t:t + 1] * kq * jnp.exp(jnp.minimum(bq[t:t + 1] - bq, 0.0)) for t in range(c)],
                                axis=0)
            att = jnp.dot(e.astype(BF16), ones, preferred_element_type=F32)
            w = jnp.where(causal, att, 0.0) * jnp.concatenate([v] * c, axis=0)
            st = st_ref[h]
            o = jnp.dot(sel, w.astype(BF16), preferred_element_type=F32) + _mm_nt(q * jnp.exp(bq), st)
            bl = bq[c - 1:c]
            st_ref[h] = st * jnp.exp(bl) + _mm_tn(v, kq * jnp.exp(bl - bq))
            o = o * lax.rsqrt(jnp.mean(o * o, axis=-1, keepdims=True) + NORM_EPS)
            o_ref[0, pl.ds(r0, c), hs] = o * nw_ref[:, hs] * (hg * _sigmoid(hg))
        return carry

    lax.fori_loop(0, tb // c, body, 0)

    @pl.when(j == pl.num_programs(1) - 1)
    def _():
        for h in range(HGRN_HEADS):
            so_ref[0, h] = st_ref[h].T


def _hgrn(c_hg, lb, nw, s0, chunk, t_valid, tb):
    b, t, cols = c_hg.shape
    d = cols // 4
    tb = _row_tile(t, tb)
    s_spec = pl.BlockSpec((1, HGRN_HEADS, HGRN_EXPAND, HGRN_EXPAND), lambda i, j: (i, 0, 0, 0))
    return pl.pallas_call(
        functools.partial(_hgrn_kernel, chunk=chunk, t_valid=t_valid),
        grid=(b, t // tb),
        in_specs=[pl.BlockSpec((1, tb, cols), lambda i, j: (i, j, 0)), _full((1, d)), _full((1, d)), s_spec],
        out_specs=[pl.BlockSpec((1, tb, d), lambda i, j: (i, j, 0)), s_spec],
        out_shape=[jax.ShapeDtypeStruct((b, t, d), F32), jax.ShapeDtypeStruct(s0.shape, F32)],
        scratch_shapes=[pltpu.VMEM((HGRN_HEADS, HGRN_EXPAND, HGRN_EXPAND), F32), pltpu.VMEM((tb, d), F32),
                        pltpu.VMEM((tb, d), F32)],
        compiler_params=_cparams(("parallel", "arbitrary")),
        name="hgrn",
    )(c_hg, lb, nw, s0)


def _ordered_key(s):
    bits = pltpu.bitcast(s + 0.0, I32)
    return bits ^ ((bits >> 31) & 0x7FFFFFFF)


def _kth_largest(count_ge, shape, k):
    c0 = count_ge(jnp.zeros(shape, I32))
    thr = jnp.where(c0 >= k, 0, INT_MIN).astype(I32)

    def body(i, thr):
        cand = thr | jnp.left_shift(jnp.int32(1), 30 - i)
        return jnp.where(count_ge(cand) >= k, cand, thr)

    return lax.fori_loop(0, 31, body, thr)


DSA_GROUP = 4


def _dsa_prompt_kernel(q_ref, qi_ref, kw_ref, k_ref, v_ref, ki_ref, bias_ref, o_ref,
                       key_ref, m_ref, l_ref, acc_ref, *, topk):
    i = pl.program_id(1)
    QB, HD = Q_BLOCK, ATT_HEAD_DIM
    zeros64 = jnp.zeros((HD, QB), F32)

    q_t = q_ref[0].T
    rep = ATT_HEADS // ATT_KV_HEADS
    tiles = []
    for h in range(ATT_HEADS):
        qh = q_t[h * HD:(h + 1) * HD]
        tiles.append(jnp.concatenate([qh, zeros64] if h // rep == 0 else [zeros64, qh], axis=0))
    qs_t = jnp.concatenate(tiles, axis=1).astype(BF16)
    qi_t = qi_ref[0].T
    qi_pad = jnp.concatenate([jnp.concatenate([qi_t[h * IDX_DIM:(h + 1) * IDX_DIM], zeros64], axis=0)
                              for h in range(IDX_HEADS)], axis=1).astype(BF16)
    w_t = kw_ref[0].T
    w_rows = [_bf16_round(w_t[IDX_DIM + h:IDX_DIM + h + 1] * IDX_HEADS ** -0.5) for h in range(IDX_HEADS)]

    row = lax.broadcasted_iota(I32, (QB, QB), 0)
    col = lax.broadcasted_iota(I32, (QB, QB), 1)
    n_grp = (i + DSA_GROUP) // DSA_GROUP

    def score_body(j, carry):
        r0 = pl.multiple_of(j * QB, QB)
        dots = jnp.dot(ki_ref[0, pl.ds(r0, QB), :].astype(BF16), qi_pad, preferred_element_type=F32) * IDX_DIM ** -0.5
        s = jnp.zeros((QB, QB), F32)
        for h in range(IDX_HEADS):
            s = s + _bf16_round(jnp.maximum(dots[:, h * QB:(h + 1) * QB], 0.0)) * w_rows[h]
        vis = (j * QB + row) <= (i * QB + col)
        key = _ordered_key(jnp.where(vis, s, NEG_INF))
        key_ref[pl.ds(r0, QB), :] = jnp.where(j <= i, key, INT_MIN)
        return carry

    lax.fori_loop(0, n_grp * DSA_GROUP, score_body, 0)

    def count_ge(cand):
        def grp(gi, acc):
            r0 = pl.multiple_of(gi * (DSA_GROUP * QB), DSA_GROUP * QB)
            blk = key_ref[pl.ds(r0, DSA_GROUP * QB), :]
            return acc + jnp.sum(jnp.where(blk >= cand, 1.0, 0.0), axis=0, keepdims=True)
        return lax.fori_loop(0, n_grp, grp, jnp.zeros((1, QB), F32))

    thr = _kth_largest(count_ge, (1, QB), topk)
    need = topk - count_ge(thr + 1)

    m_ref[...] = jnp.full(m_ref.shape, NEG_INF, F32)
    l_ref[...] = jnp.zeros(l_ref.shape, F32)
    acc_ref[...] = jnp.zeros(acc_ref.shape, F32)
    tri = jnp.where(col <= row, 1.0, 0.0).astype(BF16)

    def attn_body(j, taken):
        r0 = pl.multiple_of(j * QB, QB)
        key = key_ref[pl.ds(r0, QB), :]
        eq = key == thr
        prefix = jnp.dot(tri, jnp.where(eq, 1.0, 0.0).astype(BF16), preferred_element_type=F32)
        sel = jnp.logical_or(key > thr, jnp.logical_and(eq, taken + prefix <= need))
        vis = (j * QB + row) <= (i * QB + col)
        madd = jnp.where(jnp.logical_and(sel, vis), 0.0, NEG_INF)
        lg = jnp.dot(k_ref[0, pl.ds(r0, QB), :].astype(BF16), qs_t, preferred_element_type=F32) * HD ** -0.5
        lg = lg + bias_ref[jnp.minimum(i - j, 2)] + jnp.concatenate([madd] * ATT_HEADS, axis=1)
        m_old = m_ref[...]
        m_new = jnp.maximum(m_old, jnp.max(lg, axis=0, keepdims=True))
        alpha = jnp.exp(m_old - m_new)
        p = jnp.exp(lg - m_new)
        l_ref[...] = alpha * l_ref[...] + jnp.sum(p, axis=0, keepdims=True)
        acc_ref[...] = alpha * acc_ref[...] + _mm(v_ref[0, pl.ds(r0, QB), :].T, p)
        m_ref[...] = m_new
        return taken + prefix[QB - 1:QB, :]

    lax.fori_loop(0, i + 1, attn_body, jnp.zeros((1, QB), F32))

    out_t = (acc_ref[...] / l_ref[...]).T
    o_ref[0] = jnp.concatenate(
        [out_t[h * QB:(h + 1) * QB, (h // rep) * HD:(h // rep + 1) * HD] for h in range(ATT_HEADS)], axis=1)


def _dsa_prompt(c_att, bias_t):
    b, t, _ = c_att.shape
    nq = t // Q_BLOCK
    t_keys = ((nq + DSA_GROUP - 1) // DSA_GROUP) * DSA_GROUP * Q_BLOCK
    blk = lambda w, cb: pl.BlockSpec((1, Q_BLOCK, w), lambda bi, i: (bi, i, cb))
    allk = lambda cb: pl.BlockSpec((1, t, LANES), lambda bi, i: (bi, 0, cb))
    return pl.pallas_call(
        functools.partial(_dsa_prompt_kernel, topk=min(TOPK, t // 4)),
        grid=(b, nq),
        in_specs=[blk(512, ATT_Q0 // 512), blk(256, ATT_QI0 // 256), blk(LANES, ATT_KI0 // LANES),
                  allk(ATT_K0 // LANES), allk(ATT_V0 // LANES), allk(ATT_KI0 // LANES), _full(bias_t.shape)],
        out_specs=pl.BlockSpec((1, Q_BLOCK, ATT_HEADS * ATT_HEAD_DIM), lambda bi, i: (bi, i, 0)),
        out_shape=jax.ShapeDtypeStruct((b, t, ATT_HEADS * ATT_HEAD_DIM), F32),
        scratch_shapes=[pltpu.VMEM((t_keys, Q_BLOCK), I32), pltpu.VMEM((1, ATT_HEADS * Q_BLOCK), F32),
                        pltpu.VMEM((1, ATT_HEADS * Q_BLOCK), F32), pltpu.VMEM((LANES, ATT_HEADS * Q_BLOCK), F32)],
        compiler_params=_cparams(("parallel", "arbitrary")),
        name="dsa_prompt",
    )(c_att, c_att, c_att, c_att, c_att, c_att, bias_t)


def _dsa_s_score_kernel(pt_ref, l_ref, kidx_ref, c_ref, key_ref, thr_ref, need_ref, *, n_pages, tq, topk):
    p = pl.program_id(1)
    qi = c_ref[0, :, ATT_QI0:ATT_QI0 + IDX_HEADS * IDX_DIM]
    wi = _bf16_round(c_ref[0, :, ATT_WI0:ATT_WI0 + IDX_HEADS] * IDX_HEADS ** -0.5)

    def scores(keys):
        s = jnp.zeros((tq, keys.shape[0]), F32)
        for h in range(IDX_HEADS):
            d = _mm_nt(qi[:, h * IDX_DIM:(h + 1) * IDX_DIM], keys) * IDX_DIM ** -0.5
            s = s + _bf16_round(jnp.maximum(d, 0.0)) * wi[:, h:h + 1]
        return s

    @pl.when(p < n_pages)
    def _():
        key_ref[0, :, pl.ds(pl.multiple_of(p * PAGE, PAGE), PAGE)] = _ordered_key(scores(kidx_ref[0, 0]))

    @pl.when(p == n_pages)
    def _():
        s_new = scores(c_ref[0, :, ATT_KI0:ATT_KI0 + IDX_DIM])
        vis = lax.broadcasted_iota(I32, (tq, tq), 1) <= lax.broadcasted_iota(I32, (tq, tq), 0)
        s_new = jnp.where(vis, s_new, NEG_INF)
        tail = jnp.concatenate([s_new, jnp.full((tq, PAGE - tq), NEG_INF, F32)], axis=1)
        key_ref[0, :, n_pages * PAGE:(n_pages + 1) * PAGE] = _ordered_key(tail)
        keys = key_ref[0]
        count_ge = lambda cand: jnp.sum(jnp.where(keys >= cand, 1.0, 0.0), axis=1, keepdims=True)
        thr = _kth_largest(count_ge, (tq, 1), topk)
        thr_ref[0] = jnp.broadcast_to(thr, (tq, LANES))
        need_ref[0] = jnp.broadcast_to(topk - count_ge(thr + 1), (tq, LANES))


def _dsa_s_attn_kernel(pt_ref, l_ref, kp_ref, vp_ref, c_ref, key_ref, thr_ref, need_ref, bias_ref, o_ref,
                       m_ref, s_ref, acc_ref, taken_ref, *, n_pages, tq):
    p = pl.program_id(1)
    HD = ATT_HEAD_DIM
    rep = ATT_HEADS // ATT_KV_HEADS
    rows = ATT_HEADS * tq

    @pl.when(p == 0)
    def _():
        m_ref[...] = jnp.full(m_ref.shape, NEG_INF, F32)
        s_ref[...] = jnp.zeros(s_ref.shape, F32)
        acc_ref[...] = jnp.zeros(acc_ref.shape, F32)
        taken_ref[...] = jnp.zeros(taken_ref.shape, F32)

    q = c_ref[0, :, ATT_Q0:ATT_Q0 + ATT_HEADS * HD]
    z = jnp.zeros((tq, HD), F32)
    qs = jnp.concatenate([jnp.concatenate([q[:, h * HD:(h + 1) * HD], z] if h // rep == 0
                                          else [z, q[:, h * HD:(h + 1) * HD]], axis=1)
                          for h in range(ATT_HEADS)], axis=0)
    thr = thr_ref[0][:, 0:1]
    need = need_ref[0][:, 0:1]
    tri = jnp.where(lax.broadcasted_iota(I32, (PAGE, PAGE), 0) <= lax.broadcasted_iota(I32, (PAGE, PAGE), 1),
                    1.0, 0.0).astype(BF16)

    def step(kpage, vpage, bias, vis):
        key = key_ref[0]
        eq = key == thr
        prefix = jnp.dot(jnp.where(eq, 1.0, 0.0).astype(BF16), tri, preferred_element_type=F32)
        taken = taken_ref[...]
        sel = jnp.logical_or(key > thr, jnp.logical_and(eq, taken[:, 0:1] + prefix <= need))
        taken_ref[...] = taken + prefix[:, PAGE - 1:PAGE]
        madd = jnp.where(jnp.logical_and(sel, vis), 0.0, NEG_INF)
        lg = _mm_nt(qs, kpage) * HD ** -0.5 + bias + jnp.concatenate([madd] * ATT_HEADS, axis=0)
        m_old = m_ref[...]
        m_new = jnp.maximum(m_old, jnp.max(lg, axis=-1, keepdims=True))
        alpha = jnp.exp(m_old - m_new)
        pr = jnp.exp(lg - m_new)
        s_ref[...] = alpha * s_ref[...] + jnp.sum(pr, axis=-1, keepdims=True)
        acc_ref[...] = alpha * acc_ref[...] + _mm(pr, vpage)
        m_ref[...] = m_new

    all_vis = jnp.full((tq, PAGE), True)

    @pl.when(p < n_pages - 1)
    def _():
        step(kp_ref[0, 0], vp_ref[0, 0], bias_ref[2], all_vis)

    @pl.when(p == n_pages - 1)
    def _():
        step(kp_ref[0, 0], vp_ref[0, 0], bias_ref[1], all_vis)

    @pl.when(p == n_pages)
    def _():
        pad = jnp.zeros((PAGE - tq, LANES), F32)
        k_new = jnp.concatenate([c_ref[0, :, ATT_K0:ATT_K0 + LANES], pad], axis=0)
        v_new = jnp.concatenate([c_ref[0, :, ATT_V0:ATT_V0 + LANES], pad], axis=0)
        vis = lax.broadcasted_iota(I32, (tq, PAGE), 1) <= lax.broadcasted_iota(I32, (tq, PAGE), 0)
        step(k_new, v_new, bias_ref[0], vis)
        out = acc_ref[...] / s_ref[...]
        o_ref[0] = jnp.concatenate(
            [out[h * tq:(h + 1) * tq, (h // rep) * HD:(h // rep + 1) * HD] for h in range(ATT_HEADS)], axis=1)


def _dsa_sample(c_att, t_valid, layer, page_table, kidx_pool, k_pool, v_pool, bias_s):
    b, tq, _ = c_att.shape
    n_pages = page_table.shape[1]
    pt = page_table.reshape(-1)
    n_keys = (n_pages + 1) * PAGE
    page_of = lambda bi, p, pt_ref: pt_ref[bi * n_pages + jnp.minimum(p, n_pages - 1)]
    c_spec = pl.BlockSpec((1, tq, ATT_W), lambda bi, p, pt_ref, l_ref: (bi, 0, 0))
    vec_spec = pl.BlockSpec((1, tq, LANES), lambda bi, p, pt_ref, l_ref: (bi, 0, 0))
    keys, thr, need = pl.pallas_call(
        functools.partial(_dsa_s_score_kernel, n_pages=n_pages, tq=tq, topk=min(TOPK, (n_pages * PAGE + t_valid) // 4)),
        grid_spec=pltpu.PrefetchScalarGridSpec(
            num_scalar_prefetch=2,
            grid=(b, n_pages + 1),
            in_specs=[pl.BlockSpec((1, 1, PAGE, IDX_DIM),
                                   lambda bi, p, pt_ref, l_ref: (l_ref[0], page_of(bi, p, pt_ref), 0, 0)), c_spec],
            out_specs=[pl.BlockSpec((1, tq, n_keys), lambda bi, p, pt_ref, l_ref: (bi, 0, 0)), vec_spec, vec_spec],
        ),
        out_shape=[jax.ShapeDtypeStruct((b, tq, n_keys), I32), jax.ShapeDtypeStruct((b, tq, LANES), I32),
                   jax.ShapeDtypeStruct((b, tq, LANES), F32)],
        compiler_params=_cparams(("parallel", "arbitrary")),
        name="dsa_sample_select",
    )(pt, layer, kidx_pool, c_att)
    rows = ATT_HEADS * tq
    kv_spec = pl.BlockSpec((1, 1, PAGE, LANES), lambda bi, p, pt_ref, l_ref: (l_ref[0], page_of(bi, p, pt_ref), 0, 0))
    return pl.pallas_call(
        functools.partial(_dsa_s_attn_kernel, n_pages=n_pages, tq=tq),
        grid_spec=pltpu.PrefetchScalarGridSpec(
            num_scalar_prefetch=2,
            grid=(b, n_pages + 1),
            in_specs=[kv_spec, kv_spec, c_spec,
                      pl.BlockSpec((1, tq, PAGE), lambda bi, p, pt_ref, l_ref: (bi, 0, p)), vec_spec, vec_spec,
                      pl.BlockSpec(bias_s.shape, lambda bi, p, pt_ref, l_ref: (0, 0, 0))],
            out_specs=pl.BlockSpec((1, tq, ATT_HEADS * ATT_HEAD_DIM), lambda bi, p, pt_ref, l_ref: (bi, 0, 0)),
            scratch_shapes=[pltpu.VMEM((rows, 1), F32), pltpu.VMEM((rows, 1), F32), pltpu.VMEM((rows, LANES), F32),
                            pltpu.VMEM((tq, LANES), F32)],
        ),
        out_shape=jax.ShapeDtypeStruct((b, tq, ATT_HEADS * ATT_HEAD_DIM), F32),
        compiler_params=_cparams(("parallel", "arbitrary")),
        name="dsa_sample_attn",
    )(pt, layer, k_pool, v_pool, c_att, keys, thr, need, bias_s)


def _t5_bucket(dist):
    n = jnp.maximum(dist, 0)
    max_exact = NUM_BUCKETS // 2
    nf = jnp.maximum(n, 1).astype(F32)
    large = max_exact + (jnp.log(nf / max_exact) / math.log(MAX_DISTANCE / max_exact)
                         * (NUM_BUCKETS - max_exact)).astype(I32)
    large = jnp.minimum(large, NUM_BUCKETS - 1)
    return jnp.where(n < max_exact, n, large)


def _bias_tiles(rel_bias):
    r = jnp.arange(Q_BLOCK)
    d0 = r[:, None] - r[None, :]
    dist = jnp.stack([d0, d0 + Q_BLOCK, jnp.full_like(d0, 2 * Q_BLOCK)])
    return jnp.transpose(rel_bias[_t5_bucket(dist)].astype(F32), (0, 3, 1, 2))


def _group_layer(x, w, lb, mem, mem_blocks, shift, s_rwkv0, s_hgrn0, attend, cfg):
    b, t, d = x.shape
    n = b * t
    tm, t_valid = cfg["tm"], cfg["t_valid"]
    x2 = x.reshape(n, d)
    x2 = _ffn(x2, w["ffn1_norm_pre"], w["ffn1_w_up"], w["ffn1_w_down"], w["ffn1_norm_post"], cfg["tm_ffn"])
    c_rw, c_att, c_hg, gates = _proj(x2, w["mix_norm_pre"], [w["w_in_rw"], w["w_in_att"], w["w_in_hg"], w["w_in_gate"]],
                                     True, tm, "mix_in")
    c_rw = c_rw.reshape(b, t, -1)
    c_att = c_att.reshape(b, t, -1)
    streams = _rwkv_prep(c_rw, shift, w, t_valid, cfg["tm_prep"])
    o_a, s_rwkv = _rwkv(streams, s_rwkv0, w["rwkv_ln_w"], w["rwkv_ln_b"], w["rwkv_r_k"], cfg["rwkv_chunk"])
    o_b = attend(c_att)
    o_c, s_hgrn = _hgrn(c_hg.reshape(b, t, -1), lb, w["hgrn_norm_w"], s_hgrn0, cfg["hgrn_chunk"], t_valid, cfg["hgrn_tb"])
    x2 = _merge(x2, o_a.reshape(n, -1), o_b.reshape(n, -1), o_c.reshape(n, -1), gates,
                w["mix_w_proj_a"], w["mix_w_proj_b"], w["mix_w_proj_c"], w["mix_w_out"], w["mix_norm_post"], tm)
    (q,) = _proj(x2, w["cross_norm_pre"], [w["cross_wq"]], True, tm, "cross_q")
    o = _xattn(q.reshape(b, t, d), mem[0], mem[1], mem_blocks[0], mem_blocks[1], cfg["tq_x"])
    x2 = _out_proj(x2, o.reshape(n, d), w["cross_wo"], w["cross_norm_post"], tm)
    x2 = _ffn(x2, w["ffn2_norm_pre"], w["ffn2_w_up"], w["ffn2_w_down"], w["ffn2_norm_post"], cfg["tm_ffn"])
    nkv = ATT_KV_HEADS * ATT_HEAD_DIM
    state = (c_rw[:, t_valid - 1], s_rwkv, s_hgrn,
             c_att[:, :t_valid, ATT_K0:ATT_K0 + nkv].reshape(b, t_valid, ATT_KV_HEADS, ATT_HEAD_DIM),
             c_att[:, :t_valid, ATT_V0:ATT_V0 + nkv].reshape(b, t_valid, ATT_KV_HEADS, ATT_HEAD_DIM),
             c_att[:, :t_valid, ATT_KI0:ATT_KI0 + IDX_DIM])
    return x2.reshape(b, t, d), state


PROMPT_CFG = dict(tm=512, tm_ffn=512, tm_prep=256, rwkv_chunk=64, hgrn_chunk=16, hgrn_tb=256, tq_x=512)
SAMPLE_PAD = 8
SAMPLE_CFG = dict(tm=256, tm_ffn=256, tm_prep=SAMPLE_PAD, rwkv_chunk=SAMPLE_PAD, hgrn_chunk=SAMPLE_PAD,
                  hgrn_tb=SAMPLE_PAD, tq_x=SAMPLE_PAD)


def kernel(x_prompt, x_sample, cache_k, cache_v, cache_kidx, cache_mem_k, cache_mem_v, state_rwkv, state_rwkv_shift, state_hgrn, page_table, mem_prompt, rel_bias, hgrn_lb_logits, ffn1_norm_pre, ffn1_norm_post, ffn1_w_up, ffn1_w_down, mix_norm_pre, mix_norm_post, mix_w_in, rwkv_mu, rwkv_w0, rwkv_w_up, rwkv_a0, rwkv_a_up, rwkv_g_up, rwkv_k_k, rwkv_k_a, rwkv_r_k, rwkv_ln_w, rwkv_ln_b, hgrn_norm_w, mix_w_proj_a, mix_w_proj_b, mix_w_proj_c, mix_w_out, cross_norm_pre, cross_norm_post, cross_wq, cross_wk, cross_wv, cross_wo, ffn2_norm_pre, ffn2_norm_post, ffn2_w_up, ffn2_w_down):
    depth, d = ffn1_norm_pre.shape
    bp, tp, _ = x_prompt.shape
    bs, ts, _ = x_sample.shape
    d_r = rwkv_w0.shape[1]
    rw_cols = rwkv_mu.shape[1]
    d_h = hgrn_norm_w.shape[1]
    n_att = ATT_WI0 + IDX_HEADS - ATT_Q0
    bf = lambda a: a.astype(BF16)
    vec = lambda a: a[:, None, :]

    o1 = rw_cols + n_att
    w_att = jnp.pad(mix_w_in[:, :, rw_cols:o1], ((0, 0), (0, 0), (0, ATT_W - n_att)))
    head_sum = (np.arange(d_r)[:, None] // RWKV_HEAD == np.arange(d_r)[None, :] // RWKV_HEAD).astype(np.float32)
    per_head = lambda a: a.reshape(depth, RWKV_HEADS, 1, RWKV_HEAD)
    weights = dict(
        ffn1_norm_pre=vec(ffn1_norm_pre), ffn1_norm_post=vec(ffn1_norm_post), ffn1_w_up=bf(ffn1_w_up), ffn1_w_down=bf(ffn1_w_down),
        ffn2_norm_pre=vec(ffn2_norm_pre), ffn2_norm_post=vec(ffn2_norm_post), ffn2_w_up=bf(ffn2_w_up), ffn2_w_down=bf(ffn2_w_down),
        mix_norm_pre=vec(mix_norm_pre), mix_norm_post=vec(mix_norm_post),
        w_in_rw=bf(mix_w_in[:, :, :rw_cols]), w_in_att=bf(w_att),
        w_in_hg=bf(mix_w_in[:, :, o1:o1 + 4 * d_h]), w_in_gate=bf(mix_w_in[:, :, o1 + 4 * d_h:]),
        rwkv_mu=vec(rwkv_mu), rwkv_w0=vec(rwkv_w0), rwkv_w_up=bf(rwkv_w_up), rwkv_a0=vec(rwkv_a0), rwkv_a_up=bf(rwkv_a_up),
        rwkv_g_up=bf(rwkv_g_up), rwkv_k_k=vec(rwkv_k_k), rwkv_k_a=vec(rwkv_k_a),
        rwkv_r_k=rwkv_r_k[:, :, None, :], rwkv_ln_w=per_head(rwkv_ln_w), rwkv_ln_b=per_head(rwkv_ln_b),
        hgrn_norm_w=vec(hgrn_norm_w),
        mix_w_proj_a=bf(mix_w_proj_a), mix_w_proj_b=bf(mix_w_proj_b), mix_w_proj_c=bf(mix_w_proj_c), mix_w_out=bf(mix_w_out),
        cross_norm_pre=vec(cross_norm_pre), cross_norm_post=vec(cross_norm_post), cross_wq=bf(cross_wq), cross_wo=bf(cross_wo),
        cross_wkv=bf(jnp.concatenate([cross_wk, cross_wv], axis=-1)),
    )
    p_lb = jax.nn.softmax(hgrn_lb_logits.astype(F32), axis=0)
    lower_bounds = vec(jnp.cumsum(p_lb, axis=0) - p_lb[0:1])

    bias = _bias_tiles(rel_bias)
    bias_t = jnp.transpose(bias, (0, 3, 1, 2)).reshape(3, Q_BLOCK, ATT_HEADS * Q_BLOCK)
    bias_s = bias[:, :, :SAMPLE_PAD, :].reshape(3, ATT_HEADS * SAMPLE_PAD, Q_BLOCK)

    pad_t = SAMPLE_PAD - ts
    xs0 = jnp.pad(x_sample, ((0, 0), (0, pad_t), (0, 0)))
    n_pool = cache_k.shape[1]
    k_pool = cache_k.reshape(depth, n_pool, PAGE, ATT_KV_HEADS * ATT_HEAD_DIM)
    v_pool = cache_v.reshape(depth, n_pool, PAGE, ATT_KV_HEADS * ATT_HEAD_DIM)
    mem_tokens = mem_prompt.shape[1]
    mem2 = mem_prompt.reshape(bp * mem_tokens, d)
    ones_d = jnp.ones((1, d), F32)
    zero_shift = jnp.zeros((bp, 1, rw_cols), F32)
    zero_rwkv = jnp.zeros((bp, RWKV_HEADS, RWKV_HEAD, RWKV_HEAD), F32)
    zero_hgrn = jnp.zeros((bp, HGRN_HEADS, HGRN_EXPAND, HGRN_EXPAND), F32)
    prompt_cfg = dict(PROMPT_CFG, t_valid=tp)
    sample_cfg = dict(SAMPLE_CFG, t_valid=ts)

    def layer(carry, per_layer):
        xp, xs = carry
        w, lb, mem_ks, mem_vs, s_rw, s_sh, s_hg, li = per_layer
        w = dict(w, head_sum=jnp.asarray(head_sum, BF16))
        (mem_kv,) = _proj(mem2, ones_d, [w["cross_wkv"]], False, 256, "mem_kv")
        mem_kv = mem_kv.reshape(bp, mem_tokens, 2 * d)
        xp, st_p = _group_layer(xp, w, lb, (mem_kv, mem_kv), (0, 1), zero_shift, zero_rwkv, zero_hgrn,
                                lambda c: _dsa_prompt(c, bias_t), prompt_cfg)
        attend_s = lambda c: _dsa_sample(c, ts, li, page_table, cache_kidx, k_pool, v_pool, bias_s)
        xs, st_s = _group_layer(xs, w, lb, (mem_ks, mem_vs), (0, 0), s_sh, s_rw, s_hg, attend_s, sample_cfg)
        mem_k = mem_kv[:, :, :d].reshape(bp, mem_tokens, MEM_HEADS, d // MEM_HEADS)
        mem_v = mem_kv[:, :, d:].reshape(bp, mem_tokens, MEM_HEADS, d // MEM_HEADS)
        return (xp, xs), (st_p, (mem_k, mem_v), st_s)

    per_layer = (weights, lower_bounds,
                 cache_mem_k.reshape(depth, bs, mem_tokens, d), cache_mem_v.reshape(depth, bs, mem_tokens, d),
                 state_rwkv, state_rwkv_shift[:, :, None, :], state_hgrn,
                 jnp.arange(depth, dtype=I32)[:, None])
    (xp, xs), (st_p, (mem_k, mem_v), st_s) = lax.scan(layer, (x_prompt, xs0), per_layer)
    sh_p, rw_p, hg_p, k_p, v_p, ki_p = st_p
    sh_s, rw_s, hg_s, k_s, v_s, ki_s = st_s
    return (xp, xs[:, :ts], k_p, v_p, ki_p, mem_k, mem_v, rw_p, sh_p, hg_p, k_s, v_s, ki_s, rw_s, sh_s, hg_s)
```

```python
import functools
import math

import jax
import jax.numpy as jnp
import numpy as np
from jax import lax
from jax.experimental import pallas as pl
from jax.experimental.pallas import tpu as pltpu

F32 = jnp.float32
BF16 = jnp.bfloat16
I32 = jnp.int32

LANES = 128
SUBLANES = 8
VMEM_LIMIT_BYTES = 56 * 1024 * 1024

NORM_EPS = 1e-6
RWKV_LN_EPS = 64e-5
LB_FLOOR = 1e-30
NEG_INF = -1e30
INT_MIN = int(np.iinfo(np.int32).min)

RWKV_HEAD = 64
RWKV_HEADS = 8
RWKV_GROUP = 4
HGRN_HEADS = 4
HGRN_EXPAND = 128
ATT_HEADS = 8
ATT_KV_HEADS = 2
ATT_HEAD_DIM = 64
IDX_HEADS = 4
IDX_DIM = 64
TOPK = 256
Q_BLOCK = 128
MEM_HEADS = 4
NUM_BUCKETS = 32
MAX_DISTANCE = 128
PAGE = 128

ATT_Q0, ATT_K0, ATT_V0, ATT_QI0, ATT_KI0, ATT_WI0, ATT_W = 0, 512, 640, 768, 1024, 1088, 1152


def _cparams(sem):
    return pltpu.CompilerParams(dimension_semantics=sem, vmem_limit_bytes=VMEM_LIMIT_BYTES)


def _mm(a, b):
    return jnp.dot(a.astype(BF16), b.astype(BF16), preferred_element_type=F32)


def _mm_nt(a, b):
    return lax.dot_general(a.astype(BF16), b.astype(BF16), (((1,), (1,)), ((), ())),
                           preferred_element_type=F32)


def _mm_tn(a, b):
    return lax.dot_general(a.astype(BF16), b.astype(BF16), (((0,), (0,)), ((), ())),
                           preferred_element_type=F32)


def _bf16_round(x):
    return x.astype(BF16).astype(F32)


def _split3(x):
    hi = x.astype(BF16)
    r1 = x - hi.astype(F32)
    mid = r1.astype(BF16)
    lo = (r1 - mid.astype(F32)).astype(BF16)
    return hi, mid, lo


def _mm_exact_lhs(a01, x):
    a = a01.astype(BF16)
    hi, mid, lo = _split3(x)
    d = lambda p: jnp.dot(a, p, preferred_element_type=F32)
    return d(hi) + d(mid) + d(lo)


def _mm_exact_rhs(x, b01):
    b = b01.astype(BF16)
    hi, mid, lo = _split3(x)
    d = lambda p: jnp.dot(p, b, preferred_element_type=F32)
    return d(hi) + d(mid) + d(lo)


def _rms(x, g):
    return x * lax.rsqrt(jnp.mean(x * x, axis=-1, keepdims=True) + NORM_EPS) * g


def _sigmoid(x):
    return jax.nn.sigmoid(x)


def _softplus(x):
    return jnp.maximum(x, 0.0) + jnp.log1p(jnp.exp(-jnp.abs(x)))


def _full(shape):
    nd = len(shape)
    return pl.BlockSpec(shape, lambda *_: (0,) * nd)


def _row_tile(n, want):
    t = min(n, want)
    assert n % t == 0, (n, t)
    return t


def _ffn_kernel(x_ref, gpre_ref, wup_ref, wdn_ref, gpost_ref, o_ref, *, d_ff, tf):
    x = x_ref[...]
    h = _rms(x, gpre_ref[...]).astype(BF16)
    acc = jnp.zeros(x.shape, F32)
    for f0 in range(0, d_ff, tf):
        gate = jnp.dot(h, wup_ref[:, f0:f0 + tf], preferred_element_type=F32)
        up = jnp.dot(h, wup_ref[:, d_ff + f0:d_ff + f0 + tf], preferred_element_type=F32)
        act = (gate * _sigmoid(gate) * up).astype(BF16)
        acc = acc + jnp.dot(act, wdn_ref[f0:f0 + tf, :], preferred_element_type=F32)
    o_ref[...] = x + 0.5 * _rms(acc, gpost_ref[...])


def _ffn(x, gpre, wup, wdn, gpost, tm):
    n, d = x.shape
    d_ff = wdn.shape[0]
    tf = d_ff // 2 if (d_ff // 2) % LANES == 0 else d_ff
    tm = _row_tile(n, tm)
    return pl.pallas_call(
        functools.partial(_ffn_kernel, d_ff=d_ff, tf=tf),
        grid=(n // tm,),
        in_specs=[pl.BlockSpec((tm, d), lambda i: (i, 0)), _full((1, d)), _full(wup.shape), _full(wdn.shape),
                  _full((1, d))],
        out_specs=pl.BlockSpec((tm, d), lambda i: (i, 0)),
        out_shape=jax.ShapeDtypeStruct((n, d), F32),
        compiler_params=_cparams(("parallel",)),
        name="ffn",
    )(x, gpre, wup, wdn, gpost)


def _proj_kernel(*refs, n_out, norm, tn):
    x_ref, g_ref = refs[0], refs[1]
    w_refs = refs[2:2 + n_out]
    o_refs = refs[2 + n_out:]
    x = x_ref[...]
    h = (_rms(x, g_ref[...]) if norm else x).astype(BF16)
    for w_ref, o_ref in zip(w_refs, o_refs):
        n = w_ref.shape[1]
        step = tn if n % tn == 0 else n
        for n0 in range(0, n, step):
            o_ref[:, n0:n0 + step] = jnp.dot(h, w_ref[:, n0:n0 + step], preferred_element_type=F32)


def _proj(x, g, ws, norm, tm, name):
    n, d = x.shape
    tm = _row_tile(n, tm)
    return pl.pallas_call(
        functools.partial(_proj_kernel, n_out=len(ws), norm=norm, tn=512),
        grid=(n // tm,),
        in_specs=[pl.BlockSpec((tm, d), lambda i: (i, 0)), _full((1, d))] + [_full(w.shape) for w in ws],
        out_specs=[pl.BlockSpec((tm, w.shape[1]), lambda i: (i, 0)) for w in ws],
        out_shape=[jax.ShapeDtypeStruct((n, w.shape[1]), F32) for w in ws],
        compiler_params=_cparams(("parallel",)),
        name=name,
    )(x, g, *ws)


def _out_kernel(x_ref, a_ref, w_ref, g_ref, o_ref):
    y = jnp.dot(a_ref[...].astype(BF16), w_ref[...], preferred_element_type=F32)
    o_ref[...] = x_ref[...] + _rms(y, g_ref[...])


def _out_proj(x, a, w, g, tm):
    n, d = x.shape
    tm = _row_tile(n, tm)
    return pl.pallas_call(
        _out_kernel,
        grid=(n // tm,),
        in_specs=[pl.BlockSpec((tm, d), lambda i: (i, 0)), pl.BlockSpec((tm, a.shape[1]), lambda i: (i, 0)),
                  _full(w.shape), _full((1, d))],
        out_specs=pl.BlockSpec((tm, d), lambda i: (i, 0)),
        out_shape=jax.ShapeDtypeStruct((n, d), F32),
        compiler_params=_cparams(("parallel",)),
        name="out_proj",
    )(x, a, w, g)


def _merge_kernel(x_ref, oa_ref, ob_ref, oc_ref, gt_ref, wa_ref, wb_ref, wc_ref, wo_ref, g_ref, o_ref, *, d):
    m = jnp.zeros((x_ref.shape[0], d), F32)
    for j, (o_r, w_r) in enumerate(((oa_ref, wa_ref), (ob_ref, wb_ref), (oc_ref, wc_ref))):
        p = jnp.dot(o_r[...].astype(BF16), w_r[...], preferred_element_type=F32)
        m = m + _sigmoid(gt_ref[:, j * d:(j + 1) * d]) * p
    y = jnp.dot(m.astype(BF16), wo_ref[...], preferred_element_type=F32)
    o_ref[...] = x_ref[...] + _rms(y, g_ref[...])


def _merge(x, oa, ob, oc, gates, wa, wb, wc, wo, g, tm):
    n, d = x.shape
    tm = _row_tile(n, tm)
    row = lambda w: pl.BlockSpec((tm, w), lambda i: (i, 0))
    return pl.pallas_call(
        functools.partial(_merge_kernel, d=d),
        grid=(n // tm,),
        in_specs=[row(d), row(oa.shape[1]), row(ob.shape[1]), row(oc.shape[1]), row(gates.shape[1]),
                  _full(wa.shape), _full(wb.shape), _full(wc.shape), _full(wo.shape), _full((1, d))],
        out_specs=row(d),
        out_shape=jax.ShapeDtypeStruct((n, d), F32),
        compiler_params=_cparams(("parallel",)),
        name="merge",
    )(x, oa, ob, oc, gates, wa, wb, wc, wo, g)


def _xattn_kernel(q_ref, mk_ref, mv_ref, o_ref, *, heads):
    hd = q_ref.shape[2] // heads
    for h in range(heads):
        sl = slice(h * hd, (h + 1) * hd)
        lg = _mm_nt(q_ref[0, :, sl], mk_ref[0, :, sl]) * hd ** -0.5
        lg = lg - jnp.max(lg, axis=-1, keepdims=True)
        p = jnp.exp(lg)
        p = p / jnp.sum(p, axis=-1, keepdims=True)
        o_ref[0, :, sl] = _mm(p, mv_ref[0, :, sl])


def _xattn(q, mem_k, mem_v, k_blk, v_blk, tq):
    b, t, d = q.shape
    s = mem_k.shape[1]
    tq = _row_tile(t, tq)
    return pl.pallas_call(
        functools.partial(_xattn_kernel, heads=MEM_HEADS),
        grid=(b, t // tq),
        in_specs=[pl.BlockSpec((1, tq, d), lambda i, j: (i, j, 0)),
                  pl.BlockSpec((1, s, d), lambda i, j: (i, 0, k_blk)),
                  pl.BlockSpec((1, s, d), lambda i, j: (i, 0, v_blk))],
        out_specs=pl.BlockSpec((1, tq, d), lambda i, j: (i, j, 0)),
        out_shape=jax.ShapeDtypeStruct((b, t, d), F32),
        compiler_params=_cparams(("parallel", "parallel")),
        name="xattn",
    )(q, mem_k, mem_v)


def _rwkv_prep_kernel(c_ref, sh_ref, mu_ref, w0_ref, wup_ref, a0_ref, aup_ref, gup_ref, kk_ref, ka_ref, hs_ref,
                      r_o, lw_o, k_o, v_o, kk_o, b_o, g_o, carry_ref, *, t_valid, d_r):
    j = pl.program_id(1)
    tm = c_ref.shape[1]

    @pl.when(j == 0)
    def _():
        carry_ref[...] = sh_ref[0]

    c = c_ref[0]
    row = lax.broadcasted_iota(I32, c.shape, 0)
    prev = jnp.where(row == 0, carry_ref[...], pltpu.roll(c, 1, axis=0))
    carry_ref[...] = c[tm - 1:tm, :]
    csh = c + (prev - c) * mu_ref[...]
    r, k, v = csh[:, 0:d_r], csh[:, d_r:2 * d_r], csh[:, 2 * d_r:3 * d_r]
    o = 3 * d_r
    n_w, n_a, n_g = wup_ref.shape[0], aup_ref.shape[0], gup_ref.shape[0]
    xw, xa, xg = csh[:, o:o + n_w], csh[:, o + n_w:o + n_w + n_a], csh[:, o + n_w + n_a:o + n_w + n_a + n_g]
    w_log = -_softplus(-(w0_ref[...] + _mm(jnp.tanh(xw), wup_ref[...]))) - 0.5
    lw = -jnp.exp(w_log)
    a_lr = _sigmoid(a0_ref[...] + _mm(xa, aup_ref[...]))
    g = _mm(_sigmoid(xg), gup_ref[...])
    kk = k * kk_ref[...]
    ss = _mm_exact_rhs(kk * kk, hs_ref[...])
    kk = kk / jnp.maximum(jnp.sqrt(ss), 1e-12)
    k_mod = k * (1.0 + (a_lr - 1.0) * ka_ref[...])
    b = kk * a_lr
    if t_valid < tm:
        ok = (lax.broadcasted_iota(I32, (tm, d_r), 0) + j * tm) < t_valid
        z = lambda t: jnp.where(ok, t, 0.0)
        lw, k_mod, v, kk, b = z(lw), z(k_mod), z(v), z(kk), z(b)
    for h in range(RWKV_HEADS):
        sl = slice(h * RWKV_HEAD, (h + 1) * RWKV_HEAD)
        for o_ref, val in ((r_o, r), (lw_o, lw), (k_o, k_mod), (v_o, v), (kk_o, kk), (b_o, b), (g_o, g)):
            o_ref[0, h] = val[:, sl]


def _rwkv_prep(c_rw, shift, p, t_valid, tm):
    b, t, cols = c_rw.shape
    d_r = p["rwkv_w0"].shape[1]
    tm = _row_tile(t, tm)
    hm = jax.ShapeDtypeStruct((b, RWKV_HEADS, t, RWKV_HEAD), F32)
    hm_spec = pl.BlockSpec((1, RWKV_HEADS, tm, RWKV_HEAD), lambda i, j: (i, 0, j, 0))
    params = [p["rwkv_mu"], p["rwkv_w0"], p["rwkv_w_up"], p["rwkv_a0"], p["rwkv_a_up"], p["rwkv_g_up"],
              p["rwkv_k_k"], p["rwkv_k_a"], p["head_sum"]]
    return pl.pallas_call(
        functools.partial(_rwkv_prep_kernel, t_valid=t_valid, d_r=d_r),
        grid=(b, t // tm),
        in_specs=[pl.BlockSpec((1, tm, cols), lambda i, j: (i, j, 0)),
                  pl.BlockSpec((1, 1, cols), lambda i, j: (i, 0, 0))] + [_full(a.shape) for a in params],
        out_specs=[hm_spec] * 7,
        out_shape=[hm] * 7,
        scratch_shapes=[pltpu.VMEM((1, cols), F32)],
        compiler_params=_cparams(("parallel", "arbitrary")),
        name="rwkv_prep",
    )(c_rw, shift, *params)


def _rwkv_kernel(r_ref, lw_ref, k_ref, v_ref, kk_ref, b_ref, g_ref, s0_ref, lnw_ref, lnb_ref, rk_ref,
                 o_ref, so_ref, s_ref, *, chunk):
    t_idx = pl.program_id(2)
    nh, C, N = RWKV_GROUP, chunk, RWKV_HEAD
    R = nh * C

    @pl.when(t_idx == 0)
    def _():
        s_ref[...] = s0_ref[0]

    ld = lambda ref: ref[0].reshape(R, N)
    r, lw, k, v, kk, b, g = (ld(x) for x in (r_ref, lw_ref, k_ref, v_ref, kk_ref, b_ref, g_ref))
    row = lax.broadcasted_iota(I32, (R, R), 0)
    col = lax.broadcasted_iota(I32, (R, R), 1)
    same = (row // C) == (col // C)
    incl = jnp.logical_and(same, col <= row)
    strict = jnp.logical_and(same, col < row)
    cum = _mm_exact_lhs(jnp.where(incl, 1.0, 0.0), lw)
    e_neg = jnp.exp(-cum)
    at = -kk * jnp.exp(cum - lw)
    rt = r * jnp.exp(cum)
    bt = b * e_neg
    kt = k * e_neg
    gram = _mm_nt(jnp.concatenate([at, rt], axis=0), jnp.concatenate([bt, kt], axis=0))
    a_ab = jnp.where(strict, gram[:R, :R], 0.0)
    a_ak = jnp.where(strict, gram[:R, R:], 0.0)
    a_rb = jnp.where(incl, gram[R:, :R], 0.0)
    a_rk = jnp.where(incl, gram[R:, R:], 0.0)
    x = jnp.concatenate([at, _mm(a_ak, v)], axis=1)
    d_inv = jnp.where(row == col, 1.0, 0.0)
    m_blk = 1
    while m_blk < C:
        lower_left = jnp.logical_and(row // (2 * m_blk) == col // (2 * m_blk),
                                     jnp.logical_and((row // m_blk) % 2 == 1, (col // m_blk) % 2 == 0))
        a_off = jnp.where(lower_left, a_ab, 0.0)
        d_inv = d_inv + (a_off if m_blk == 1 else _mm(d_inv, _mm(a_off, d_inv)))
        m_blk *= 2
    x = _mm(d_inv, x)
    ax = _mm(a_rb, x)
    qt = rt + ax[:, :N]
    y0 = ax[:, N:] + _mm(a_rk, v)
    eye = jnp.where(lax.broadcasted_iota(I32, (N, N), 0) == lax.broadcasted_iota(I32, (N, N), 1), 1.0, 0.0)
    outs = []
    for h in range(nh):
        sl = slice(h * C, (h + 1) * C)
        p_c = jnp.exp(cum[h * C + C - 1:h * C + C, :])
        m = (eye + _mm_tn(x[sl, :N], bt[sl])) * p_c
        s_loc = _mm_tn(jnp.concatenate([x[sl, N:], v[sl]], axis=0),
                       jnp.concatenate([bt[sl], kt[sl]], axis=0)) * p_c
        s = s_ref[h]
        y = _mm_nt(qt[sl], s) + y0[sl]
        s_ref[h] = _mm(s, m) + s_loc
        mu = jnp.mean(y, axis=-1, keepdims=True)
        yc = y - mu
        var = jnp.mean(yc * yc, axis=-1, keepdims=True)
        yn = yc * lax.rsqrt(var + RWKV_LN_EPS) * lnw_ref[h] + lnb_ref[h]
        yn = yn + jnp.sum(r[sl] * k[sl] * rk_ref[h], axis=-1, keepdims=True) * v[sl]
        outs.append(yn * g[sl])
    o_ref[0] = jnp.concatenate(outs, axis=1)

    @pl.when(t_idx == pl.num_programs(2) - 1)
    def _():
        so_ref[0] = s_ref[...]


def _rwkv(streams, s0, lnw, lnb, rk, chunk):
    b, nh, t, n = streams[0].shape
    ng = nh // RWKV_GROUP
    st_spec = pl.BlockSpec((1, RWKV_GROUP, chunk, n), lambda i, gq, j: (i, gq, j, 0))
    s_spec = pl.BlockSpec((1, RWKV_GROUP, n, n), lambda i, gq, j: (i, gq, 0, 0))
    p_spec = pl.BlockSpec((RWKV_GROUP, 1, n), lambda i, gq, j: (gq, 0, 0))
    return pl.pallas_call(
        functools.partial(_rwkv_kernel, chunk=chunk),
        grid=(b, ng, t // chunk),
        in_specs=[st_spec] * 7 + [s_spec, p_spec, p_spec, p_spec],
        out_specs=[pl.BlockSpec((1, chunk, RWKV_GROUP * n), lambda i, gq, j: (i, j, gq)), s_spec],
        out_shape=[jax.ShapeDtypeStruct((b, t, nh * n), F32), jax.ShapeDtypeStruct((b, nh, n, n), F32)],
        scratch_shapes=[pltpu.VMEM((RWKV_GROUP, n, n), F32)],
        compiler_params=_cparams(("parallel", "parallel", "arbitrary")),
        name="rwkv",
    )(*streams, s0, lnw, lnb, rk)


def _hgrn_kernel(c_ref, lb_ref, nw_ref, s0_ref, o_ref, so_ref, st_ref, cum_ref, kh_ref, *, chunk, t_valid):
    j = pl.program_id(1)
    tb = c_ref.shape[1]
    d = lb_ref.shape[1]
    K = HGRN_EXPAND
    c = chunk

    @pl.when(j == 0)
    def _():
        for h in range(HGRN_HEADS):
            st_ref[h] = s0_ref[0, h].T

    z = c_ref[0, :, d:2 * d]
    lb = lb_ref[...]
    ls = -_softplus(-z)
    x1 = jnp.log(jnp.maximum(lb, LB_FLOOR))
    x2 = jnp.log1p(-lb) + ls
    logf = jnp.maximum(x1, x2) + jnp.log1p(jnp.exp(-jnp.abs(x1 - x2)))
    kh = (1.0 - lb) * _sigmoid(-z)
    if t_valid < tb:
        ok = (lax.broadcasted_iota(I32, (tb, d), 0) + j * tb) < t_valid
        logf = jnp.where(ok, logf, 0.0)
        kh = jnp.where(ok, kh, 0.0)
    row = lax.broadcasted_iota(I32, (tb, tb), 0)
    col = lax.broadcasted_iota(I32, (tb, tb), 1)
    tri = jnp.where(jnp.logical_and(row // c == col // c, col <= row), 1.0, 0.0)
    cum_ref[...] = _mm_exact_lhs(tri, logf)
    kh_ref[...] = kh

    ones = jnp.ones((K, K), BF16)
    rr = lax.broadcasted_iota(I32, (c * c, K), 0)
    causal = (rr % c) <= (rr // c)
    sel = jnp.where(lax.broadcasted_iota(I32, (c, c * c), 1) // c == lax.broadcasted_iota(I32, (c, c * c), 0),
                    1.0, 0.0).astype(BF16)

    def body(ci, carry):
        r0 = pl.multiple_of(ci * c, c)
        for h in range(HGRN_HEADS):
            hs = slice(h * K, (h + 1) * K)
            hq = c_ref[0, pl.ds(r0, c), h * K:(h + 1) * K]
            q = hq * _sigmoid(hq)
            v = c_ref[0, pl.ds(r0, c), 2 * d + h * K:2 * d + (h + 1) * K]
            hg = c_ref[0, pl.ds(r0, c), 3 * d + h * K:3 * d + (h + 1) * K]
            bq = cum_ref[pl.ds(r0, c), hs]
            kq = kh_ref[pl.ds(r0, c), hs]
            e = jnp.concatenate([q[t:t + 1] * kq * jnp.exp(jnp.minimum(bq[t:t + 1] - bq, 0.0)) for t in range(c)],
                                axis=0)
            att = jnp.dot(e.astype(BF16), ones, preferred_element_type=F32)
            w = jnp.where(causal, att, 0.0) * jnp.concatenate([v] * c, axis=0)
            st = st_ref[h]
            o = jnp.dot(sel, w.astype(BF16), preferred_element_type=F32) + _mm_nt(q * jnp.exp(bq), st)
            bl = bq[c - 1:c]
            st_ref[h] = st * jnp.exp(bl) + _mm_tn(v, kq * jnp.exp(bl - bq))
            o = o * lax.rsqrt(jnp.mean(o * o, axis=-1, keepdims=True) + NORM_EPS)
            o_ref[0, pl.ds(r0, c), hs] = o * nw_ref[:, hs] * (hg * _sigmoid(hg))
        return carry

    lax.fori_loop(0, tb // c, body, 0)

    @pl.when(j == pl.num_programs(1) - 1)
    def _():
        for h in range(HGRN_HEADS):
            so_ref[0, h] = st_ref[h].T


def _hgrn(c_hg, lb, nw, s0, chunk, t_valid, tb):
    b, t, cols = c_hg.shape
    d = cols // 4
    tb = _row_tile(t, tb)
    s_spec = pl.BlockSpec((1, HGRN_HEADS, HGRN_EXPAND, HGRN_EXPAND), lambda i, j: (i, 0, 0, 0))
    return pl.pallas_call(
        functools.partial(_hgrn_kernel, chunk=chunk, t_valid=t_valid),
        grid=(b, t // tb),
        in_specs=[pl.BlockSpec((1, tb, cols), lambda i, j: (i, j, 0)), _full((1, d)), _full((1, d)), s_spec],
        out_specs=[pl.BlockSpec((1, tb, d), lambda i, j: (i, j, 0)), s_spec],
        out_shape=[jax.ShapeDtypeStruct((b, t, d), F32), jax.ShapeDtypeStruct(s0.shape, F32)],
        scratch_shapes=[pltpu.VMEM((HGRN_HEADS, HGRN_EXPAND, HGRN_EXPAND), F32), pltpu.VMEM((tb, d), F32),
                        pltpu.VMEM((tb, d), F32)],
        compiler_params=_cparams(("parallel", "arbitrary")),
        name="hgrn",
    )(c_hg, lb, nw, s0)


def _ordered_key(s):
    bits = pltpu.bitcast(s + 0.0, I32)
    return bits ^ ((bits >> 31) & 0x7FFFFFFF)


def _kth_largest(count_ge, shape, k):
    c0 = count_ge(jnp.zeros(shape, I32))
    thr = jnp.where(c0 >= k, 0, INT_MIN).astype(I32)

    def body(i, thr):
        cand = thr | jnp.left_shift(jnp.int32(1), 30 - i)
        return jnp.where(count_ge(cand) >= k, cand, thr)

    return lax.fori_loop(0, 31, body, thr)


DSA_GROUP = 4


def _dsa_prompt_kernel(q_ref, qi_ref, kw_ref, k_ref, v_ref, ki_ref, bias_ref, o_ref,
                       key_ref, m_ref, l_ref, acc_ref, *, topk):
    i = pl.program_id(1)
    QB, HD = Q_BLOCK, ATT_HEAD_DIM
    zeros64 = jnp.zeros((HD, QB), F32)

    q_t = q_ref[0].T
    rep = ATT_HEADS // ATT_KV_HEADS
    tiles = []
    for h in range(ATT_HEADS):
        qh = q_t[h * HD:(h + 1) * HD]
        tiles.append(jnp.concatenate([qh, zeros64] if h // rep == 0 else [zeros64, qh], axis=0))
    qs_t = (jnp.concatenate(tiles, axis=1) * HD ** -0.5).astype(BF16)
    qi_t = qi_ref[0].T * IDX_DIM ** -0.5
    qi_pad = jnp.concatenate([jnp.concatenate([qi_t[h * IDX_DIM:(h + 1) * IDX_DIM], zeros64], axis=0)
                              for h in range(IDX_HEADS)], axis=1).astype(BF16)
    w_t = kw_ref[0].T
    w_rows = [_bf16_round(w_t[IDX_DIM + h:IDX_DIM + h + 1] * IDX_HEADS ** -0.5) for h in range(IDX_HEADS)]

    row = lax.broadcasted_iota(I32, (QB, QB), 0)
    col = lax.broadcasted_iota(I32, (QB, QB), 1)
    n_grp = (i + DSA_GROUP) // DSA_GROUP

    def score_body(j, carry):
        r0 = pl.multiple_of(j * QB, QB)
        dots = jnp.dot(ki_ref[0, pl.ds(r0, QB), :].astype(BF16), qi_pad, preferred_element_type=F32)
        s = jnp.zeros((QB, QB), F32)
        for h in range(IDX_HEADS):
            s = s + _bf16_round(jnp.maximum(dots[:, h * QB:(h + 1) * QB], 0.0)) * w_rows[h]
        vis = (j * QB + row) <= (i * QB + col)
        key = _ordered_key(jnp.where(vis, s, NEG_INF))
        key_ref[pl.ds(r0, QB), :] = jnp.where(j <= i, key, INT_MIN)
        return carry

    lax.fori_loop(0, n_grp * DSA_GROUP, score_body, 0)

    def count_ge(cand):
        rows = DSA_GROUP * QB

        def grp(gi, acc):
            r0 = pl.multiple_of(gi * rows, rows)
            hit = jnp.where(key_ref[pl.ds(r0, rows), :] >= cand, 1, 0)
            return acc + jnp.sum(hit.reshape(rows // SUBLANES, SUBLANES, QB), axis=0)
        acc = lax.fori_loop(0, n_grp, grp, jnp.zeros((SUBLANES, QB), I32))
        return jnp.sum(acc, axis=0, keepdims=True)

    thr = _kth_largest(count_ge, (1, QB), topk)
    need = (topk - count_ge(thr + 1)).astype(F32)

    m_ref[...] = jnp.full(m_ref.shape, NEG_INF, F32)
    l_ref[...] = jnp.zeros(l_ref.shape, F32)
    acc_ref[...] = jnp.zeros(acc_ref.shape, F32)
    tri = jnp.where(col <= row, 1.0, 0.0).astype(BF16)

    def attn_body(j, taken):
        r0 = pl.multiple_of(j * QB, QB)
        key = key_ref[pl.ds(r0, QB), :]
        eq = key == thr
        prefix = jnp.dot(tri, jnp.where(eq, 1.0, 0.0).astype(BF16), preferred_element_type=F32)
        sel = jnp.logical_or(key > thr, jnp.logical_and(eq, taken + prefix <= need))
        vis = (j * QB + row) <= (i * QB + col)
        madd = jnp.where(jnp.logical_and(sel, vis), 0.0, NEG_INF)
        lg = jnp.dot(k_ref[0, pl.ds(r0, QB), :].astype(BF16), qs_t, preferred_element_type=F32)
        lg = lg + bias_ref[jnp.minimum(i - j, 2)] + jnp.concatenate([madd] * ATT_HEADS, axis=1)
        m_old = m_ref[...]
        m_new = jnp.maximum(m_old, jnp.max(lg, axis=0, keepdims=True))
        alpha = jnp.exp(m_old - m_new)
        p = jnp.exp(lg - m_new)
        l_ref[...] = alpha * l_ref[...] + jnp.sum(p, axis=0, keepdims=True)
        acc_ref[...] = alpha * acc_ref[...] + _mm(v_ref[0, pl.ds(r0, QB), :].T, p)
        m_ref[...] = m_new
        return taken + prefix[QB - 1:QB, :]

    lax.fori_loop(0, i + 1, attn_body, jnp.zeros((1, QB), F32))

    out_t = (acc_ref[...] / l_ref[...]).T
    o_ref[0] = jnp.concatenate(
        [out_t[h * QB:(h + 1) * QB, (h // rep) * HD:(h // rep + 1) * HD] for h in range(ATT_HEADS)], axis=1)


def _dsa_prompt(c_att, bias_t):
    b, t, _ = c_att.shape
    nq = t // Q_BLOCK
    t_keys = ((nq + DSA_GROUP - 1) // DSA_GROUP) * DSA_GROUP * Q_BLOCK
    blk = lambda w, cb: pl.BlockSpec((1, Q_BLOCK, w), lambda bi, i: (bi, i, cb))
    allk = lambda cb: pl.BlockSpec((1, t, LANES), lambda bi, i: (bi, 0, cb))
    return pl.pallas_call(
        functools.partial(_dsa_prompt_kernel, topk=min(TOPK, t // 4)),
        grid=(b, nq),
        in_specs=[blk(512, ATT_Q0 // 512), blk(256, ATT_QI0 // 256), blk(LANES, ATT_KI0 // LANES),
                  allk(ATT_K0 // LANES), allk(ATT_V0 // LANES), allk(ATT_KI0 // LANES), _full(bias_t.shape)],
        out_specs=pl.BlockSpec((1, Q_BLOCK, ATT_HEADS * ATT_HEAD_DIM), lambda bi, i: (bi, i, 0)),
        out_shape=jax.ShapeDtypeStruct((b, t, ATT_HEADS * ATT_HEAD_DIM), F32),
        scratch_shapes=[pltpu.VMEM((t_keys, Q_BLOCK), I32), pltpu.VMEM((1, ATT_HEADS * Q_BLOCK), F32),
                        pltpu.VMEM((1, ATT_HEADS * Q_BLOCK), F32), pltpu.VMEM((LANES, ATT_HEADS * Q_BLOCK), F32)],
        compiler_params=_cparams(("parallel", "arbitrary")),
        name="dsa_prompt",
    )(c_att, c_att, c_att, c_att, c_att, c_att, bias_t)


def _dsa_sample_kernel(pt_ref, l_ref, kidx_hbm, k_hbm, v_hbm, c_ref, bias_ref, o_ref,
                       ki_buf, k_buf, v_buf, sems, key_ref, madd_ref, *, n_pages, tq, topk):
    b = pl.program_id(0)
    slot = b % 2
    layer = l_ref[0]
    HD = ATT_HEAD_DIM
    rep = ATT_HEADS // ATT_KV_HEADS
    n_past = n_pages * PAGE

    def page_copies(page, p, sl):
        rows = pl.ds(pl.multiple_of(p * PAGE, PAGE), PAGE)
        return (pltpu.make_async_copy(kidx_hbm.at[layer, page], ki_buf.at[sl, rows], sems.at[0, sl]),
                pltpu.make_async_copy(k_hbm.at[layer, page], k_buf.at[sl, rows], sems.at[1, sl]),
                pltpu.make_async_copy(v_hbm.at[layer, page], v_buf.at[sl, rows], sems.at[2, sl]))

    def start_gather(bi, sl):
        def body(p, carry):
            for cp in page_copies(pt_ref[bi * n_pages + p], p, sl):
                cp.start()
            return carry
        lax.fori_loop(0, n_pages, body, 0)

    def wait_gather(sl):
        def body(p, carry):
            for cp in page_copies(0, p, sl):
                cp.wait()
            return carry
        lax.fori_loop(0, n_pages, body, 0)

    @pl.when(b == 0)
    def _():
        start_gather(0, 0)

    @pl.when(b + 1 < pl.num_programs(0))
    def _():
        start_gather(b + 1, 1 - slot)

    wait_gather(slot)

    qi = c_ref[0, :, ATT_QI0:ATT_QI0 + IDX_HEADS * IDX_DIM] * IDX_DIM ** -0.5
    wi = _bf16_round(c_ref[0, :, ATT_WI0:ATT_WI0 + IDX_HEADS] * IDX_HEADS ** -0.5)
    qi_rows = jnp.concatenate([qi[:, h * IDX_DIM:(h + 1) * IDX_DIM] for h in range(IDX_HEADS)], axis=0)

    def scores(keys):
        d = _bf16_round(jnp.maximum(_mm_nt(qi_rows, keys), 0.0))
        s = jnp.zeros((tq, keys.shape[0]), F32)
        for h in range(IDX_HEADS):
            s = s + d[h * tq:(h + 1) * tq] * wi[:, h:h + 1]
        return s

    causal_new = lax.broadcasted_iota(I32, (tq, PAGE), 1) <= lax.broadcasted_iota(I32, (tq, PAGE), 0)
    key_ref[:, 0:n_past] = _ordered_key(scores(ki_buf[slot]))
    ki_new = jnp.concatenate([c_ref[0, :, ATT_KI0:ATT_KI0 + IDX_DIM], jnp.zeros((PAGE - tq, IDX_DIM), F32)], axis=0)
    key_ref[:, n_past:n_past + PAGE] = _ordered_key(jnp.where(causal_new, scores(ki_new), NEG_INF))
    keys = key_ref[...]
    count_ge = lambda cand: jnp.sum(jnp.where(keys >= cand, 1, 0), axis=1, keepdims=True)
    thr = _kth_largest(count_ge, (tq, 1), topk)
    excess = jnp.max(count_ge(thr) - topk)

    @pl.when(excess == 0)
    def _():
        madd_ref[...] = jnp.where(keys >= thr, 0.0, NEG_INF)

    @pl.when(excess > 0)
    def _():
        need = (topk - count_ge(thr + 1)).astype(F32)
        tri = jnp.where(lax.broadcasted_iota(I32, (PAGE, PAGE), 0) <= lax.broadcasted_iota(I32, (PAGE, PAGE), 1),
                        1.0, 0.0).astype(BF16)

        def blk(jb, taken):
            cols = pl.ds(pl.multiple_of(jb * PAGE, PAGE), PAGE)
            key = key_ref[:, cols]
            eq = key == thr
            prefix = jnp.dot(jnp.where(eq, 1.0, 0.0).astype(BF16), tri, preferred_element_type=F32)
            sel = jnp.logical_or(key > thr, jnp.logical_and(eq, taken + prefix <= need))
            madd_ref[:, cols] = jnp.where(sel, 0.0, NEG_INF)
            return taken + prefix[:, PAGE - 1:PAGE]

        lax.fori_loop(0, n_pages + 1, blk, jnp.zeros((tq, 1), F32))

    q = c_ref[0, :, ATT_Q0:ATT_Q0 + ATT_HEADS * HD] * HD ** -0.5
    z = jnp.zeros((tq, HD), F32)
    qs = jnp.concatenate([jnp.concatenate([q[:, h * HD:(h + 1) * HD], z] if h // rep == 0
                                          else [z, q[:, h * HD:(h + 1) * HD]], axis=1)
                          for h in range(ATT_HEADS)], axis=0)
    pad = jnp.zeros((PAGE - tq, LANES), F32)
    k_new = jnp.concatenate([c_ref[0, :, ATT_K0:ATT_K0 + LANES], pad], axis=0)
    v_new = jnp.concatenate([c_ref[0, :, ATT_V0:ATT_V0 + LANES], pad], axis=0)
    lg_past = _mm_nt(qs, k_buf[slot])
    far = bias_ref[2][:, 0:1]
    madd_ref[:, n_past:n_past + PAGE] = jnp.where(causal_new, madd_ref[:, n_past:n_past + PAGE], NEG_INF)
    madd = madd_ref[...]
    lg = jnp.concatenate([lg_past[:, :n_past - PAGE] + far, lg_past[:, n_past - PAGE:] + bias_ref[1],
                          _mm_nt(qs, k_new) + bias_ref[0]], axis=1) + jnp.concatenate([madd] * ATT_HEADS, axis=0)
    pr = jnp.exp(lg - jnp.max(lg, axis=-1, keepdims=True))
    out = (_mm(pr[:, :n_past], v_buf[slot]) + _mm(pr[:, n_past:], v_new)) / jnp.sum(pr, axis=-1, keepdims=True)
    o_ref[0] = jnp.concatenate(
        [out[h * tq:(h + 1) * tq, (h // rep) * HD:(h // rep + 1) * HD] for h in range(ATT_HEADS)], axis=1)


def _dsa_sample(c_att, t_valid, layer, page_table, kidx_pool, k_pool, v_pool, bias_s):
    b, tq, _ = c_att.shape
    n_pages = page_table.shape[1]
    pt = page_table.reshape(-1)
    n_past = n_pages * PAGE
    n_keys = n_past + PAGE
    hbm = pl.BlockSpec(memory_space=pl.ANY)
    return pl.pallas_call(
        functools.partial(_dsa_sample_kernel, n_pages=n_pages, tq=tq, topk=min(TOPK, (n_past + t_valid) // 4)),
        grid_spec=pltpu.PrefetchScalarGridSpec(
            num_scalar_prefetch=2,
            grid=(b,),
            in_specs=[hbm, hbm, hbm, pl.BlockSpec((1, tq, ATT_W), lambda bi, pt_ref, l_ref: (bi, 0, 0)),
                      pl.BlockSpec(bias_s.shape, lambda bi, pt_ref, l_ref: (0, 0, 0))],
            out_specs=pl.BlockSpec((1, tq, ATT_HEADS * ATT_HEAD_DIM), lambda bi, pt_ref, l_ref: (bi, 0, 0)),
            scratch_shapes=[pltpu.VMEM((2, n_past, IDX_DIM), F32), pltpu.VMEM((2, n_past, LANES), F32),
                            pltpu.VMEM((2, n_past, LANES), F32), pltpu.SemaphoreType.DMA((3, 2)),
                            pltpu.VMEM((tq, n_keys), I32), pltpu.VMEM((tq, n_keys), F32)],
        ),
        out_shape=jax.ShapeDtypeStruct((b, tq, ATT_HEADS * ATT_HEAD_DIM), F32),
        compiler_params=_cparams(("arbitrary",)),
        name="dsa_sample",
    )(pt, layer, kidx_pool, k_pool, v_pool, c_att, bias_s)


def _t5_bucket(dist):
    n = jnp.maximum(dist, 0)
    max_exact = NUM_BUCKETS // 2
    nf = jnp.maximum(n, 1).astype(F32)
    large = max_exact + (jnp.log(nf / max_exact) / math.log(MAX_DISTANCE / max_exact)
                         * (NUM_BUCKETS - max_exact)).astype(I32)
    large = jnp.minimum(large, NUM_BUCKETS - 1)
    return jnp.where(n < max_exact, n, large)


def _bias_tiles(rel_bias):
    r = jnp.arange(Q_BLOCK)
    d0 = r[:, None] - r[None, :]
    dist = jnp.stack([d0, d0 + Q_BLOCK, jnp.full_like(d0, 2 * Q_BLOCK)])
    return jnp.transpose(rel_bias[_t5_bucket(dist)].astype(F32), (0, 3, 1, 2))


def _group_layer(x, w, lb, mem, mem_blocks, shift, s_rwkv0, s_hgrn0, attend, cfg):
    b, t, d = x.shape
    n = b * t
    tm, t_valid = cfg["tm"], cfg["t_valid"]
    x2 = x.reshape(n, d)
    x2 = _ffn(x2, w["ffn1_norm_pre"], w["ffn1_w_up"], w["ffn1_w_down"], w["ffn1_norm_post"], cfg["tm_ffn"])
    c_rw, c_att, c_hg, gates = _proj(x2, w["mix_norm_pre"], [w["w_in_rw"], w["w_in_att"], w["w_in_hg"], w["w_in_gate"]],
                                     True, tm, "mix_in")
    c_rw = c_rw.reshape(b, t, -1)
    c_att = c_att.reshape(b, t, -1)
    streams = _rwkv_prep(c_rw, shift, w, t_valid, cfg["tm_prep"])
    o_a, s_rwkv = _rwkv(streams, s_rwkv0, w["rwkv_ln_w"], w["rwkv_ln_b"], w["rwkv_r_k"], cfg["rwkv_chunk"])
    o_b = attend(c_att)
    o_c, s_hgrn = _hgrn(c_hg.reshape(b, t, -1), lb, w["hgrn_norm_w"], s_hgrn0, cfg["hgrn_chunk"], t_valid, cfg["hgrn_tb"])
    x2 = _merge(x2, o_a.reshape(n, -1), o_b.reshape(n, -1), o_c.reshape(n, -1), gates,
                w["mix_w_proj_a"], w["mix_w_proj_b"], w["mix_w_proj_c"], w["mix_w_out"], w["mix_norm_post"], tm)
    (q,) = _proj(x2, w["cross_norm_pre"], [w["cross_wq"]], True, tm, "cross_q")
    o = _xattn(q.reshape(b, t, d), mem[0], mem[1], mem_blocks[0], mem_blocks[1], cfg["tq_x"])
    x2 = _out_proj(x2, o.reshape(n, d), w["cross_wo"], w["cross_norm_post"], tm)
    x2 = _ffn(x2, w["ffn2_norm_pre"], w["ffn2_w_up"], w["ffn2_w_down"], w["ffn2_norm_post"], cfg["tm_ffn"])
    nkv = ATT_KV_HEADS * ATT_HEAD_DIM
    state = (c_rw[:, t_valid - 1], s_rwkv, s_hgrn,
             c_att[:, :t_valid, ATT_K0:ATT_K0 + nkv].reshape(b, t_valid, ATT_KV_HEADS, ATT_HEAD_DIM),
             c_att[:, :t_valid, ATT_V0:ATT_V0 + nkv].reshape(b, t_valid, ATT_KV_HEADS, ATT_HEAD_DIM),
             c_att[:, :t_valid, ATT_KI0:ATT_KI0 + IDX_DIM])
    return x2.reshape(b, t, d), state


PROMPT_CFG = dict(tm=512, tm_ffn=512, tm_prep=256, rwkv_chunk=64, hgrn_chunk=16, hgrn_tb=256, tq_x=512)
SAMPLE_PAD = 8
SAMPLE_CFG = dict(tm=256, tm_ffn=256, tm_prep=SAMPLE_PAD, rwkv_chunk=SAMPLE_PAD, hgrn_chunk=SAMPLE_PAD,
                  hgrn_tb=SAMPLE_PAD, tq_x=SAMPLE_PAD)


def kernel(x_prompt, x_sample, cache_k, cache_v, cache_kidx, cache_mem_k, cache_mem_v, state_rwkv, state_rwkv_shift, state_hgrn, page_table, mem_prompt, rel_bias, hgrn_lb_logits, ffn1_norm_pre, ffn1_norm_post, ffn1_w_up, ffn1_w_down, mix_norm_pre, mix_norm_post, mix_w_in, rwkv_mu, rwkv_w0, rwkv_w_up, rwkv_a0, rwkv_a_up, rwkv_g_up, rwkv_k_k, rwkv_k_a, rwkv_r_k, rwkv_ln_w, rwkv_ln_b, hgrn_norm_w, mix_w_proj_a, mix_w_proj_b, mix_w_proj_c, mix_w_out, cross_norm_pre, cross_norm_post, cross_wq, cross_wk, cross_wv, cross_wo, ffn2_norm_pre, ffn2_norm_post, ffn2_w_up, ffn2_w_down):
    depth, d = ffn1_norm_pre.shape
    bp, tp, _ = x_prompt.shape
    bs, ts, _ = x_sample.shape
    d_r = rwkv_w0.shape[1]
    rw_cols = rwkv_mu.shape[1]
    d_h = hgrn_norm_w.shape[1]
    n_att = ATT_WI0 + IDX_HEADS - ATT_Q0
    bf = lambda a: a.astype(BF16)
    vec = lambda a: a[:, None, :]

    o1 = rw_cols + n_att
    w_att = jnp.pad(mix_w_in[:, :, rw_cols:o1], ((0, 0), (0, 0), (0, ATT_W - n_att)))
    head_sum = (np.arange(d_r)[:, None] // RWKV_HEAD == np.arange(d_r)[None, :] // RWKV_HEAD).astype(np.float32)
    per_head = lambda a: a.reshape(depth, RWKV_HEADS, 1, RWKV_HEAD)
    weights = dict(
        ffn1_norm_pre=vec(ffn1_norm_pre), ffn1_norm_post=vec(ffn1_norm_post), ffn1_w_up=bf(ffn1_w_up), ffn1_w_down=bf(ffn1_w_down),
        ffn2_norm_pre=vec(ffn2_norm_pre), ffn2_norm_post=vec(ffn2_norm_post), ffn2_w_up=bf(ffn2_w_up), ffn2_w_down=bf(ffn2_w_down),
        mix_norm_pre=vec(mix_norm_pre), mix_norm_post=vec(mix_norm_post),
        w_in_rw=bf(mix_w_in[:, :, :rw_cols]), w_in_att=bf(w_att),
        w_in_hg=bf(mix_w_in[:, :, o1:o1 + 4 * d_h]), w_in_gate=bf(mix_w_in[:, :, o1 + 4 * d_h:]),
        rwkv_mu=vec(rwkv_mu), rwkv_w0=vec(rwkv_w0), rwkv_w_up=bf(rwkv_w_up), rwkv_a0=vec(rwkv_a0), rwkv_a_up=bf(rwkv_a_up),
        rwkv_g_up=bf(rwkv_g_up), rwkv_k_k=vec(rwkv_k_k), rwkv_k_a=vec(rwkv_k_a),
        rwkv_r_k=rwkv_r_k[:, :, None, :], rwkv_ln_w=per_head(rwkv_ln_w), rwkv_ln_b=per_head(rwkv_ln_b),
        hgrn_norm_w=vec(hgrn_norm_w),
        mix_w_proj_a=bf(mix_w_proj_a), mix_w_proj_b=bf(mix_w_proj_b), mix_w_proj_c=bf(mix_w_proj_c), mix_w_out=bf(mix_w_out),
        cross_norm_pre=vec(cross_norm_pre), cross_norm_post=vec(cross_norm_post), cross_wq=bf(cross_wq), cross_wo=bf(cross_wo),
        cross_wkv=bf(jnp.concatenate([cross_wk, cross_wv], axis=-1)),
    )
    p_lb = jax.nn.softmax(hgrn_lb_logits.astype(F32), axis=0)
    lower_bounds = vec(jnp.cumsum(p_lb, axis=0) - p_lb[0:1])

    bias = _bias_tiles(rel_bias)
    bias_t = jnp.transpose(bias, (0, 3, 1, 2)).reshape(3, Q_BLOCK, ATT_HEADS * Q_BLOCK)
    bias_s = bias[:, :, :SAMPLE_PAD, :].reshape(3, ATT_HEADS * SAMPLE_PAD, Q_BLOCK)

    pad_t = SAMPLE_PAD - ts
    xs0 = jnp.pad(x_sample, ((0, 0), (0, pad_t), (0, 0)))
    n_pool = cache_k.shape[1]
    k_pool = cache_k.reshape(depth, n_pool, PAGE, ATT_KV_HEADS * ATT_HEAD_DIM)
    v_pool = cache_v.reshape(depth, n_pool, PAGE, ATT_KV_HEADS * ATT_HEAD_DIM)
    mem_tokens = mem_prompt.shape[1]
    mem2 = mem_prompt.reshape(bp * mem_tokens, d)
    ones_d = jnp.ones((1, d), F32)
    zero_shift = jnp.zeros((bp, 1, rw_cols), F32)
    zero_rwkv = jnp.zeros((bp, RWKV_HEADS, RWKV_HEAD, RWKV_HEAD), F32)
    zero_hgrn = jnp.zeros((bp, HGRN_HEADS, HGRN_EXPAND, HGRN_EXPAND), F32)
    prompt_cfg = dict(PROMPT_CFG, t_valid=tp)
    sample_cfg = dict(SAMPLE_CFG, t_valid=ts)

    def layer(carry, per_layer):
        xp, xs = carry
        w, lb, mem_ks, mem_vs, s_rw, s_sh, s_hg, li = per_layer
        w = dict(w, head_sum=jnp.asarray(head_sum, BF16))
        (mem_kv,) = _proj(mem2, ones_d, [w["cross_wkv"]], False, 256, "mem_kv")
        mem_kv = mem_kv.reshape(bp, mem_tokens, 2 * d)
        xp, st_p = _group_layer(xp, w, lb, (mem_kv, mem_kv), (0, 1), zero_shift, zero_rwkv, zero_hgrn,
                                lambda c: _dsa_prompt(c, bias_t), prompt_cfg)
        attend_s = lambda c: _dsa_sample(c, ts, li, page_table, cache_kidx, k_pool, v_pool, bias_s)
        xs, st_s = _group_layer(xs, w, lb, (mem_ks, mem_vs), (0, 0), s_sh, s_rw, s_hg, attend_s, sample_cfg)
        mem_k = mem_kv[:, :, :d].reshape(bp, mem_tokens, MEM_HEADS, d // MEM_HEADS)
        mem_v = mem_kv[:, :, d:].reshape(bp, mem_tokens, MEM_HEADS, d // MEM_HEADS)
        return (xp, xs), (st_p, (mem_k, mem_v), st_s)

    per_layer = (weights, lower_bounds,
                 cache_mem_k.reshape(depth, bs, mem_tokens, d), cache_mem_v.reshape(depth, bs, mem_tokens, d),
                 state_rwkv, state_rwkv_shift[:, :, None, :], state_hgrn,
                 jnp.arange(depth, dtype=I32)[:, None])
    (xp, xs), (st_p, (mem_k, mem_v), st_s) = lax.scan(layer, (x_prompt, xs0), per_layer)
    sh_p, rw_p, hg_p, k_p, v_p, ki_p = st_p
    sh_s, rw_s, hg_s, k_s, v_s, ki_s = st_s
    return (xp, xs[:, :ts], k_p, v_p, ki_p, mem_k, mem_v, rw_p, sh_p, hg_p, k_s, v_s, ki_s, rw_s, sh_s, hg_s)
```

```python
import functools
import math

import jax
import jax.numpy as jnp
import numpy as np
from jax import lax
from jax.experimental import pallas as pl
from jax.experimental.pallas import tpu as pltpu

F32 = jnp.float32
BF16 = jnp.bfloat16
I32 = jnp.int32

LANES = 128
SUBLANES = 8
VMEM_LIMIT_BYTES = 56 * 1024 * 1024

NORM_EPS = 1e-6
RWKV_LN_EPS = 64e-5
LB_FLOOR = 1e-30
NEG_INF = -1e30
INT_MIN = int(np.iinfo(np.int32).min)

RWKV_HEAD = 64
RWKV_HEADS = 8
RWKV_GROUP = 4
HGRN_HEADS = 4
HGRN_EXPAND = 128
ATT_HEADS = 8
ATT_KV_HEADS = 2
ATT_HEAD_DIM = 64
IDX_HEADS = 4
IDX_DIM = 64
TOPK = 256
Q_BLOCK = 128
MEM_HEADS = 4
NUM_BUCKETS = 32
MAX_DISTANCE = 128
PAGE = 128

ATT_Q0, ATT_K0, ATT_V0, ATT_QI0, ATT_KI0, ATT_WI0, ATT_W = 0, 512, 640, 768, 1024, 1088, 1152


def _cparams(sem):
    return pltpu.CompilerParams(dimension_semantics=sem, vmem_limit_bytes=VMEM_LIMIT_BYTES)


def _mm(a, b):
    return jnp.dot(a.astype(BF16), b.astype(BF16), preferred_element_type=F32)


def _mm_nt(a, b):
    return lax.dot_general(a.astype(BF16), b.astype(BF16), (((1,), (1,)), ((), ())),
                           preferred_element_type=F32)


def _mm_tn(a, b):
    return lax.dot_general(a.astype(BF16), b.astype(BF16), (((0,), (0,)), ((), ())),
                           preferred_element_type=F32)


def _bf16_round(x):
    return x.astype(BF16).astype(F32)


def _split3(x):
    hi = x.astype(BF16)
    r1 = x - hi.astype(F32)
    mid = r1.astype(BF16)
    lo = (r1 - mid.astype(F32)).astype(BF16)
    return hi, mid, lo


def _mm_exact_lhs(a01, x):
    a = a01.astype(BF16)
    hi, mid, lo = _split3(x)
    d = lambda p: jnp.dot(a, p, preferred_element_type=F32)
    return d(hi) + d(mid) + d(lo)


def _mm_exact_rhs(x, b01):
    b = b01.astype(BF16)
    hi, mid, lo = _split3(x)
    d = lambda p: jnp.dot(p, b, preferred_element_type=F32)
    return d(hi) + d(mid) + d(lo)


def _rms(x, g):
    return x * lax.rsqrt(jnp.mean(x * x, axis=-1, keepdims=True) + NORM_EPS) * g


def _sigmoid(x):
    return jax.nn.sigmoid(x)


def _softplus(x):
    return jnp.maximum(x, 0.0) + jnp.log1p(jnp.exp(-jnp.abs(x)))


def _full(shape):
    nd = len(shape)
    return pl.BlockSpec(shape, lambda *_: (0,) * nd)


def _row_tile(n, want):
    t = min(n, want)
    assert n % t == 0, (n, t)
    return t


def _ffn_kernel(x_ref, gpre_ref, wup_ref, wdn_ref, gpost_ref, o_ref, *, d_ff, tf):
    x = x_ref[...]
    h = _rms(x, gpre_ref[...]).astype(BF16)
    acc = jnp.zeros(x.shape, F32)
    for f0 in range(0, d_ff, tf):
        gate = jnp.dot(h, wup_ref[:, f0:f0 + tf], preferred_element_type=F32)
        up = jnp.dot(h, wup_ref[:, d_ff + f0:d_ff + f0 + tf], preferred_element_type=F32)
        act = (gate * _sigmoid(gate) * up).astype(BF16)
        acc = acc + jnp.dot(act, wdn_ref[f0:f0 + tf, :], preferred_element_type=F32)
    o_ref[...] = x + 0.5 * _rms(acc, gpost_ref[...])


def _ffn(x, gpre, wup, wdn, gpost, tm):
    n, d = x.shape
    d_ff = wdn.shape[0]
    tf = d_ff // 2 if (d_ff // 2) % LANES == 0 else d_ff
    tm = _row_tile(n, tm)
    return pl.pallas_call(
        functools.partial(_ffn_kernel, d_ff=d_ff, tf=tf),
        grid=(n // tm,),
        in_specs=[pl.BlockSpec((tm, d), lambda i: (i, 0)), _full((1, d)), _full(wup.shape), _full(wdn.shape),
                  _full((1, d))],
        out_specs=pl.BlockSpec((tm, d), lambda i: (i, 0)),
        out_shape=jax.ShapeDtypeStruct((n, d), F32),
        compiler_params=_cparams(("parallel",)),
        name="ffn",
    )(x, gpre, wup, wdn, gpost)


def _proj_kernel(*refs, n_out, norm, tn):
    x_ref, g_ref = refs[0], refs[1]
    w_refs = refs[2:2 + n_out]
    o_refs = refs[2 + n_out:]
    x = x_ref[...]
    h = (_rms(x, g_ref[...]) if norm else x).astype(BF16)
    for w_ref, o_ref in zip(w_refs, o_refs):
        n = w_ref.shape[1]
        step = tn if n % tn == 0 else n
        for n0 in range(0, n, step):
            o_ref[:, n0:n0 + step] = jnp.dot(h, w_ref[:, n0:n0 + step], preferred_element_type=F32)


def _proj(x, g, ws, norm, tm, name):
    n, d = x.shape
    tm = _row_tile(n, tm)
    return pl.pallas_call(
        functools.partial(_proj_kernel, n_out=len(ws), norm=norm, tn=512),
        grid=(n // tm,),
        in_specs=[pl.BlockSpec((tm, d), lambda i: (i, 0)), _full((1, d))] + [_full(w.shape) for w in ws],
        out_specs=[pl.BlockSpec((tm, w.shape[1]), lambda i: (i, 0)) for w in ws],
        out_shape=[jax.ShapeDtypeStruct((n, w.shape[1]), F32) for w in ws],
        compiler_params=_cparams(("parallel",)),
        name=name,
    )(x, g, *ws)


def _out_kernel(x_ref, a_ref, w_ref, g_ref, o_ref):
    y = jnp.dot(a_ref[...].astype(BF16), w_ref[...], preferred_element_type=F32)
    o_ref[...] = x_ref[...] + _rms(y, g_ref[...])


def _out_proj(x, a, w, g, tm):
    n, d = x.shape
    tm = _row_tile(n, tm)
    return pl.pallas_call(
        _out_kernel,
        grid=(n // tm,),
        in_specs=[pl.BlockSpec((tm, d), lambda i: (i, 0)), pl.BlockSpec((tm, a.shape[1]), lambda i: (i, 0)),
                  _full(w.shape), _full((1, d))],
        out_specs=pl.BlockSpec((tm, d), lambda i: (i, 0)),
        out_shape=jax.ShapeDtypeStruct((n, d), F32),
        compiler_params=_cparams(("parallel",)),
        name="out_proj",
    )(x, a, w, g)


def _merge_kernel(x_ref, oa_ref, ob_ref, oc_ref, gt_ref, wa_ref, wb_ref, wc_ref, wo_ref, g_ref, o_ref, *, d):
    m = jnp.zeros((x_ref.shape[0], d), F32)
    for j, (o_r, w_r) in enumerate(((oa_ref, wa_ref), (ob_ref, wb_ref), (oc_ref, wc_ref))):
        p = jnp.dot(o_r[...].astype(BF16), w_r[...], preferred_element_type=F32)
        m = m + _sigmoid(gt_ref[:, j * d:(j + 1) * d]) * p
    y = jnp.dot(m.astype(BF16), wo_ref[...], preferred_element_type=F32)
    o_ref[...] = x_ref[...] + _rms(y, g_ref[...])


def _merge(x, oa, ob, oc, gates, wa, wb, wc, wo, g, tm):
    n, d = x.shape
    tm = _row_tile(n, tm)
    row = lambda w: pl.BlockSpec((tm, w), lambda i: (i, 0))
    return pl.pallas_call(
        functools.partial(_merge_kernel, d=d),
        grid=(n // tm,),
        in_specs=[row(d), row(oa.shape[1]), row(ob.shape[1]), row(oc.shape[1]), row(gates.shape[1]),
                  _full(wa.shape), _full(wb.shape), _full(wc.shape), _full(wo.shape), _full((1, d))],
        out_specs=row(d),
        out_shape=jax.ShapeDtypeStruct((n, d), F32),
        compiler_params=_cparams(("parallel",)),
        name="merge",
    )(x, oa, ob, oc, gates, wa, wb, wc, wo, g)


def _xattn_kernel(q_ref, mk_ref, mv_ref, o_ref, *, heads):
    hd = q_ref.shape[2] // heads
    for h in range(heads):
        sl = slice(h * hd, (h + 1) * hd)
        lg = _mm_nt(q_ref[0, :, sl], mk_ref[0, :, sl]) * hd ** -0.5
        lg = lg - jnp.max(lg, axis=-1, keepdims=True)
        p = jnp.exp(lg)
        p = p / jnp.sum(p, axis=-1, keepdims=True)
        o_ref[0, :, sl] = _mm(p, mv_ref[0, :, sl])


def _xattn(q, mem_k, mem_v, k_blk, v_blk, tq):
    b, t, d = q.shape
    s = mem_k.shape[1]
    tq = _row_tile(t, tq)
    return pl.pallas_call(
        functools.partial(_xattn_kernel, heads=MEM_HEADS),
        grid=(b, t // tq),
        in_specs=[pl.BlockSpec((1, tq, d), lambda i, j: (i, j, 0)),
                  pl.BlockSpec((1, s, d), lambda i, j: (i, 0, k_blk)),
                  pl.BlockSpec((1, s, d), lambda i, j: (i, 0, v_blk))],
        out_specs=pl.BlockSpec((1, tq, d), lambda i, j: (i, j, 0)),
        out_shape=jax.ShapeDtypeStruct((b, t, d), F32),
        compiler_params=_cparams(("parallel", "parallel")),
        name="xattn",
    )(q, mem_k, mem_v)


def _rwkv_prep_kernel(c_ref, sh_ref, mu_ref, w0_ref, wup_ref, a0_ref, aup_ref, gup_ref, kk_ref, ka_ref, hs_ref,
                      r_o, lw_o, k_o, v_o, kk_o, b_o, g_o, carry_ref, *, t_valid, d_r):
    j = pl.program_id(1)
    tm = c_ref.shape[1]

    @pl.when(j == 0)
    def _():
        carry_ref[...] = sh_ref[0]

    c = c_ref[0]
    row = lax.broadcasted_iota(I32, c.shape, 0)
    prev = jnp.where(row == 0, carry_ref[...], pltpu.roll(c, 1, axis=0))
    carry_ref[...] = c[tm - 1:tm, :]
    csh = c + (prev - c) * mu_ref[...]
    r, k, v = csh[:, 0:d_r], csh[:, d_r:2 * d_r], csh[:, 2 * d_r:3 * d_r]
    o = 3 * d_r
    n_w, n_a, n_g = wup_ref.shape[0], aup_ref.shape[0], gup_ref.shape[0]
    xw, xa, xg = csh[:, o:o + n_w], csh[:, o + n_w:o + n_w + n_a], csh[:, o + n_w + n_a:o + n_w + n_a + n_g]
    w_log = -_softplus(-(w0_ref[...] + _mm(jnp.tanh(xw), wup_ref[...]))) - 0.5
    lw = -jnp.exp(w_log)
    a_lr = _sigmoid(a0_ref[...] + _mm(xa, aup_ref[...]))
    g = _mm(_sigmoid(xg), gup_ref[...])
    kk = k * kk_ref[...]
    ss = _mm_exact_rhs(kk * kk, hs_ref[...])
    kk = kk / jnp.maximum(jnp.sqrt(ss), 1e-12)
    k_mod = k * (1.0 + (a_lr - 1.0) * ka_ref[...])
    b = kk * a_lr
    if t_valid < tm:
        ok = (lax.broadcasted_iota(I32, (tm, d_r), 0) + j * tm) < t_valid
        z = lambda t: jnp.where(ok, t, 0.0)
        lw, k_mod, v, kk, b = z(lw), z(k_mod), z(v), z(kk), z(b)
    for h in range(RWKV_HEADS):
        sl = slice(h * RWKV_HEAD, (h + 1) * RWKV_HEAD)
        for o_ref, val in ((r_o, r), (lw_o, lw), (k_o, k_mod), (v_o, v), (kk_o, kk), (b_o, b), (g_o, g)):
            o_ref[0, h] = val[:, sl]


def _rwkv_prep(c_rw, shift, p, t_valid, tm):
    b, t, cols = c_rw.shape
    d_r = p["rwkv_w0"].shape[1]
    tm = _row_tile(t, tm)
    hm = jax.ShapeDtypeStruct((b, RWKV_HEADS, t, RWKV_HEAD), F32)
    hm_spec = pl.BlockSpec((1, RWKV_HEADS, tm, RWKV_HEAD), lambda i, j: (i, 0, j, 0))
    params = [p["rwkv_mu"], p["rwkv_w0"], p["rwkv_w_up"], p["rwkv_a0"], p["rwkv_a_up"], p["rwkv_g_up"],
              p["rwkv_k_k"], p["rwkv_k_a"], p["head_sum"]]
    return pl.pallas_call(
        functools.partial(_rwkv_prep_kernel, t_valid=t_valid, d_r=d_r),
        grid=(b, t // tm),
        in_specs=[pl.BlockSpec((1, tm, cols), lambda i, j: (i, j, 0)),
                  pl.BlockSpec((1, 1, cols), lambda i, j: (i, 0, 0))] + [_full(a.shape) for a in params],
        out_specs=[hm_spec] * 7,
        out_shape=[hm] * 7,
        scratch_shapes=[pltpu.VMEM((1, cols), F32)],
        compiler_params=_cparams(("parallel", "arbitrary")),
        name="rwkv_prep",
    )(c_rw, shift, *params)


def _rwkv_kernel(r_ref, lw_ref, k_ref, v_ref, kk_ref, b_ref, g_ref, s0_ref, lnw_ref, lnb_ref, rk_ref,
                 o_ref, so_ref, s_ref, *, chunk):
    t_idx = pl.program_id(1)
    nh, C, N = RWKV_GROUP, chunk, RWKV_HEAD
    R = nh * C
    n_seq, n_heads = r_ref.shape[0], r_ref.shape[1]

    @pl.when(t_idx == 0)
    def _():
        s_ref[...] = s0_ref[...]

    row = lax.broadcasted_iota(I32, (R, R), 0)
    col = lax.broadcasted_iota(I32, (R, R), 1)
    same = (row // C) == (col // C)
    incl = jnp.logical_and(same, col <= row)
    strict = jnp.logical_and(same, col < row)
    incl01 = jnp.where(incl, 1.0, 0.0)
    eye_r = jnp.where(row == col, 1.0, 0.0)
    levels = []
    m_blk = 1
    while m_blk < C:
        levels.append(jnp.logical_and(row // (2 * m_blk) == col // (2 * m_blk),
                                      jnp.logical_and((row // m_blk) % 2 == 1, (col // m_blk) % 2 == 0)))
        m_blk *= 2
    eye = jnp.where(lax.broadcasted_iota(I32, (N, N), 0) == lax.broadcasted_iota(I32, (N, N), 1), 1.0, 0.0)

    chains = [(bi, h0) for bi in range(n_seq) for h0 in range(0, n_heads, nh)]
    each = lambda f, *cols: [f(*vals) for vals in zip(*cols)]
    ld = lambda ref: [ref[bi, h0:h0 + nh].reshape(R, N) for bi, h0 in chains]
    r, lw, k, v, kk, b, g = (ld(x) for x in (r_ref, lw_ref, k_ref, v_ref, kk_ref, b_ref, g_ref))
    cum = each(lambda t: _mm_exact_lhs(incl01, t), lw)
    e_neg = each(lambda c: jnp.exp(-c), cum)
    at = each(lambda kk_, c, l: -kk_ * jnp.exp(c - l), kk, cum, lw)
    rt = each(lambda r_, c: r_ * jnp.exp(c), r, cum)
    bt = each(lambda b_, e: b_ * e, b, e_neg)
    kt = each(lambda k_, e: k_ * e, k, e_neg)
    gram = each(lambda a_, r_, b_, k_: _mm_nt(jnp.concatenate([a_, r_], axis=0), jnp.concatenate([b_, k_], axis=0)),
                at, rt, bt, kt)
    a_ab = each(lambda gm: jnp.where(strict, gm[:R, :R], 0.0), gram)
    a_ak = each(lambda gm: jnp.where(strict, gm[:R, R:], 0.0), gram)
    a_rb = each(lambda gm: jnp.where(incl, gm[R:, :R], 0.0), gram)
    a_rk = each(lambda gm: jnp.where(incl, gm[R:, R:], 0.0), gram)
    x = each(lambda a_, ak, v_: jnp.concatenate([a_, _mm(ak, v_)], axis=1), at, a_ak, v)
    d_inv = [eye_r for _ in chains]
    for li, lower_left in enumerate(levels):
        a_off = each(lambda ab: jnp.where(lower_left, ab, 0.0), a_ab)
        if li == 0:
            d_inv = each(lambda d, ao: d + ao, d_inv, a_off)
        else:
            half = each(lambda ao, d: _mm(ao, d), a_off, d_inv)
            d_inv = each(lambda d, hf: d + _mm(d, hf), d_inv, half)
    x = each(lambda d, x_: _mm(d, x_), d_inv, x)
    ax = each(lambda rb, x_: _mm(rb, x_), a_rb, x)
    qt = each(lambda r_, ax_: r_ + ax_[:, :N], rt, ax)
    y0 = each(lambda ax_, rk_, v_: ax_[:, N:] + _mm(rk_, v_), ax, a_rk, v)
    for ci, (bi, h0) in enumerate(chains):
        outs = []
        for h in range(nh):
            sl = slice(h * C, (h + 1) * C)
            p_c = jnp.exp(cum[ci][h * C + C - 1:h * C + C, :])
            m = (eye + _mm_tn(x[ci][sl, :N], bt[ci][sl])) * p_c
            s_loc = _mm_tn(jnp.concatenate([x[ci][sl, N:], v[ci][sl]], axis=0),
                           jnp.concatenate([bt[ci][sl], kt[ci][sl]], axis=0)) * p_c
            s = s_ref[bi, h0 + h]
            y = _mm_nt(qt[ci][sl], s) + y0[ci][sl]
            s_ref[bi, h0 + h] = _mm(s, m) + s_loc
            mu = jnp.mean(y, axis=-1, keepdims=True)
            yc = y - mu
            var = jnp.mean(yc * yc, axis=-1, keepdims=True)
            yn = yc * lax.rsqrt(var + RWKV_LN_EPS) * lnw_ref[h0 + h] + lnb_ref[h0 + h]
            yn = yn + jnp.sum(r[ci][sl] * k[ci][sl] * rk_ref[h0 + h], axis=-1, keepdims=True) * v[ci][sl]
            outs.append(yn * g[ci][sl])
        o_ref[bi, :, h0 * N:(h0 + nh) * N] = jnp.concatenate(outs, axis=1)

    @pl.when(t_idx == pl.num_programs(1) - 1)
    def _():
        so_ref[...] = s_ref[...]


def _rwkv(streams, s0, lnw, lnb, rk, chunk, n_seq):
    b, nh, t, n = streams[0].shape
    n_seq = _row_tile(b, n_seq)
    st_spec = pl.BlockSpec((n_seq, nh, chunk, n), lambda i, j: (i, 0, j, 0))
    s_spec = pl.BlockSpec((n_seq, nh, n, n), lambda i, j: (i, 0, 0, 0))
    p_spec = _full((nh, 1, n))
    return pl.pallas_call(
        functools.partial(_rwkv_kernel, chunk=chunk),
        grid=(b // n_seq, t // chunk),
        in_specs=[st_spec] * 7 + [s_spec, p_spec, p_spec, p_spec],
        out_specs=[pl.BlockSpec((n_seq, chunk, nh * n), lambda i, j: (i, j, 0)), s_spec],
        out_shape=[jax.ShapeDtypeStruct((b, t, nh * n), F32), jax.ShapeDtypeStruct((b, nh, n, n), F32)],
        scratch_shapes=[pltpu.VMEM((n_seq, nh, n, n), F32)],
        compiler_params=_cparams(("parallel", "arbitrary")),
        name="rwkv",
    )(*streams, s0, lnw, lnb, rk)


def _hgrn_kernel(c_ref, lb_ref, nw_ref, s0_ref, o_ref, so_ref, st_ref, cum_ref, kh_ref, *, chunk, t_valid):
    j = pl.program_id(1)
    tb = c_ref.shape[1]
    d = lb_ref.shape[1]
    K = HGRN_EXPAND
    c = chunk

    @pl.when(j == 0)
    def _():
        for h in range(HGRN_HEADS):
            st_ref[h] = s0_ref[0, h].T

    z = c_ref[0, :, d:2 * d]
    lb = lb_ref[...]
    ls = -_softplus(-z)
    x1 = jnp.log(jnp.maximum(lb, LB_FLOOR))
    x2 = jnp.log1p(-lb) + ls
    logf = jnp.maximum(x1, x2) + jnp.log1p(jnp.exp(-jnp.abs(x1 - x2)))
    kh = (1.0 - lb) * _sigmoid(-z)
    if t_valid < tb:
        ok = (lax.broadcasted_iota(I32, (tb, d), 0) + j * tb) < t_valid
        logf = jnp.where(ok, logf, 0.0)
        kh = jnp.where(ok, kh, 0.0)
    row = lax.broadcasted_iota(I32, (tb, tb), 0)
    col = lax.broadcasted_iota(I32, (tb, tb), 1)
    tri = jnp.where(jnp.logical_and(row // c == col // c, col <= row), 1.0, 0.0)
    cum_ref[...] = _mm_exact_lhs(tri, logf)
    kh_ref[...] = kh

    ones = jnp.ones((K, K), BF16)
    rr = lax.broadcasted_iota(I32, (c * c, K), 0)
    causal = (rr % c) <= (rr // c)
    sel = jnp.where(lax.broadcasted_iota(I32, (c, c * c), 1) // c == lax.broadcasted_iota(I32, (c, c * c), 0),
                    1.0, 0.0).astype(BF16)

    heads = range(HGRN_HEADS)
    each = lambda f, *cols: [f(*vals) for vals in zip(*cols)]

    def body(ci, carry):
        r0 = pl.multiple_of(ci * c, c)
        rows = pl.ds(r0, c)
        hq = [c_ref[0, rows, h * K:(h + 1) * K] for h in heads]
        q = each(lambda t: t * _sigmoid(t), hq)
        v = [c_ref[0, rows, 2 * d + h * K:2 * d + (h + 1) * K] for h in heads]
        hg = [c_ref[0, rows, 3 * d + h * K:3 * d + (h + 1) * K] for h in heads]
        bq = [cum_ref[rows, h * K:(h + 1) * K] for h in heads]
        kq = [kh_ref[rows, h * K:(h + 1) * K] for h in heads]
        e = each(lambda q_, k_, b_: jnp.concatenate(
            [q_[t:t + 1] * k_ * jnp.exp(jnp.minimum(b_[t:t + 1] - b_, 0.0)) for t in range(c)], axis=0),
            q, kq, bq)
        att = each(lambda e_: jnp.dot(e_.astype(BF16), ones, preferred_element_type=F32), e)
        w = each(lambda a_, v_: jnp.where(causal, a_, 0.0) * jnp.concatenate([v_] * c, axis=0), att, v)
        st = [st_ref[h] for h in heads]
        o = each(lambda w_, q_, b_, s_: jnp.dot(sel, w_.astype(BF16), preferred_element_type=F32)
                 + _mm_nt(q_ * jnp.exp(b_), s_), w, q, bq, st)
        upd = each(lambda s_, b_, v_, k_: s_ * jnp.exp(b_[c - 1:c]) + _mm_tn(v_, k_ * jnp.exp(b_[c - 1:c] - b_)),
                   st, bq, v, kq)
        for h in heads:
            st_ref[h] = upd[h]
            on = o[h] * lax.rsqrt(jnp.mean(o[h] * o[h], axis=-1, keepdims=True) + NORM_EPS)
            o_ref[0, rows, h * K:(h + 1) * K] = on * nw_ref[:, h * K:(h + 1) * K] * (hg[h] * _sigmoid(hg[h]))
        return carry

    lax.fori_loop(0, tb // c, body, 0, unroll=min(2, tb // c))

    @pl.when(j == pl.num_programs(1) - 1)
    def _():
        for h in range(HGRN_HEADS):
            so_ref[0, h] = st_ref[h].T


def _hgrn(c_hg, lb, nw, s0, chunk, t_valid, tb):
    b, t, cols = c_hg.shape
    d = cols // 4
    tb = _row_tile(t, tb)
    s_spec = pl.BlockSpec((1, HGRN_HEADS, HGRN_EXPAND, HGRN_EXPAND), lambda i, j: (i, 0, 0, 0))
    return pl.pallas_call(
        functools.partial(_hgrn_kernel, chunk=chunk, t_valid=t_valid),
        grid=(b, t // tb),
        in_specs=[pl.BlockSpec((1, tb, cols), lambda i, j: (i, j, 0)), _full((1, d)), _full((1, d)), s_spec],
        out_specs=[pl.BlockSpec((1, tb, d), lambda i, j: (i, j, 0)), s_spec],
        out_shape=[jax.ShapeDtypeStruct((b, t, d), F32), jax.ShapeDtypeStruct(s0.shape, F32)],
        scratch_shapes=[pltpu.VMEM((HGRN_HEADS, HGRN_EXPAND, HGRN_EXPAND), F32), pltpu.VMEM((tb, d), F32),
                        pltpu.VMEM((tb, d), F32)],
        compiler_params=_cparams(("parallel", "arbitrary")),
        name="hgrn",
    )(c_hg, lb, nw, s0)


def _ordered_key(s):
    bits = pltpu.bitcast(s + 0.0, I32)
    return bits ^ ((bits >> 31) & 0x7FFFFFFF)


def _kth_largest(count_ge, shape, k):
    c0 = count_ge(jnp.zeros(shape, I32))
    thr = jnp.where(c0 >= k, 0, INT_MIN).astype(I32)

    def body(i, thr):
        cand = thr | jnp.left_shift(jnp.int32(1), 30 - i)
        return jnp.where(count_ge(cand) >= k, cand, thr)

    return lax.fori_loop(0, 31, body, thr)


DSA_GROUP = 4


def _dsa_prompt_kernel(q_ref, qi_ref, kw_ref, k_ref, v_ref, ki_ref, bias_ref, o_ref,
                       key_ref, m_ref, l_ref, acc_ref, *, topk):
    i = pl.program_id(1)
    QB, HD = Q_BLOCK, ATT_HEAD_DIM
    zeros64 = jnp.zeros((HD, QB), F32)

    q_t = q_ref[0].T
    rep = ATT_HEADS // ATT_KV_HEADS
    tiles = []
    for h in range(ATT_HEADS):
        qh = q_t[h * HD:(h + 1) * HD]
        tiles.append(jnp.concatenate([qh, zeros64] if h // rep == 0 else [zeros64, qh], axis=0))
    qs_t = (jnp.concatenate(tiles, axis=1) * HD ** -0.5).astype(BF16)
    qi_t = qi_ref[0].T * IDX_DIM ** -0.5
    qi_pad = jnp.concatenate([jnp.concatenate([qi_t[h * IDX_DIM:(h + 1) * IDX_DIM], zeros64], axis=0)
                              for h in range(IDX_HEADS)], axis=1).astype(BF16)
    w_t = kw_ref[0].T
    w_rows = [_bf16_round(w_t[IDX_DIM + h:IDX_DIM + h + 1] * IDX_HEADS ** -0.5) for h in range(IDX_HEADS)]

    row = lax.broadcasted_iota(I32, (QB, QB), 0)
    col = lax.broadcasted_iota(I32, (QB, QB), 1)
    n_grp = (i + DSA_GROUP) // DSA_GROUP

    def score_group(gi, carry):
        for u in range(DSA_GROUP):
            j = gi * DSA_GROUP + u
            r0 = pl.multiple_of(j * QB, QB)
            dots = jnp.dot(ki_ref[0, pl.ds(r0, QB), :].astype(BF16), qi_pad, preferred_element_type=F32)
            s = jnp.zeros((QB, QB), F32)
            for h in range(IDX_HEADS):
                s = s + _bf16_round(jnp.maximum(dots[:, h * QB:(h + 1) * QB], 0.0)) * w_rows[h]
            vis = (j * QB + row) <= (i * QB + col)
            key = _ordered_key(jnp.where(vis, s, NEG_INF))
            key_ref[pl.ds(r0, QB), :] = jnp.where(j <= i, key, INT_MIN)
        return carry

    lax.fori_loop(0, n_grp, score_group, 0)

    def count_ge(cand):
        rows = DSA_GROUP * QB

        def grp(gi, acc):
            r0 = pl.multiple_of(gi * rows, rows)
            hit = jnp.where(key_ref[pl.ds(r0, rows), :] >= cand, 1, 0)
            return acc + jnp.sum(hit.reshape(rows // SUBLANES, SUBLANES, QB), axis=0)
        acc = lax.fori_loop(0, n_grp, grp, jnp.zeros((SUBLANES, QB), I32))
        return jnp.sum(acc, axis=0, keepdims=True)

    thr = _kth_largest(count_ge, (1, QB), topk)
    need = (topk - count_ge(thr + 1)).astype(F32)

    m_ref[...] = jnp.full(m_ref.shape, NEG_INF, F32)
    l_ref[...] = jnp.zeros(l_ref.shape, F32)
    acc_ref[...] = jnp.zeros(acc_ref.shape, F32)
    tri = jnp.where(col <= row, 1.0, 0.0).astype(BF16)

    def attn_pair(jp, taken):
        lgs = []
        for u in range(2):
            j = 2 * jp + u
            r0 = pl.multiple_of(j * QB, QB)
            key = key_ref[pl.ds(r0, QB), :]
            eq = key == thr
            prefix = jnp.dot(tri, jnp.where(eq, 1.0, 0.0).astype(BF16), preferred_element_type=F32)
            sel = jnp.logical_or(key > thr, jnp.logical_and(eq, taken + prefix <= need))
            taken = taken + prefix[QB - 1:QB, :]
            vis = (j * QB + row) <= (i * QB + col)
            madd = jnp.where(jnp.logical_and(sel, vis), 0.0, NEG_INF)
            lg = jnp.dot(k_ref[0, pl.ds(r0, QB), :].astype(BF16), qs_t, preferred_element_type=F32)
            lgs.append(lg + bias_ref[jnp.clip(i - j, 0, 2)] + jnp.concatenate([madd] * ATT_HEADS, axis=1))
        m_old = m_ref[...]
        m_new = jnp.maximum(m_old, jnp.maximum(jnp.max(lgs[0], axis=0, keepdims=True),
                                               jnp.max(lgs[1], axis=0, keepdims=True)))
        alpha = jnp.exp(m_old - m_new)
        ps = [jnp.exp(lg - m_new) for lg in lgs]
        l_ref[...] = alpha * l_ref[...] + jnp.sum(ps[0], axis=0, keepdims=True) + jnp.sum(ps[1], axis=0, keepdims=True)
        v_pair = v_ref[0, pl.ds(pl.multiple_of(jp * 2 * QB, 2 * QB), 2 * QB), :]
        acc_ref[...] = alpha * acc_ref[...] + _mm(v_pair.T, jnp.concatenate([p.astype(BF16) for p in ps], axis=0))
        m_ref[...] = m_new
        return taken

    lax.fori_loop(0, (i + 2) // 2, attn_pair, jnp.zeros((1, QB), F32))

    out_t = (acc_ref[...] / l_ref[...]).T
    o_ref[0] = jnp.concatenate(
        [out_t[h * QB:(h + 1) * QB, (h // rep) * HD:(h // rep + 1) * HD] for h in range(ATT_HEADS)], axis=1)


def _dsa_prompt(c_att, bias_t):
    b, t, _ = c_att.shape
    nq = t // Q_BLOCK
    assert nq % DSA_GROUP == 0, (t, Q_BLOCK, DSA_GROUP)
    t_keys =((nq + DSA_GROUP - 1) // DSA_GROUP) * DSA_GROUP * Q_BLOCK
    blk = lambda w, cb: pl.BlockSpec((1, Q_BLOCK, w), lambda bi, i: (bi, i, cb))
    allk = lambda cb: pl.BlockSpec((1, t, LANES), lambda bi, i: (bi, 0, cb))
    return pl.pallas_call(
        functools.partial(_dsa_prompt_kernel, topk=min(TOPK, t // 4)),
        grid=(b, nq),
        in_specs=[blk(512, ATT_Q0 // 512), blk(256, ATT_QI0 // 256), blk(LANES, ATT_KI0 // LANES),
                  allk(ATT_K0 // LANES), allk(ATT_V0 // LANES), allk(ATT_KI0 // LANES), _full(bias_t.shape)],
        out_specs=pl.BlockSpec((1, Q_BLOCK, ATT_HEADS * ATT_HEAD_DIM), lambda bi, i: (bi, i, 0)),
        out_shape=jax.ShapeDtypeStruct((b, t, ATT_HEADS * ATT_HEAD_DIM), F32),
        scratch_shapes=[pltpu.VMEM((t_keys, Q_BLOCK), I32), pltpu.VMEM((1, ATT_HEADS * Q_BLOCK), F32),
                        pltpu.VMEM((1, ATT_HEADS * Q_BLOCK), F32), pltpu.VMEM((LANES, ATT_HEADS * Q_BLOCK), F32)],
        compiler_params=_cparams(("parallel", "arbitrary")),
        name="dsa_prompt",
    )(c_att, c_att, c_att, c_att, c_att, c_att, bias_t)


def _dsa_sample_kernel(pt_ref, l_ref, kidx_hbm, k_hbm, v_hbm, c_ref, bias_ref, o_ref,
                       ki_buf, k_buf, v_buf, sems, key_ref, madd_ref, *, n_pages, tq, topk):
    b = pl.program_id(0)
    slot = b % 2
    layer = l_ref[0]
    HD = ATT_HEAD_DIM
    rep = ATT_HEADS // ATT_KV_HEADS
    n_past = n_pages * PAGE

    def page_copies(page, p, sl):
        rows = pl.ds(pl.multiple_of(p * PAGE, PAGE), PAGE)
        return (pltpu.make_async_copy(kidx_hbm.at[layer, page], ki_buf.at[sl, rows], sems.at[0, sl]),
                pltpu.make_async_copy(k_hbm.at[layer, page], k_buf.at[sl, rows], sems.at[1, sl]),
                pltpu.make_async_copy(v_hbm.at[layer, page], v_buf.at[sl, rows], sems.at[2, sl]))

    def start_gather(bi, sl):
        def body(p, carry):
            for cp in page_copies(pt_ref[bi * n_pages + p], p, sl):
                cp.start()
            return carry
        lax.fori_loop(0, n_pages, body, 0)

    def wait_gather(sl):
        def body(p, carry):
            for cp in page_copies(0, p, sl):
                cp.wait()
            return carry
        lax.fori_loop(0, n_pages, body, 0)

    @pl.when(b == 0)
    def _():
        start_gather(0, 0)

    @pl.when(b + 1 < pl.num_programs(0))
    def _():
        start_gather(b + 1, 1 - slot)

    wait_gather(slot)

    qi = c_ref[0, :, ATT_QI0:ATT_QI0 + IDX_HEADS * IDX_DIM] * IDX_DIM ** -0.5
    wi = _bf16_round(c_ref[0, :, ATT_WI0:ATT_WI0 + IDX_HEADS] * IDX_HEADS ** -0.5)
    qi_rows = jnp.concatenate([qi[:, h * IDX_DIM:(h + 1) * IDX_DIM] for h in range(IDX_HEADS)], axis=0)

    def scores(keys):
        d = _bf16_round(jnp.maximum(_mm_nt(qi_rows, keys), 0.0))
        s = jnp.zeros((tq, keys.shape[0]), F32)
        for h in range(IDX_HEADS):
            s = s + d[h * tq:(h + 1) * tq] * wi[:, h:h + 1]
        return s

    causal_new = lax.broadcasted_iota(I32, (tq, PAGE), 1) <= lax.broadcasted_iota(I32, (tq, PAGE), 0)
    key_ref[:, 0:n_past] = _ordered_key(scores(ki_buf[slot]))
    ki_new = jnp.concatenate([c_ref[0, :, ATT_KI0:ATT_KI0 + IDX_DIM], jnp.zeros((PAGE - tq, IDX_DIM), F32)], axis=0)
    key_ref[:, n_past:n_past + PAGE] = _ordered_key(jnp.where(causal_new, scores(ki_new), NEG_INF))
    keys = key_ref[...]
    count_ge = lambda cand: jnp.sum(jnp.where(keys >= cand, 1, 0), axis=1, keepdims=True)
    thr = _kth_largest(count_ge, (tq, 1), topk)
    excess = jnp.max(count_ge(thr) - topk)

    @pl.when(excess == 0)
    def _():
        madd_ref[...] = jnp.where(keys >= thr, 0.0, NEG_INF)

    @pl.when(excess > 0)
    def _():
        need = (topk - count_ge(thr + 1)).astype(F32)
        tri = jnp.where(lax.broadcasted_iota(I32, (PAGE, PAGE), 0) <= lax.broadcasted_iota(I32, (PAGE, PAGE), 1),
                        1.0, 0.0).astype(BF16)

        def blk(jb, taken):
            cols = pl.ds(pl.multiple_of(jb * PAGE, PAGE), PAGE)
            key = key_ref[:, cols]
            eq = key == thr
            prefix = jnp.dot(jnp.where(eq, 1.0, 0.0).astype(BF16), tri, preferred_element_type=F32)
            sel = jnp.logical_or(key > thr, jnp.logical_and(eq, taken + prefix <= need))
            madd_ref[:, cols] = jnp.where(sel, 0.0, NEG_INF)
            return taken + prefix[:, PAGE - 1:PAGE]

        lax.fori_loop(0, n_pages + 1, blk, jnp.zeros((tq, 1), F32))

    q = c_ref[0, :, ATT_Q0:ATT_Q0 + ATT_HEADS * HD] * HD ** -0.5
    z = jnp.zeros((tq, HD), F32)
    qs = jnp.concatenate([jnp.concatenate([q[:, h * HD:(h + 1) * HD], z] if h // rep == 0
                                          else [z, q[:, h * HD:(h + 1) * HD]], axis=1)
                          for h in range(ATT_HEADS)], axis=0)
    pad = jnp.zeros((PAGE - tq, LANES), F32)
    k_new = jnp.concatenate([c_ref[0, :, ATT_K0:ATT_K0 + LANES], pad], axis=0)
    v_new = jnp.concatenate([c_ref[0, :, ATT_V0:ATT_V0 + LANES], pad], axis=0)
    lg_past = _mm_nt(qs, k_buf[slot])
    far = bias_ref[2][:, 0:1]
    madd_ref[:, n_past:n_past + PAGE] = jnp.where(causal_new, madd_ref[:, n_past:n_past + PAGE], NEG_INF)
    madd = madd_ref[...]
    lg = jnp.concatenate([lg_past[:, :n_past - PAGE] + far, lg_past[:, n_past - PAGE:] + bias_ref[1],
                          _mm_nt(qs, k_new) + bias_ref[0]], axis=1) + jnp.concatenate([madd] * ATT_HEADS, axis=0)
    pr = jnp.exp(lg - jnp.max(lg, axis=-1, keepdims=True))
    out = (_mm(pr[:, :n_past], v_buf[slot]) + _mm(pr[:, n_past:], v_new)) / jnp.sum(pr, axis=-1, keepdims=True)
    o_ref[0] = jnp.concatenate(
        [out[h * tq:(h + 1) * tq, (h // rep) * HD:(h // rep + 1) * HD] for h in range(ATT_HEADS)], axis=1)


def _dsa_sample(c_att, t_valid, layer, page_table, kidx_pool, k_pool, v_pool, bias_s):
    b, tq, _ = c_att.shape
    n_pages = page_table.shape[1]
    pt = page_table.reshape(-1)
    n_past = n_pages * PAGE
    n_keys = n_past + PAGE
    hbm = pl.BlockSpec(memory_space=pl.ANY)
    return pl.pallas_call(
        functools.partial(_dsa_sample_kernel, n_pages=n_pages, tq=tq, topk=min(TOPK, (n_past + t_valid) // 4)),
        grid_spec=pltpu.PrefetchScalarGridSpec(
            num_scalar_prefetch=2,
            grid=(b,),
            in_specs=[hbm, hbm, hbm, pl.BlockSpec((1, tq, ATT_W), lambda bi, pt_ref, l_ref: (bi, 0, 0)),
                      pl.BlockSpec(bias_s.shape, lambda bi, pt_ref, l_ref: (0, 0, 0))],
            out_specs=pl.BlockSpec((1, tq, ATT_HEADS * ATT_HEAD_DIM), lambda bi, pt_ref, l_ref: (bi, 0, 0)),
            scratch_shapes=[pltpu.VMEM((2, n_past, IDX_DIM), F32), pltpu.VMEM((2, n_past, LANES), F32),
                            pltpu.VMEM((2, n_past, LANES), F32), pltpu.SemaphoreType.DMA((3, 2)),
                            pltpu.VMEM((tq, n_keys), I32), pltpu.VMEM((tq, n_keys), F32)],
        ),
        out_shape=jax.ShapeDtypeStruct((b, tq, ATT_HEADS * ATT_HEAD_DIM), F32),
        compiler_params=_cparams(("arbitrary",)),
        name="dsa_sample",
    )(pt, layer, kidx_pool, k_pool, v_pool, c_att, bias_s)


def _t5_bucket(dist):
    n = jnp.maximum(dist, 0)
    max_exact = NUM_BUCKETS // 2
    nf = jnp.maximum(n, 1).astype(F32)
    large = max_exact + (jnp.log(nf / max_exact) / math.log(MAX_DISTANCE / max_exact)
                         * (NUM_BUCKETS - max_exact)).astype(I32)
    large = jnp.minimum(large, NUM_BUCKETS - 1)
    return jnp.where(n < max_exact, n, large)


def _bias_tiles(rel_bias):
    r = jnp.arange(Q_BLOCK)
    d0 = r[:, None] - r[None, :]
    dist = jnp.stack([d0, d0 + Q_BLOCK, jnp.full_like(d0, 2 * Q_BLOCK)])
    return jnp.transpose(rel_bias[_t5_bucket(dist)].astype(F32), (0, 3, 1, 2))


def _group_layer(x, w, lb, mem, mem_blocks, shift, s_rwkv0, s_hgrn0, attend, cfg):
    b, t, d = x.shape
    n = b * t
    tm, t_valid = cfg["tm"], cfg["t_valid"]
    x2 = x.reshape(n, d)
    x2 = _ffn(x2, w["ffn1_norm_pre"], w["ffn1_w_up"], w["ffn1_w_down"], w["ffn1_norm_post"], cfg["tm_ffn"])
    c_rw, c_att, c_hg, gates = _proj(x2, w["mix_norm_pre"], [w["w_in_rw"], w["w_in_att"], w["w_in_hg"], w["w_in_gate"]],
                                     True, tm, "mix_in")
    c_rw = c_rw.reshape(b, t, -1)
    c_att = c_att.reshape(b, t, -1)
    streams = _rwkv_prep(c_rw, shift, w, t_valid, cfg["tm_prep"])
    o_a, s_rwkv = _rwkv(streams, s_rwkv0, w["rwkv_ln_w"], w["rwkv_ln_b"], w["rwkv_r_k"], cfg["rwkv_chunk"],
                         cfg["rwkv_seqs"])
    o_b = attend(c_att)
    o_c, s_hgrn = _hgrn(c_hg.reshape(b, t, -1), lb, w["hgrn_norm_w"], s_hgrn0, cfg["hgrn_chunk"], t_valid, cfg["hgrn_tb"])
    x2 = _merge(x2, o_a.reshape(n, -1), o_b.reshape(n, -1), o_c.reshape(n, -1), gates,
                w["mix_w_proj_a"], w["mix_w_proj_b"], w["mix_w_proj_c"], w["mix_w_out"], w["mix_norm_post"], tm)
    (q,) = _proj(x2, w["cross_norm_pre"], [w["cross_wq"]], True, tm, "cross_q")
    o = _xattn(q.reshape(b, t, d), mem[0], mem[1], mem_blocks[0], mem_blocks[1], cfg["tq_x"])
    x2 = _out_proj(x2, o.reshape(n, d), w["cross_wo"], w["cross_norm_post"], tm)
    x2 = _ffn(x2, w["ffn2_norm_pre"], w["ffn2_w_up"], w["ffn2_w_down"], w["ffn2_norm_post"], cfg["tm_ffn"])
    nkv = ATT_KV_HEADS * ATT_HEAD_DIM
    state = (c_rw[:, t_valid - 1], s_rwkv, s_hgrn,
             c_att[:, :t_valid, ATT_K0:ATT_K0 + nkv].reshape(b, t_valid, ATT_KV_HEADS, ATT_HEAD_DIM),
             c_att[:, :t_valid, ATT_V0:ATT_V0 + nkv].reshape(b, t_valid, ATT_KV_HEADS, ATT_HEAD_DIM),
             c_att[:, :t_valid, ATT_KI0:ATT_KI0 + IDX_DIM])
    return x2.reshape(b, t, d), state


PROMPT_CFG = dict(tm=512, tm_ffn=512, tm_prep=256, rwkv_chunk=64, rwkv_seqs=4, hgrn_chunk=16, hgrn_tb=256, tq_x=512)
SAMPLE_PAD = 8
SAMPLE_CFG = dict(tm=256, tm_ffn=256, tm_prep=SAMPLE_PAD, rwkv_chunk=SAMPLE_PAD, rwkv_seqs=4, hgrn_chunk=SAMPLE_PAD,
                  hgrn_tb=SAMPLE_PAD, tq_x=SAMPLE_PAD)


def kernel(x_prompt, x_sample, cache_k, cache_v, cache_kidx, cache_mem_k, cache_mem_v, state_rwkv, state_rwkv_shift, state_hgrn, page_table, mem_prompt, rel_bias, hgrn_lb_logits, ffn1_norm_pre, ffn1_norm_post, ffn1_w_up, ffn1_w_down, mix_norm_pre, mix_norm_post, mix_w_in, rwkv_mu, rwkv_w0, rwkv_w_up, rwkv_a0, rwkv_a_up, rwkv_g_up, rwkv_k_k, rwkv_k_a, rwkv_r_k, rwkv_ln_w, rwkv_ln_b, hgrn_norm_w, mix_w_proj_a, mix_w_proj_b, mix_w_proj_c, mix_w_out, cross_norm_pre, cross_norm_post, cross_wq, cross_wk, cross_wv, cross_wo, ffn2_norm_pre, ffn2_norm_post, ffn2_w_up, ffn2_w_down):
    depth, d = ffn1_norm_pre.shape
    bp, tp, _ = x_prompt.shape
    bs, ts, _ = x_sample.shape
    d_r = rwkv_w0.shape[1]
    rw_cols = rwkv_mu.shape[1]
    d_h = hgrn_norm_w.shape[1]
    n_att = ATT_WI0 + IDX_HEADS - ATT_Q0
    bf = lambda a: a.astype(BF16)
    vec = lambda a: a[:, None, :]

    o1 = rw_cols + n_att
    w_att = jnp.pad(mix_w_in[:, :, rw_cols:o1], ((0, 0), (0, 0), (0, ATT_W - n_att)))
    head_sum = (np.arange(d_r)[:, None] // RWKV_HEAD == np.arange(d_r)[None, :] // RWKV_HEAD).astype(np.float32)
    per_head = lambda a: a.reshape(depth, RWKV_HEADS, 1, RWKV_HEAD)
    weights = dict(
        ffn1_norm_pre=vec(ffn1_norm_pre), ffn1_norm_post=vec(ffn1_norm_post), ffn1_w_up=bf(ffn1_w_up), ffn1_w_down=bf(ffn1_w_down),
        ffn2_norm_pre=vec(ffn2_norm_pre), ffn2_norm_post=vec(ffn2_norm_post), ffn2_w_up=bf(ffn2_w_up), ffn2_w_down=bf(ffn2_w_down),
        mix_norm_pre=vec(mix_norm_pre), mix_norm_post=vec(mix_norm_post),
        w_in_rw=bf(mix_w_in[:, :, :rw_cols]), w_in_att=bf(w_att),
        w_in_hg=bf(mix_w_in[:, :, o1:o1 + 4 * d_h]), w_in_gate=bf(mix_w_in[:, :, o1 + 4 * d_h:]),
        rwkv_mu=vec(rwkv_mu), rwkv_w0=vec(rwkv_w0), rwkv_w_up=bf(rwkv_w_up), rwkv_a0=vec(rwkv_a0), rwkv_a_up=bf(rwkv_a_up),
        rwkv_g_up=bf(rwkv_g_up), rwkv_k_k=vec(rwkv_k_k), rwkv_k_a=vec(rwkv_k_a),
        rwkv_r_k=rwkv_r_k[:, :, None, :], rwkv_ln_w=per_head(rwkv_ln_w), rwkv_ln_b=per_head(rwkv_ln_b),
        hgrn_norm_w=vec(hgrn_norm_w),
        mix_w_proj_a=bf(mix_w_proj_a), mix_w_proj_b=bf(mix_w_proj_b), mix_w_proj_c=bf(mix_w_proj_c), mix_w_out=bf(mix_w_out),
        cross_norm_pre=vec(cross_norm_pre), cross_norm_post=vec(cross_norm_post), cross_wq=bf(cross_wq), cross_wo=bf(cross_wo),
        cross_wkv=bf(jnp.concatenate([cross_wk, cross_wv], axis=-1)),
    )
    p_lb = jax.nn.softmax(hgrn_lb_logits.astype(F32), axis=0)
    lower_bounds = vec(jnp.cumsum(p_lb, axis=0) - p_lb[0:1])

    bias = _bias_tiles(rel_bias)
    bias_t = jnp.transpose(bias, (0, 3, 1, 2)).reshape(3, Q_BLOCK, ATT_HEADS * Q_BLOCK)
    bias_s = bias[:, :, :SAMPLE_PAD, :].reshape(3, ATT_HEADS * SAMPLE_PAD, Q_BLOCK)

    pad_t = SAMPLE_PAD - ts
    xs0 = jnp.pad(x_sample, ((0, 0), (0, pad_t), (0, 0)))
    n_pool = cache_k.shape[1]
    k_pool = cache_k.reshape(depth, n_pool, PAGE, ATT_KV_HEADS * ATT_HEAD_DIM)
    v_pool = cache_v.reshape(depth, n_pool, PAGE, ATT_KV_HEADS * ATT_HEAD_DIM)
    mem_tokens = mem_prompt.shape[1]
    mem2 = mem_prompt.reshape(bp * mem_tokens, d)
    ones_d = jnp.ones((1, d), F32)
    zero_shift = jnp.zeros((bp, 1, rw_cols), F32)
    zero_rwkv = jnp.zeros((bp, RWKV_HEADS, RWKV_HEAD, RWKV_HEAD), F32)
    zero_hgrn = jnp.zeros((bp, HGRN_HEADS, HGRN_EXPAND, HGRN_EXPAND), F32)
    prompt_cfg = dict(PROMPT_CFG, t_valid=tp)
    sample_cfg = dict(SAMPLE_CFG, t_valid=ts)

    def layer(carry, per_layer):
        xp, xs = carry
        w, lb, mem_ks, mem_vs, s_rw, s_sh, s_hg, li = per_layer
        w = dict(w, head_sum=jnp.asarray(head_sum, BF16))
        (mem_kv,) = _proj(mem2, ones_d, [w["cross_wkv"]], False, 256, "mem_kv")
        mem_kv = mem_kv.reshape(bp, mem_tokens, 2 * d)
        xp, st_p = _group_layer(xp, w, lb, (mem_kv, mem_kv), (0, 1), zero_shift, zero_rwkv, zero_hgrn,
                                lambda c: _dsa_prompt(c, bias_t), prompt_cfg)
        attend_s = lambda c: _dsa_sample(c, ts, li, page_table, cache_kidx, k_pool, v_pool, bias_s)
        xs, st_s = _group_layer(xs, w, lb, (mem_ks, mem_vs), (0, 0), s_sh, s_rw, s_hg, attend_s, sample_cfg)
        mem_k = mem_kv[:, :, :d].reshape(bp, mem_tokens, MEM_HEADS, d // MEM_HEADS)
        mem_v = mem_kv[:, :, d:].reshape(bp, mem_tokens, MEM_HEADS, d // MEM_HEADS)
        return (xp, xs), (st_p, (mem_k, mem_v), st_s)

    per_layer = (weights, lower_bounds,
                 cache_mem_k.reshape(depth, bs, mem_tokens, d), cache_mem_v.reshape(depth, bs, mem_tokens, d),
                 state_rwkv, state_rwkv_shift[:, :, None, :], state_hgrn,
                 jnp.arange(depth, dtype=I32)[:, None])
    (xp, xs), (st_p, (mem_k, mem_v), st_s) = lax.scan(layer, (x_prompt, xs0), per_layer)
    sh_p, rw_p, hg_p, k_p, v_p, ki_p = st_p
    sh_s, rw_s, hg_s, k_s, v_s, ki_s = st_s
    return (xp, xs[:, :ts], k_p, v_p, ki_p, mem_k, mem_v, rw_p, sh_p, hg_p, k_s, v_s, ki_s, rw_s, sh_s, hg_s)
```

```python
import functools
import math

import jax
import jax.numpy as jnp
import numpy as np
from jax import lax
from jax.experimental import pallas as pl
from jax.experimental.pallas import tpu as pltpu

F32 = jnp.float32
BF16 = jnp.bfloat16
I32 = jnp.int32

LANES = 128
SUBLANES = 8
VMEM_LIMIT_BYTES = 56 * 1024 * 1024

NORM_EPS = 1e-6
RWKV_LN_EPS = 64e-5
LB_FLOOR = 1e-30
NEG_INF = -1e30
INT_MIN = int(np.iinfo(np.int32).min)
LOG2E = math.log2(math.e)

RWKV_HEAD = 64
RWKV_HEADS = 8
RWKV_GROUP = 4
HGRN_HEADS = 4
HGRN_EXPAND = 128
ATT_HEADS = 8
ATT_KV_HEADS = 2
ATT_HEAD_DIM = 64
IDX_HEADS = 4
IDX_DIM = 64
TOPK = 256
Q_BLOCK = 128
MEM_HEADS = 4
NUM_BUCKETS = 32
MAX_DISTANCE = 128
PAGE = 128

ATT_Q0, ATT_K0, ATT_V0, ATT_QI0, ATT_KI0, ATT_WI0, ATT_W = 0, 512, 640, 768, 1024, 1088, 1152


def _cparams(sem):
    return pltpu.CompilerParams(dimension_semantics=sem, vmem_limit_bytes=VMEM_LIMIT_BYTES)


def _mm(a, b):
    return jnp.dot(a.astype(BF16), b.astype(BF16), preferred_element_type=F32)


def _mm_nt(a, b):
    return lax.dot_general(a.astype(BF16), b.astype(BF16), (((1,), (1,)), ((), ())),
                           preferred_element_type=F32)


def _mm_tn(a, b):
    return lax.dot_general(a.astype(BF16), b.astype(BF16), (((0,), (0,)), ((), ())),
                           preferred_element_type=F32)


def _bf16_round(x):
    return x.astype(BF16).astype(F32)


def _split3(x):
    hi = x.astype(BF16)
    r1 = x - hi.astype(F32)
    mid = r1.astype(BF16)
    lo = (r1 - mid.astype(F32)).astype(BF16)
    return hi, mid, lo


def _mm_exact_lhs(a01, x):
    a = a01.astype(BF16)
    hi, mid, lo = _split3(x)
    d = lambda p: jnp.dot(a, p, preferred_element_type=F32)
    return d(hi) + d(mid) + d(lo)


def _mm_exact_rhs(x, b01):
    b = b01.astype(BF16)
    hi, mid, lo = _split3(x)
    d = lambda p: jnp.dot(p, b, preferred_element_type=F32)
    return d(hi) + d(mid) + d(lo)


def _rms(x, g):
    return x * lax.rsqrt(jnp.mean(x * x, axis=-1, keepdims=True) + NORM_EPS) * g


def _sigmoid(x):
    return jax.nn.sigmoid(x)


def _softplus(x):
    return jnp.maximum(x, 0.0) + jnp.log1p(jnp.exp(-jnp.abs(x)))


def _full(shape):
    nd = len(shape)
    return pl.BlockSpec(shape, lambda *_: (0,) * nd)


def _row_tile(n, want):
    t = min(n, want)
    assert n % t == 0, (n, t)
    return t


def _ffn_kernel(x_ref, gpre_ref, wup_ref, wdn_ref, gpost_ref, o_ref, *, d_ff, tf):
    x = x_ref[...]
    h = _rms(x, gpre_ref[...]).astype(BF16)
    acc = jnp.zeros(x.shape, F32)
    for f0 in range(0, d_ff, tf):
        gate = jnp.dot(h, wup_ref[:, f0:f0 + tf], preferred_element_type=F32)
        up = jnp.dot(h, wup_ref[:, d_ff + f0:d_ff + f0 + tf], preferred_element_type=F32)
        act = (gate * _sigmoid(gate) * up).astype(BF16)
        acc = acc + jnp.dot(act, wdn_ref[f0:f0 + tf, :], preferred_element_type=F32)
    o_ref[...] = x + 0.5 * _rms(acc, gpost_ref[...])


def _ffn(x, gpre, wup, wdn, gpost, tm):
    n, d = x.shape
    d_ff = wdn.shape[0]
    tf = d_ff // 2 if (d_ff // 2) % LANES == 0 else d_ff
    tm = _row_tile(n, tm)
    return pl.pallas_call(
        functools.partial(_ffn_kernel, d_ff=d_ff, tf=tf),
        grid=(n // tm,),
        in_specs=[pl.BlockSpec((tm, d), lambda i: (i, 0)), _full((1, d)), _full(wup.shape), _full(wdn.shape),
                  _full((1, d))],
        out_specs=pl.BlockSpec((tm, d), lambda i: (i, 0)),
        out_shape=jax.ShapeDtypeStruct((n, d), F32),
        compiler_params=_cparams(("parallel",)),
        name="ffn",
    )(x, gpre, wup, wdn, gpost)


def _proj_kernel(*refs, n_out, norm, tn):
    x_ref, g_ref = refs[0], refs[1]
    w_refs = refs[2:2 + n_out]
    o_refs = refs[2 + n_out:]
    x = x_ref[...]
    h = (_rms(x, g_ref[...]) if norm else x).astype(BF16)
    for w_ref, o_ref in zip(w_refs, o_refs):
        n = w_ref.shape[1]
        step = tn if n % tn == 0 else n
        for n0 in range(0, n, step):
            o_ref[:, n0:n0 + step] = jnp.dot(h, w_ref[:, n0:n0 + step], preferred_element_type=F32)


def _proj(x, g, ws, norm, tm, name):
    n, d = x.shape
    tm = _row_tile(n, tm)
    return pl.pallas_call(
        functools.partial(_proj_kernel, n_out=len(ws), norm=norm, tn=512),
        grid=(n // tm,),
        in_specs=[pl.BlockSpec((tm, d), lambda i: (i, 0)), _full((1, d))] + [_full(w.shape) for w in ws],
        out_specs=[pl.BlockSpec((tm, w.shape[1]), lambda i: (i, 0)) for w in ws],
        out_shape=[jax.ShapeDtypeStruct((n, w.shape[1]), F32) for w in ws],
        compiler_params=_cparams(("parallel",)),
        name=name,
    )(x, g, *ws)


def _out_kernel(x_ref, a_ref, w_ref, g_ref, o_ref):
    y = jnp.dot(a_ref[...].astype(BF16), w_ref[...], preferred_element_type=F32)
    o_ref[...] = x_ref[...] + _rms(y, g_ref[...])


def _out_proj(x, a, w, g, tm):
    n, d = x.shape
    tm = _row_tile(n, tm)
    return pl.pallas_call(
        _out_kernel,
        grid=(n // tm,),
        in_specs=[pl.BlockSpec((tm, d), lambda i: (i, 0)), pl.BlockSpec((tm, a.shape[1]), lambda i: (i, 0)),
                  _full(w.shape), _full((1, d))],
        out_specs=pl.BlockSpec((tm, d), lambda i: (i, 0)),
        out_shape=jax.ShapeDtypeStruct((n, d), F32),
        compiler_params=_cparams(("parallel",)),
        name="out_proj",
    )(x, a, w, g)


def _merge_kernel(x_ref, oa_ref, ob_ref, oc_ref, gt_ref, wa_ref, wb_ref, wc_ref, wo_ref, g_ref, o_ref, *, d):
    m = jnp.zeros((x_ref.shape[0], d), F32)
    for j, (o_r, w_r) in enumerate(((oa_ref, wa_ref), (ob_ref, wb_ref), (oc_ref, wc_ref))):
        p = jnp.dot(o_r[...].astype(BF16), w_r[...], preferred_element_type=F32)
        m = m + _sigmoid(gt_ref[:, j * d:(j + 1) * d]) * p
    y = jnp.dot(m.astype(BF16), wo_ref[...], preferred_element_type=F32)
    o_ref[...] = x_ref[...] + _rms(y, g_ref[...])


def _merge(x, oa, ob, oc, gates, wa, wb, wc, wo, g, tm):
    n, d = x.shape
    tm = _row_tile(n, tm)
    row = lambda w: pl.BlockSpec((tm, w), lambda i: (i, 0))
    return pl.pallas_call(
        functools.partial(_merge_kernel, d=d),
        grid=(n // tm,),
        in_specs=[row(d), row(oa.shape[1]), row(ob.shape[1]), row(oc.shape[1]), row(gates.shape[1]),
                  _full(wa.shape), _full(wb.shape), _full(wc.shape), _full(wo.shape), _full((1, d))],
        out_specs=row(d),
        out_shape=jax.ShapeDtypeStruct((n, d), F32),
        compiler_params=_cparams(("parallel",)),
        name="merge",
    )(x, oa, ob, oc, gates, wa, wb, wc, wo, g)


def _xattn_kernel(q_ref, mk_ref, mv_ref, o_ref, *, heads):
    hd = q_ref.shape[2] // heads
    for h in range(heads):
        sl = slice(h * hd, (h + 1) * hd)
        lg = _mm_nt(q_ref[0, :, sl], mk_ref[0, :, sl]) * hd ** -0.5
        lg = lg - jnp.max(lg, axis=-1, keepdims=True)
        p = jnp.exp(lg)
        p = p / jnp.sum(p, axis=-1, keepdims=True)
        o_ref[0, :, sl] = _mm(p, mv_ref[0, :, sl])


def _xattn(q, mem_k, mem_v, k_blk, v_blk, tq):
    b, t, d = q.shape
    s = mem_k.shape[1]
    tq = _row_tile(t, tq)
    mem_specs = [pl.BlockSpec((1, s, d), lambda i, j: (i, 0, k_blk)),
                 pl.BlockSpec((1, s, d), lambda i, j: (i, 0, v_blk))]
    return pl.pallas_call(
        functools.partial(_xattn_kernel, heads=MEM_HEADS),
        grid=(b, t // tq),
        in_specs=[pl.BlockSpec((1, tq, d), lambda i, j: (i, j, 0))] + mem_specs,
        out_specs=pl.BlockSpec((1, tq, d), lambda i, j: (i, j, 0)),
        out_shape=jax.ShapeDtypeStruct((b, t, d), F32),
        compiler_params=_cparams(("parallel", "parallel")),
        name="xattn",
    )(q, mem_k, mem_v)


def _rwkv_prep_kernel(c_ref, sh_ref, mu_ref, w0_ref, wup_ref, a0_ref, aup_ref, gup_ref, kk_ref, ka_ref, hs_ref,
                      r_o, lw_o, k_o, v_o, kk_o, b_o, g_o, carry_ref, *, t_valid, d_r):
    j = pl.program_id(1)
    tm = c_ref.shape[1]

    @pl.when(j == 0)
    def _():
        carry_ref[...] = sh_ref[0]

    c = c_ref[0]
    row = lax.broadcasted_iota(I32, c.shape, 0)
    prev = jnp.where(row == 0, carry_ref[...], pltpu.roll(c, 1, axis=0))
    carry_ref[...] = c[tm - 1:tm, :]
    csh = c + (prev - c) * mu_ref[...]
    r, k, v = csh[:, 0:d_r], csh[:, d_r:2 * d_r], csh[:, 2 * d_r:3 * d_r]
    o = 3 * d_r
    n_w, n_a, n_g = wup_ref.shape[0], aup_ref.shape[0], gup_ref.shape[0]
    xw, xa, xg = csh[:, o:o + n_w], csh[:, o + n_w:o + n_w + n_a], csh[:, o + n_w + n_a:o + n_w + n_a + n_g]
    w_log = -_softplus(-(w0_ref[...] + _mm(jnp.tanh(xw), wup_ref[...]))) - 0.5
    lw = -jnp.exp(w_log)
    a_lr = _sigmoid(a0_ref[...] + _mm(xa, aup_ref[...]))
    g = _mm(_sigmoid(xg), gup_ref[...])
    kk = k * kk_ref[...]
    ss = _mm_exact_rhs(kk * kk, hs_ref[...])
    kk = kk / jnp.maximum(jnp.sqrt(ss), 1e-12)
    k_mod = k * (1.0 + (a_lr - 1.0) * ka_ref[...])
    b = kk * a_lr
    if t_valid < tm:
        ok = (lax.broadcasted_iota(I32, (tm, d_r), 0) + j * tm) < t_valid
        z = lambda t: jnp.where(ok, t, 0.0)
        lw, k_mod, v, kk, b = z(lw), z(k_mod), z(v), z(kk), z(b)
    for h in range(RWKV_HEADS):
        sl = slice(h * RWKV_HEAD, (h + 1) * RWKV_HEAD)
        for o_ref, val in ((r_o, r), (lw_o, lw), (k_o, k_mod), (v_o, v), (kk_o, kk), (b_o, b), (g_o, g)):
            o_ref[0, h] = val[:, sl]


def _rwkv_prep(c_rw, shift, p, t_valid, tm):
    b, t, cols = c_rw.shape
    d_r = p["rwkv_w0"].shape[1]
    tm = _row_tile(t, tm)
    hm = jax.ShapeDtypeStruct((b, RWKV_HEADS, t, RWKV_HEAD), F32)
    hm_spec = pl.BlockSpec((1, RWKV_HEADS, tm, RWKV_HEAD), lambda i, j: (i, 0, j, 0))
    params = [p["rwkv_mu"], p["rwkv_w0"], p["rwkv_w_up"], p["rwkv_a0"], p["rwkv_a_up"], p["rwkv_g_up"],
              p["rwkv_k_k"], p["rwkv_k_a"], p["head_sum"]]
    return pl.pallas_call(
        functools.partial(_rwkv_prep_kernel, t_valid=t_valid, d_r=d_r),
        grid=(b, t // tm),
        in_specs=[pl.BlockSpec((1, tm, cols), lambda i, j: (i, j, 0)),
                  pl.BlockSpec((1, 1, cols), lambda i, j: (i, 0, 0))] + [_full(a.shape) for a in params],
        out_specs=[hm_spec] * 7,
        out_shape=[hm] * 7,
        scratch_shapes=[pltpu.VMEM((1, cols), F32)],
        compiler_params=_cparams(("parallel", "arbitrary")),
        name="rwkv_prep",
    )(c_rw, shift, *params)


def _rwkv_kernel(r_ref, lw_ref, k_ref, v_ref, kk_ref, b_ref, g_ref, s0_ref, lnw_ref, lnb_ref, rk_ref,
                 o_ref, so_ref, s_ref, *, chunk):
    t_idx = pl.program_id(1)
    nh, C, N = RWKV_GROUP, chunk, RWKV_HEAD
    R = nh * C
    n_seq, n_heads = r_ref.shape[0], r_ref.shape[1]

    @pl.when(t_idx == 0)
    def _():
        s_ref[...] = s0_ref[...]

    row = lax.broadcasted_iota(I32, (R, R), 0)
    col = lax.broadcasted_iota(I32, (R, R), 1)
    same = (row // C) == (col // C)
    incl = jnp.logical_and(same, col <= row)
    strict = jnp.logical_and(same, col < row)
    incl01 = jnp.where(incl, 1.0, 0.0)
    eye_r = jnp.where(row == col, 1.0, 0.0)
    levels = []
    m_blk = 1
    while m_blk < C:
        levels.append(jnp.logical_and(row // (2 * m_blk) == col // (2 * m_blk),
                                      jnp.logical_and((row // m_blk) % 2 == 1, (col // m_blk) % 2 == 0)))
        m_blk *= 2
    eye = jnp.where(lax.broadcasted_iota(I32, (N, N), 0) == lax.broadcasted_iota(I32, (N, N), 1), 1.0, 0.0)

    chains = [(bi, h0) for bi in range(n_seq) for h0 in range(0, n_heads, nh)]
    each = lambda f, *cols: [f(*vals) for vals in zip(*cols)]
    ld = lambda ref: [ref[bi, h0:h0 + nh].reshape(R, N) for bi, h0 in chains]
    r, lw, k, v, kk, b, g = (ld(x) for x in (r_ref, lw_ref, k_ref, v_ref, kk_ref, b_ref, g_ref))
    cum = each(lambda t: _mm_exact_lhs(incl01, t), lw)
    e_neg = each(lambda c: jnp.exp(-c), cum)
    at = each(lambda kk_, c, l: -kk_ * jnp.exp(c - l), kk, cum, lw)
    rt = each(lambda r_, c: r_ * jnp.exp(c), r, cum)
    bt = each(lambda b_, e: b_ * e, b, e_neg)
    kt = each(lambda k_, e: k_ * e, k, e_neg)
    gram = each(lambda a_, r_, b_, k_: _mm_nt(jnp.concatenate([a_, r_], axis=0), jnp.concatenate([b_, k_], axis=0)),
                at, rt, bt, kt)
    a_ab = each(lambda gm: jnp.where(strict, gm[:R, :R], 0.0), gram)
    a_ak = each(lambda gm: jnp.where(strict, gm[:R, R:], 0.0), gram)
    a_rb = each(lambda gm: jnp.where(incl, gm[R:, :R], 0.0), gram)
    a_rk = each(lambda gm: jnp.where(incl, gm[R:, R:], 0.0), gram)
    x = each(lambda a_, ak, v_: jnp.concatenate([a_, _mm(ak, v_)], axis=1), at, a_ak, v)
    d_inv = [eye_r for _ in chains]
    for li, lower_left in enumerate(levels):
        a_off = each(lambda ab: jnp.where(lower_left, ab, 0.0), a_ab)
        if li == 0:
            d_inv = each(lambda d, ao: d + ao, d_inv, a_off)
        else:
            half = each(lambda ao, d: _mm(ao, d), a_off, d_inv)
            d_inv = each(lambda d, hf: d + _mm(d, hf), d_inv, half)
    x = each(lambda d, x_: _mm(d, x_), d_inv, x)
    ax = each(lambda rb, x_: _mm(rb, x_), a_rb, x)
    qt = each(lambda r_, ax_: r_ + ax_[:, :N], rt, ax)
    y0 = each(lambda ax_, rk_, v_: ax_[:, N:] + _mm(rk_, v_), ax, a_rk, v)
    for ci, (bi, h0) in enumerate(chains):
        outs = []
        for h in range(nh):
            sl = slice(h * C, (h + 1) * C)
            p_c = jnp.exp(cum[ci][h * C + C - 1:h * C + C, :])
            m = (eye + _mm_tn(x[ci][sl, :N], bt[ci][sl])) * p_c
            s_loc = _mm_tn(jnp.concatenate([x[ci][sl, N:], v[ci][sl]], axis=0),
                           jnp.concatenate([bt[ci][sl], kt[ci][sl]], axis=0)) * p_c
            s = s_ref[bi, h0 + h]
            y = _mm_nt(qt[ci][sl], s) + y0[ci][sl]
            s_ref[bi, h0 + h] = _mm(s, m) + s_loc
            mu = jnp.mean(y, axis=-1, keepdims=True)
            yc = y - mu
            var = jnp.mean(yc * yc, axis=-1, keepdims=True)
            yn = yc * lax.rsqrt(var + RWKV_LN_EPS) * lnw_ref[h0 + h] + lnb_ref[h0 + h]
            yn = yn + jnp.sum(r[ci][sl] * k[ci][sl] * rk_ref[h0 + h], axis=-1, keepdims=True) * v[ci][sl]
            outs.append(yn * g[ci][sl])
        o_ref[bi, :, h0 * N:(h0 + nh) * N] = jnp.concatenate(outs, axis=1)

    @pl.when(t_idx == pl.num_programs(1) - 1)
    def _():
        so_ref[...] = s_ref[...]


def _rwkv(streams, s0, lnw, lnb, rk, chunk, n_seq):
    b, nh, t, n = streams[0].shape
    n_seq = _row_tile(b, n_seq)
    st_spec = pl.BlockSpec((n_seq, nh, chunk, n), lambda i, j: (i, 0, j, 0))
    s_spec = pl.BlockSpec((n_seq, nh, n, n), lambda i, j: (i, 0, 0, 0))
    p_spec = _full((nh, 1, n))
    return pl.pallas_call(
        functools.partial(_rwkv_kernel, chunk=chunk),
        grid=(b // n_seq, t // chunk),
        in_specs=[st_spec] * 7 + [s_spec, p_spec, p_spec, p_spec],
        out_specs=[pl.BlockSpec((n_seq, chunk, nh * n), lambda i, j: (i, j, 0)), s_spec],
        out_shape=[jax.ShapeDtypeStruct((b, t, nh * n), F32), jax.ShapeDtypeStruct((b, nh, n, n), F32)],
        scratch_shapes=[pltpu.VMEM((n_seq, nh, n, n), F32)],
        compiler_params=_cparams(("parallel", "arbitrary")),
        name="rwkv",
    )(*streams, s0, lnw, lnb, rk)


def _hgrn_kernel(c_ref, lb_ref, nw_ref, s0_ref, o_ref, so_ref, st_ref, cum_ref, kh_ref, *, chunk, t_valid):
    j = pl.program_id(1)
    tb = c_ref.shape[1]
    d = lb_ref.shape[1]
    K = HGRN_EXPAND
    c = chunk

    @pl.when(j == 0)
    def _():
        for h in range(HGRN_HEADS):
            st_ref[h] = s0_ref[0, h].T

    z = c_ref[0, :, d:2 * d]
    lb = lb_ref[...]
    ls = -_softplus(-z)
    x1 = jnp.log(jnp.maximum(lb, LB_FLOOR))
    x2 = jnp.log1p(-lb) + ls
    logf = jnp.maximum(x1, x2) + jnp.log1p(jnp.exp(-jnp.abs(x1 - x2)))
    kh = (1.0 - lb) * _sigmoid(-z)
    if t_valid < tb:
        ok = (lax.broadcasted_iota(I32, (tb, d), 0) + j * tb) < t_valid
        logf = jnp.where(ok, logf, 0.0)
        kh = jnp.where(ok, kh, 0.0)
    row = lax.broadcasted_iota(I32, (tb, tb), 0)
    col = lax.broadcasted_iota(I32, (tb, tb), 1)
    tri = jnp.where(jnp.logical_and(row // c == col // c, col <= row), 1.0, 0.0)
    cum_ref[...] = _mm_exact_lhs(tri, logf)
    kh_ref[...] = kh

    ones = jnp.ones((K, K), BF16)
    rr = lax.broadcasted_iota(I32, (c * c, K), 0)
    causal = (rr % c) <= (rr // c)
    sel = jnp.where(lax.broadcasted_iota(I32, (c, c * c), 1) // c == lax.broadcasted_iota(I32, (c, c * c), 0),
                    1.0, 0.0).astype(BF16)

    heads = range(HGRN_HEADS)
    each = lambda f, *cols: [f(*vals) for vals in zip(*cols)]

    def body(ci, carry):
        r0 = pl.multiple_of(ci * c, c)
        rows = pl.ds(r0, c)
        hq = [c_ref[0, rows, h * K:(h + 1) * K] for h in heads]
        q = each(lambda t: t * _sigmoid(t), hq)
        v = [c_ref[0, rows, 2 * d + h * K:2 * d + (h + 1) * K] for h in heads]
        hg = [c_ref[0, rows, 3 * d + h * K:3 * d + (h + 1) * K] for h in heads]
        bq = [cum_ref[rows, h * K:(h + 1) * K] for h in heads]
        kq = [kh_ref[rows, h * K:(h + 1) * K] for h in heads]
        e = each(lambda q_, k_, b_: jnp.concatenate(
            [q_[t:t + 1] * k_ * jnp.exp(jnp.minimum(b_[t:t + 1] - b_, 0.0)) for t in range(c)], axis=0),
            q, kq, bq)
        att = each(lambda e_: jnp.dot(e_.astype(BF16), ones, preferred_element_type=F32), e)
        w = each(lambda a_, v_: jnp.where(causal, a_, 0.0) * jnp.concatenate([v_] * c, axis=0), att, v)
        st = [st_ref[h] for h in heads]
        o = each(lambda w_, q_, b_, s_: jnp.dot(sel, w_.astype(BF16), preferred_element_type=F32)
                 + _mm_nt(q_ * jnp.exp(b_), s_), w, q, bq, st)
        upd = each(lambda s_, b_, v_, k_: s_ * jnp.exp(b_[c - 1:c]) + _mm_tn(v_, k_ * jnp.exp(b_[c - 1:c] - b_)),
                   st, bq, v, kq)
        for h in heads:
            st_ref[h] = upd[h]
            on = o[h] * lax.rsqrt(jnp.mean(o[h] * o[h], axis=-1, keepdims=True) + NORM_EPS)
            o_ref[0, rows, h * K:(h + 1) * K] = on * nw_ref[:, h * K:(h + 1) * K] * (hg[h] * _sigmoid(hg[h]))
        return carry

    lax.fori_loop(0, tb // c, body, 0, unroll=min(4, tb // c))

    @pl.when(j == pl.num_programs(1) - 1)
    def _():
        for h in range(HGRN_HEADS):
            so_ref[0, h] = st_ref[h].T


def _hgrn(c_hg, lb, nw, s0, chunk, t_valid, tb):
    b, t, cols = c_hg.shape
    d = cols // 4
    tb = _row_tile(t, tb)
    s_spec = pl.BlockSpec((1, HGRN_HEADS, HGRN_EXPAND, HGRN_EXPAND), lambda i, j: (i, 0, 0, 0))
    return pl.pallas_call(
        functools.partial(_hgrn_kernel, chunk=chunk, t_valid=t_valid),
        grid=(b, t // tb),
        in_specs=[pl.BlockSpec((1, tb, cols), lambda i, j: (i, j, 0)), _full((1, d)), _full((1, d)), s_spec],
        out_specs=[pl.BlockSpec((1, tb, d), lambda i, j: (i, j, 0)), s_spec],
        out_shape=[jax.ShapeDtypeStruct((b, t, d), F32), jax.ShapeDtypeStruct(s0.shape, F32)],
        scratch_shapes=[pltpu.VMEM((HGRN_HEADS, HGRN_EXPAND, HGRN_EXPAND), F32), pltpu.VMEM((tb, d), F32),
                        pltpu.VMEM((tb, d), F32)],
        compiler_params=_cparams(("parallel", "arbitrary")),
        name="hgrn",
    )(c_hg, lb, nw, s0)


def _ordered_key(s):
    bits = pltpu.bitcast(s + 0.0, I32)
    return bits ^ ((bits >> 31) & 0x7FFFFFFF)


def _kth_largest(count_ge, shape, k, two_bits=False):
    c0 = count_ge(jnp.zeros(shape, I32))
    thr = jnp.where(c0 >= k, 0, INT_MIN).astype(I32)

    def body(i, thr):
        cand = thr | jnp.left_shift(jnp.int32(1), 30 - i)
        return jnp.where(count_ge(cand) >= k, cand, thr)

    if not two_bits:
        return lax.fori_loop(0, 31, body, thr)

    def body2(i, thr):
        hi = jnp.left_shift(jnp.int32(1), 30 - 2 * i)
        lo = jnp.left_shift(jnp.int32(1), 29 - 2 * i)
        c_lo, c_hi, c_both = thr | lo, thr | hi, thr | hi | lo
        n_lo, n_hi, n_both = count_ge(c_lo), count_ge(c_hi), count_ge(c_both)
        return jnp.where(n_both >= k, c_both, jnp.where(n_hi >= k, c_hi, jnp.where(n_lo >= k, c_lo, thr)))

    return body(30, lax.fori_loop(0, 15, body2, thr))


DSA_GROUP = 4


def _dsa_prompt_kernel(q_ref, qi_ref, kw_ref, k_ref, v_ref, ki_ref, bias_ref, o_ref,
                       key_ref, m_ref, acc_ref, *, topk):
    i = pl.program_id(1)
    QB, HD = Q_BLOCK, ATT_HEAD_DIM
    zeros64 = jnp.zeros((HD, QB), F32)

    q_t = q_ref[0].T
    rep = ATT_HEADS // ATT_KV_HEADS
    tiles = []
    for h in range(ATT_HEADS):
        qh = q_t[h * HD:(h + 1) * HD]
        tiles.append(jnp.concatenate([qh, zeros64] if h // rep == 0 else [zeros64, qh], axis=0))
    qs_t = (jnp.concatenate(tiles, axis=1) * (HD ** -0.5 * LOG2E)).astype(BF16)
    qi_t = qi_ref[0].T * IDX_DIM ** -0.5
    qi_pad = jnp.concatenate([jnp.concatenate([qi_t[h * IDX_DIM:(h + 1) * IDX_DIM], zeros64], axis=0)
                              for h in range(IDX_HEADS)], axis=1).astype(BF16)
    w_t = kw_ref[0].T
    w_rows = [_bf16_round(w_t[IDX_DIM + h:IDX_DIM + h + 1] * IDX_HEADS ** -0.5) for h in range(IDX_HEADS)]

    row = lax.broadcasted_iota(I32, (QB, QB), 0)
    col = lax.broadcasted_iota(I32, (QB, QB), 1)
    n_grp = (i + DSA_GROUP) // DSA_GROUP

    def score_group(gi, carry):
        for u in range(DSA_GROUP):
            j = gi * DSA_GROUP + u
            r0 = pl.multiple_of(j * QB, QB)
            dots = jnp.dot(ki_ref[0, pl.ds(r0, QB), :].astype(BF16), qi_pad, preferred_element_type=F32)
            s = jnp.zeros((QB, QB), F32)
            for h in range(IDX_HEADS):
                s = s + _bf16_round(jnp.maximum(dots[:, h * QB:(h + 1) * QB], 0.0)) * w_rows[h]
            vis = (j * QB + row) <= (i * QB + col)
            key = _ordered_key(jnp.where(vis, s, NEG_INF))
            key_ref[pl.ds(r0, QB), :] = jnp.where(j <= i, key, INT_MIN)
        return carry

    lax.fori_loop(0, n_grp, score_group, 0)

    def count_ge(cand):
        rows = DSA_GROUP * QB

        def grp(gi, acc):
            r0 = pl.multiple_of(gi * rows, rows)
            hit = jnp.where(key_ref[pl.ds(r0, rows), :] >= cand, 1, 0)
            return acc + jnp.sum(hit.reshape(rows // SUBLANES, SUBLANES, QB), axis=0)
        acc = lax.fori_loop(0, n_grp, grp, jnp.zeros((SUBLANES, QB), I32))
        return jnp.sum(acc, axis=0, keepdims=True)

    thr = _kth_largest(count_ge, (1, QB), topk)
    need = (topk - count_ge(thr + 1)).astype(F32)

    m_ref[...] = jnp.full(m_ref.shape, NEG_INF, F32)
    acc_ref[...] = jnp.zeros(acc_ref.shape, F32)
    tri = jnp.where(col <= row, 1.0, 0.0).astype(BF16)

    def attn_pair(jp, taken):
        lgs = []
        for u in range(2):
            j = 2 * jp + u
            r0 = pl.multiple_of(j * QB, QB)
            key = key_ref[pl.ds(r0, QB), :]
            eq = key == thr
            prefix = jnp.dot(tri, jnp.where(eq, 1.0, 0.0).astype(BF16), preferred_element_type=F32)
            sel = jnp.logical_or(key > thr, jnp.logical_and(eq, taken + prefix <= need))
            taken = taken + prefix[QB - 1:QB, :]
            vis = (j * QB + row) <= (i * QB + col)
            madd = jnp.where(jnp.logical_and(sel, vis), 0.0, NEG_INF)
            lg = jnp.dot(k_ref[0, pl.ds(r0, QB), :].astype(BF16), qs_t, preferred_element_type=F32)
            lgs.append(lg + bias_ref[jnp.clip(i - j, 0, 2)] + jnp.concatenate([madd] * ATT_HEADS, axis=1))
        m_old = m_ref[...]
        m_new = jnp.maximum(m_old, jnp.maximum(jnp.max(lgs[0], axis=0, keepdims=True),
                                               jnp.max(lgs[1], axis=0, keepdims=True)))
        alpha = jnp.exp2(m_old - m_new)
        p = jnp.concatenate([jnp.exp2(lg - m_new).astype(BF16) for lg in lgs], axis=0)
        v_pair = v_ref[0, pl.ds(pl.multiple_of(jp * 2 * QB, 2 * QB), 2 * QB), :]
        v_aug = jnp.concatenate([v_pair.T.astype(BF16), jnp.ones((SUBLANES, 2 * QB), BF16)], axis=0)
        acc_ref[...] = alpha * acc_ref[...] + jnp.dot(v_aug, p, preferred_element_type=F32)
        m_ref[...] = m_new
        return taken

    lax.fori_loop(0, (i + 2) // 2, attn_pair, jnp.zeros((1, QB), F32))

    acc = acc_ref[...]
    out_t = (acc[:LANES] / acc[LANES:LANES + 1]).T
    o_ref[0] = jnp.concatenate(
        [out_t[h * QB:(h + 1) * QB, (h // rep) * HD:(h // rep + 1) * HD] for h in range(ATT_HEADS)], axis=1)


def _dsa_prompt(c_att, bias_t):
    b, t, _ = c_att.shape
    nq = t // Q_BLOCK
    assert nq % DSA_GROUP == 0, (t, Q_BLOCK, DSA_GROUP)
    t_keys =((nq + DSA_GROUP - 1) // DSA_GROUP) * DSA_GROUP * Q_BLOCK
    blk = lambda w, cb: pl.BlockSpec((1, Q_BLOCK, w), lambda bi, i: (bi, i, cb))
    allk = lambda cb: pl.BlockSpec((1, t, LANES), lambda bi, i: (bi, 0, cb))
    return pl.pallas_call(
        functools.partial(_dsa_prompt_kernel, topk=min(TOPK, t // 4)),
        grid=(b, nq),
        in_specs=[blk(512, ATT_Q0 // 512), blk(256, ATT_QI0 // 256), blk(LANES, ATT_KI0 // LANES),
                  allk(ATT_K0 // LANES), allk(ATT_V0 // LANES), allk(ATT_KI0 // LANES), _full(bias_t.shape)],
        out_specs=pl.BlockSpec((1, Q_BLOCK, ATT_HEADS * ATT_HEAD_DIM), lambda bi, i: (bi, i, 0)),
        out_shape=jax.ShapeDtypeStruct((b, t, ATT_HEADS * ATT_HEAD_DIM), F32),
        scratch_shapes=[pltpu.VMEM((t_keys, Q_BLOCK), I32), pltpu.VMEM((1, ATT_HEADS * Q_BLOCK), F32),
                        pltpu.VMEM((LANES + SUBLANES, ATT_HEADS * Q_BLOCK), F32)],
        compiler_params=_cparams(("parallel", "arbitrary")),
        name="dsa_prompt",
    )(c_att, c_att, c_att, c_att, c_att, c_att, bias_t)


def _dsa_sample_kernel(pt_ref, l_ref, kidx_hbm, k_hbm, v_hbm, c_ref, bias_ref, o_ref,
                       ki_buf, k_buf, v_buf, sems, key_ref, madd_ref, *, n_pages, tq, topk):
    b = pl.program_id(0)
    slot = b % 2
    layer = l_ref[0]
    HD = ATT_HEAD_DIM
    rep = ATT_HEADS // ATT_KV_HEADS
    n_past = n_pages * PAGE

    def page_copies(page, p, sl):
        rows = pl.ds(pl.multiple_of(p * PAGE, PAGE), PAGE)
        return (pltpu.make_async_copy(kidx_hbm.at[layer, page], ki_buf.at[sl, rows], sems.at[0, sl]),
                pltpu.make_async_copy(k_hbm.at[layer, page], k_buf.at[sl, rows], sems.at[1, sl]),
                pltpu.make_async_copy(v_hbm.at[layer, page], v_buf.at[sl, rows], sems.at[2, sl]))

    def start_gather(bi, sl):
        def body(p, carry):
            for cp in page_copies(pt_ref[bi * n_pages + p], p, sl):
                cp.start()
            return carry
        lax.fori_loop(0, n_pages, body, 0)

    def wait_gather(sl):
        def body(p, carry):
            for cp in page_copies(0, p, sl):
                cp.wait()
            return carry
        lax.fori_loop(0, n_pages, body, 0)

    @pl.when(b == 0)
    def _():
        start_gather(0, 0)

    @pl.when(b + 1 < pl.num_programs(0))
    def _():
        start_gather(b + 1, 1 - slot)

    wait_gather(slot)

    qi = c_ref[0, :, ATT_QI0:ATT_QI0 + IDX_HEADS * IDX_DIM] * IDX_DIM ** -0.5
    wi = _bf16_round(c_ref[0, :, ATT_WI0:ATT_WI0 + IDX_HEADS] * IDX_HEADS ** -0.5)
    qi_rows = jnp.concatenate([qi[:, h * IDX_DIM:(h + 1) * IDX_DIM] for h in range(IDX_HEADS)], axis=0)

    def scores(keys):
        d = _bf16_round(jnp.maximum(_mm_nt(qi_rows, keys), 0.0))
        s = jnp.zeros((tq, keys.shape[0]), F32)
        for h in range(IDX_HEADS):
            s = s + d[h * tq:(h + 1) * tq] * wi[:, h:h + 1]
        return s

    causal_new = lax.broadcasted_iota(I32, (tq, PAGE), 1) <= lax.broadcasted_iota(I32, (tq, PAGE), 0)
    key_ref[:, 0:n_past] = _ordered_key(scores(ki_buf[slot]))
    ki_new = jnp.concatenate([c_ref[0, :, ATT_KI0:ATT_KI0 + IDX_DIM], jnp.zeros((PAGE - tq, IDX_DIM), F32)], axis=0)
    key_ref[:, n_past:n_past + PAGE] = _ordered_key(jnp.where(causal_new, scores(ki_new), NEG_INF))
    keys = key_ref[...]
    count_ge = lambda cand: jnp.sum(jnp.where(keys >= cand, 1, 0), axis=1, keepdims=True)
    thr = _kth_largest(count_ge, (tq, 1), topk, two_bits=True)
    excess =jnp.max(count_ge(thr) - topk)

    @pl.when(excess == 0)
    def _():
        madd_ref[...] = jnp.where(keys >= thr, 0.0, NEG_INF)

    @pl.when(excess > 0)
    def _():
        need = (topk - count_ge(thr + 1)).astype(F32)
        tri = jnp.where(lax.broadcasted_iota(I32, (PAGE, PAGE), 0) <= lax.broadcasted_iota(I32, (PAGE, PAGE), 1),
                        1.0, 0.0).astype(BF16)

        def blk(jb, taken):
            cols = pl.ds(pl.multiple_of(jb * PAGE, PAGE), PAGE)
            key = key_ref[:, cols]
            eq = key == thr
            prefix = jnp.dot(jnp.where(eq, 1.0, 0.0).astype(BF16), tri, preferred_element_type=F32)
            sel = jnp.logical_or(key > thr, jnp.logical_and(eq, taken + prefix <= need))
            madd_ref[:, cols] = jnp.where(sel, 0.0, NEG_INF)
            return taken + prefix[:, PAGE - 1:PAGE]

        lax.fori_loop(0, n_pages + 1, blk, jnp.zeros((tq, 1), F32))

    q = c_ref[0, :, ATT_Q0:ATT_Q0 + ATT_HEADS * HD] * HD ** -0.5
    z = jnp.zeros((tq, HD), F32)
    qs = jnp.concatenate([jnp.concatenate([q[:, h * HD:(h + 1) * HD], z] if h // rep == 0
                                          else [z, q[:, h * HD:(h + 1) * HD]], axis=1)
                          for h in range(ATT_HEADS)], axis=0)
    pad = jnp.zeros((PAGE - tq, LANES), F32)
    k_new = jnp.concatenate([c_ref[0, :, ATT_K0:ATT_K0 + LANES], pad], axis=0)
    v_new = jnp.concatenate([c_ref[0, :, ATT_V0:ATT_V0 + LANES], pad], axis=0)
    lg_past = _mm_nt(qs, k_buf[slot])
    far = bias_ref[2][:, 0:1]
    madd_ref[:, n_past:n_past + PAGE] = jnp.where(causal_new, madd_ref[:, n_past:n_past + PAGE], NEG_INF)
    madd = madd_ref[...]
    lg = jnp.concatenate([lg_past[:, :n_past - PAGE] + far, lg_past[:, n_past - PAGE:] + bias_ref[1],
                          _mm_nt(qs, k_new) + bias_ref[0]], axis=1) + jnp.concatenate([madd] * ATT_HEADS, axis=0)
    pr = jnp.exp(lg - jnp.max(lg, axis=-1, keepdims=True))
    out = (_mm(pr[:, :n_past], v_buf[slot]) + _mm(pr[:, n_past:], v_new)) / jnp.sum(pr, axis=-1, keepdims=True)
    o_ref[0] = jnp.concatenate(
        [out[h * tq:(h + 1) * tq, (h // rep) * HD:(h // rep + 1) * HD] for h in range(ATT_HEADS)], axis=1)


def _dsa_sample(c_att, t_valid, layer, page_table, kidx_pool, k_pool, v_pool, bias_s):
    b, tq, _ = c_att.shape
    n_pages = page_table.shape[1]
    pt = page_table.reshape(-1)
    n_past = n_pages * PAGE
    n_keys = n_past + PAGE
    hbm = pl.BlockSpec(memory_space=pl.ANY)
    return pl.pallas_call(
        functools.partial(_dsa_sample_kernel, n_pages=n_pages, tq=tq, topk=min(TOPK, (n_past + t_valid) // 4)),
        grid_spec=pltpu.PrefetchScalarGridSpec(
            num_scalar_prefetch=2,
            grid=(b,),
            in_specs=[hbm, hbm, hbm, pl.BlockSpec((1, tq, ATT_W), lambda bi, pt_ref, l_ref: (bi, 0, 0)),
                      pl.BlockSpec(bias_s.shape, lambda bi, pt_ref, l_ref: (0, 0, 0))],
            out_specs=pl.BlockSpec((1, tq, ATT_HEADS * ATT_HEAD_DIM), lambda bi, pt_ref, l_ref: (bi, 0, 0)),
            scratch_shapes=[pltpu.VMEM((2, n_past, IDX_DIM), F32), pltpu.VMEM((2, n_past, LANES), F32),
                            pltpu.VMEM((2, n_past, LANES), F32), pltpu.SemaphoreType.DMA((3, 2)),
                            pltpu.VMEM((tq, n_keys), I32), pltpu.VMEM((tq, n_keys), F32)],
        ),
        out_shape=jax.ShapeDtypeStruct((b, tq, ATT_HEADS * ATT_HEAD_DIM), F32),
        compiler_params=_cparams(("arbitrary",)),
        name="dsa_sample",
    )(pt, layer, kidx_pool, k_pool, v_pool, c_att, bias_s)


def _t5_bucket(dist):
    n = jnp.maximum(dist, 0)
    max_exact = NUM_BUCKETS // 2
    nf = jnp.maximum(n, 1).astype(F32)
    large = max_exact + (jnp.log(nf / max_exact) / math.log(MAX_DISTANCE / max_exact)
                         * (NUM_BUCKETS - max_exact)).astype(I32)
    large = jnp.minimum(large, NUM_BUCKETS - 1)
    return jnp.where(n < max_exact, n, large)


def _bias_tiles(rel_bias):
    r = jnp.arange(Q_BLOCK)
    d0 = r[:, None] - r[None, :]
    dist = jnp.stack([d0, d0 + Q_BLOCK, jnp.full_like(d0, 2 * Q_BLOCK)])
    onehot = (_t5_bucket(dist)[..., None] == jnp.arange(NUM_BUCKETS)).astype(F32)
    return jnp.einsum("cqkn,nh->chqk", onehot, rel_bias.astype(F32), precision=lax.Precision.HIGHEST)


def _group_layer(x, w, lb, mem, mem_blocks, shift, s_rwkv0, s_hgrn0, attend, cfg):
    b, t, d = x.shape
    n = b * t
    tm, t_valid = cfg["tm"], cfg["t_valid"]
    x2 = x.reshape(n, d)
    x2 = _ffn(x2, w["ffn1_norm_pre"], w["ffn1_w_up"], w["ffn1_w_down"], w["ffn1_norm_post"], cfg["tm_ffn"])
    c_rw, c_att, c_hg, gates = _proj(x2, w["mix_norm_pre"], [w["w_in_rw"], w["w_in_att"], w["w_in_hg"], w["w_in_gate"]],
                                     True, tm, "mix_in")
    c_rw = c_rw.reshape(b, t, -1)
    c_att = c_att.reshape(b, t, -1)
    streams = _rwkv_prep(c_rw, shift, w, t_valid, cfg["tm_prep"])
    o_a, s_rwkv = _rwkv(streams, s_rwkv0, w["rwkv_ln_w"], w["rwkv_ln_b"], w["rwkv_r_k"], cfg["rwkv_chunk"],
                         cfg["rwkv_seqs"])
    o_b = attend(c_att)
    o_c, s_hgrn = _hgrn(c_hg.reshape(b, t, -1), lb, w["hgrn_norm_w"], s_hgrn0, cfg["hgrn_chunk"], t_valid, cfg["hgrn_tb"])
    x2 = _merge(x2, o_a.reshape(n, -1), o_b.reshape(n, -1), o_c.reshape(n, -1), gates,
                w["mix_w_proj_a"], w["mix_w_proj_b"], w["mix_w_proj_c"], w["mix_w_out"], w["mix_norm_post"], tm)
    (q,) = _proj(x2, w["cross_norm_pre"], [w["cross_wq"]], True, tm, "cross_q")
    o = _xattn(q.reshape(b, t, d), mem[0], mem[1], mem_blocks[0], mem_blocks[1], cfg["tq_x"])
    x2 = _out_proj(x2, o.reshape(n, d), w["cross_wo"], w["cross_norm_post"], tm)
    x2 = _ffn(x2, w["ffn2_norm_pre"], w["ffn2_w_up"], w["ffn2_w_down"], w["ffn2_norm_post"], cfg["tm_ffn"])
    nkv = ATT_KV_HEADS * ATT_HEAD_DIM
    state = (c_rw[:, t_valid - 1], s_rwkv, s_hgrn,
             c_att[:, :t_valid, ATT_K0:ATT_K0 + nkv].reshape(b, t_valid, ATT_KV_HEADS, ATT_HEAD_DIM),
             c_att[:, :t_valid, ATT_V0:ATT_V0 + nkv].reshape(b, t_valid, ATT_KV_HEADS, ATT_HEAD_DIM),
             c_att[:, :t_valid, ATT_KI0:ATT_KI0 + IDX_DIM])
    return x2.reshape(b, t, d), state


PROMPT_CFG = dict(tm=512, tm_ffn=512, tm_prep=256, rwkv_chunk=64, rwkv_seqs=4, hgrn_chunk=16, hgrn_tb=256, tq_x=512)
SAMPLE_PAD = 8
SAMPLE_CFG = dict(tm=256, tm_ffn=256, tm_prep=SAMPLE_PAD, rwkv_chunk=SAMPLE_PAD, rwkv_seqs=4, hgrn_chunk=SAMPLE_PAD,
                  hgrn_tb=SAMPLE_PAD, tq_x=SAMPLE_PAD)


def kernel(x_prompt, x_sample, cache_k, cache_v, cache_kidx, cache_mem_k, cache_mem_v, state_rwkv, state_rwkv_shift, state_hgrn, page_table, mem_prompt, rel_bias, hgrn_lb_logits, ffn1_norm_pre, ffn1_norm_post, ffn1_w_up, ffn1_w_down, mix_norm_pre, mix_norm_post, mix_w_in, rwkv_mu, rwkv_w0, rwkv_w_up, rwkv_a0, rwkv_a_up, rwkv_g_up, rwkv_k_k, rwkv_k_a, rwkv_r_k, rwkv_ln_w, rwkv_ln_b, hgrn_norm_w, mix_w_proj_a, mix_w_proj_b, mix_w_proj_c, mix_w_out, cross_norm_pre, cross_norm_post, cross_wq, cross_wk, cross_wv, cross_wo, ffn2_norm_pre, ffn2_norm_post, ffn2_w_up, ffn2_w_down):
    depth, d = ffn1_norm_pre.shape
    bp, tp, _ = x_prompt.shape
    bs, ts, _ = x_sample.shape
    d_r = rwkv_w0.shape[1]
    rw_cols = rwkv_mu.shape[1]
    d_h = hgrn_norm_w.shape[1]
    n_att = ATT_WI0 + IDX_HEADS - ATT_Q0
    bf = lambda a: a.astype(BF16)
    vec = lambda a: a[:, None, :]

    o1 = rw_cols + n_att
    w_att = jnp.pad(mix_w_in[:, :, rw_cols:o1], ((0, 0), (0, 0), (0, ATT_W - n_att)))
    head_sum = (np.arange(d_r)[:, None] // RWKV_HEAD == np.arange(d_r)[None, :] // RWKV_HEAD).astype(np.float32)
    per_head = lambda a: a.reshape(depth, RWKV_HEADS, 1, RWKV_HEAD)
    weights = dict(
        ffn1_norm_pre=vec(ffn1_norm_pre), ffn1_norm_post=vec(ffn1_norm_post), ffn1_w_up=bf(ffn1_w_up), ffn1_w_down=bf(ffn1_w_down),
        ffn2_norm_pre=vec(ffn2_norm_pre), ffn2_norm_post=vec(ffn2_norm_post), ffn2_w_up=bf(ffn2_w_up), ffn2_w_down=bf(ffn2_w_down),
        mix_norm_pre=vec(mix_norm_pre), mix_norm_post=vec(mix_norm_post),
        w_in_rw=bf(mix_w_in[:, :, :rw_cols]), w_in_att=bf(w_att),
        w_in_hg=bf(mix_w_in[:, :, o1:o1 + 4 * d_h]), w_in_gate=bf(mix_w_in[:, :, o1 + 4 * d_h:]),
        rwkv_mu=vec(rwkv_mu), rwkv_w0=vec(rwkv_w0), rwkv_w_up=bf(rwkv_w_up), rwkv_a0=vec(rwkv_a0), rwkv_a_up=bf(rwkv_a_up),
        rwkv_g_up=bf(rwkv_g_up), rwkv_k_k=vec(rwkv_k_k), rwkv_k_a=vec(rwkv_k_a),
        rwkv_r_k=rwkv_r_k[:, :, None, :], rwkv_ln_w=per_head(rwkv_ln_w), rwkv_ln_b=per_head(rwkv_ln_b),
        hgrn_norm_w=vec(hgrn_norm_w),
        mix_w_proj_a=bf(mix_w_proj_a), mix_w_proj_b=bf(mix_w_proj_b), mix_w_proj_c=bf(mix_w_proj_c), mix_w_out=bf(mix_w_out),
        cross_norm_pre=vec(cross_norm_pre), cross_norm_post=vec(cross_norm_post), cross_wq=bf(cross_wq), cross_wo=bf(cross_wo),
        cross_wkv=bf(jnp.concatenate([cross_wk, cross_wv], axis=-1)),
    )
    p_lb = jax.nn.softmax(hgrn_lb_logits.astype(F32), axis=0)
    lower_bounds = vec(jnp.cumsum(p_lb, axis=0) - p_lb[0:1])

    bias = _bias_tiles(rel_bias)
    bias_t = jnp.transpose(bias * LOG2E, (0, 3, 1, 2)).reshape(3, Q_BLOCK, ATT_HEADS * Q_BLOCK)
    bias_s = bias[:, :, :SAMPLE_PAD, :].reshape(3, ATT_HEADS * SAMPLE_PAD, Q_BLOCK)

    pad_t = SAMPLE_PAD - ts
    xs0 = jnp.pad(x_sample, ((0, 0), (0, pad_t), (0, 0)))
    n_pool = cache_k.shape[1]
    k_pool = cache_k.reshape(depth, n_pool, PAGE, ATT_KV_HEADS * ATT_HEAD_DIM)
    v_pool = cache_v.reshape(depth, n_pool, PAGE, ATT_KV_HEADS * ATT_HEAD_DIM)
    mem_tokens = mem_prompt.shape[1]
    mem2 = mem_prompt.reshape(bp * mem_tokens, d)
    ones_d = jnp.ones((1, d), F32)
    zero_shift = jnp.zeros((bp, 1, rw_cols), F32)
    zero_rwkv = jnp.zeros((bp, RWKV_HEADS, RWKV_HEAD, RWKV_HEAD), F32)
    zero_hgrn = jnp.zeros((bp, HGRN_HEADS, HGRN_EXPAND, HGRN_EXPAND), F32)
    prompt_cfg = dict(PROMPT_CFG, t_valid=tp)
    sample_cfg = dict(SAMPLE_CFG, t_valid=ts)

    def layer(carry, per_layer):
        xp, xs = carry
        w, lb, mem_ks, mem_vs, s_rw, s_sh, s_hg, li = per_layer
        w = dict(w, head_sum=jnp.asarray(head_sum, BF16))
        (mem_kv,) = _proj(mem2, ones_d, [w["cross_wkv"]], False, 256, "mem_kv")
        mem_kv = mem_kv.reshape(bp, mem_tokens, 2 * d)
        xp, st_p = _group_layer(xp, w, lb, (mem_kv, mem_kv), (0, 1), zero_shift, zero_rwkv, zero_hgrn,
                                lambda c: _dsa_prompt(c, bias_t), prompt_cfg)
        attend_s = lambda c: _dsa_sample(c, ts, li, page_table, cache_kidx, k_pool, v_pool, bias_s)
        xs, st_s = _group_layer(xs, w, lb, (mem_ks, mem_vs), (0, 0), s_sh, s_rw, s_hg, attend_s, sample_cfg)
        mem_k = mem_kv[:, :, :d].reshape(bp, mem_tokens, MEM_HEADS, d // MEM_HEADS)
        mem_v = mem_kv[:, :, d:].reshape(bp, mem_tokens, MEM_HEADS, d // MEM_HEADS)
        return (xp, xs), (st_p, (mem_k, mem_v), st_s)

    per_layer = (weights, lower_bounds,
                 cache_mem_k.reshape(depth, bs, mem_tokens, d), cache_mem_v.reshape(depth, bs, mem_tokens, d),
                 state_rwkv, state_rwkv_shift[:, :, None, :], state_hgrn,
                 jnp.arange(depth, dtype=I32)[:, None])
    (xp, xs), (st_p, (mem_k, mem_v), st_s) = lax.scan(layer, (x_prompt, xs0), per_layer)
    sh_p, rw_p, hg_p, k_p, v_p, ki_p = st_p
    sh_s, rw_s, hg_s, k_s, v_s, ki_s = st_s
    return (xp, xs[:, :ts], k_p, v_p, ki_p, mem_k, mem_v, rw_p, sh_p, hg_p, k_s, v_s, ki_s, rw_s, sh_s, hg_s)
```

```python
import functools
import math

import jax
import jax.numpy as jnp
import numpy as np
from jax import lax
from jax.experimental import pallas as pl
from jax.experimental.pallas import tpu as pltpu

F32 = jnp.float32
BF16 = jnp.bfloat16
I32 = jnp.int32

LANES = 128
SUBLANES = 8
VMEM_LIMIT_BYTES = 56 * 1024 * 1024

NORM_EPS = 1e-6
RWKV_LN_EPS = 64e-5
LB_FLOOR = 1e-30
NEG_INF = -1e30
INT_MIN = int(np.iinfo(np.int32).min)
LOG2E = math.log2(math.e)

RWKV_HEAD = 64
RWKV_HEADS = 8
RWKV_GROUP = 4
HGRN_HEADS = 4
HGRN_EXPAND = 128
ATT_HEADS = 8
ATT_KV_HEADS = 2
ATT_HEAD_DIM = 64
IDX_HEADS = 4
IDX_DIM = 64
TOPK = 256
Q_BLOCK = 128
MEM_HEADS = 4
NUM_BUCKETS = 32
MAX_DISTANCE = 128
PAGE = 128

ATT_Q0, ATT_K0, ATT_V0, ATT_QI0, ATT_KI0, ATT_WI0, ATT_W = 0, 512, 640, 768, 1024, 1088, 1152


def _cparams(sem):
    return pltpu.CompilerParams(dimension_semantics=sem, vmem_limit_bytes=VMEM_LIMIT_BYTES)


def _mm(a, b):
    return jnp.dot(a.astype(BF16), b.astype(BF16), preferred_element_type=F32)


def _mm_nt(a, b):
    return lax.dot_general(a.astype(BF16), b.astype(BF16), (((1,), (1,)), ((), ())),
                           preferred_element_type=F32)


def _mm_tn(a, b):
    return lax.dot_general(a.astype(BF16), b.astype(BF16), (((0,), (0,)), ((), ())),
                           preferred_element_type=F32)


def _bf16_round(x):
    return x.astype(BF16).astype(F32)


def _split3(x):
    hi = x.astype(BF16)
    r1 = x - hi.astype(F32)
    mid = r1.astype(BF16)
    lo = (r1 - mid.astype(F32)).astype(BF16)
    return hi, mid, lo


def _mm_exact_lhs(a01, x):
    a = a01.astype(BF16)
    hi, mid, lo = _split3(x)
    d = lambda p: jnp.dot(a, p, preferred_element_type=F32)
    return d(hi) + d(mid) + d(lo)


def _mm_exact_rhs(x, b01):
    b = b01.astype(BF16)
    hi, mid, lo = _split3(x)
    d = lambda p: jnp.dot(p, b, preferred_element_type=F32)
    return d(hi) + d(mid) + d(lo)


def _rms(x, g):
    return x * lax.rsqrt(jnp.mean(x * x, axis=-1, keepdims=True) + NORM_EPS) * g


def _sigmoid(x):
    return jax.nn.sigmoid(x)


def _softplus(x):
    return jnp.maximum(x, 0.0) + jnp.log1p(jnp.exp(-jnp.abs(x)))


def _full(shape):
    nd = len(shape)
    return pl.BlockSpec(shape, lambda *_: (0,) * nd)


def _row_tile(n, want):
    t = min(n, want)
    assert n % t == 0, (n, t)
    return t


def _ffn_kernel(x_ref, gpre_ref, wup_ref, wdn_ref, gpost_ref, o_ref, *, d_ff, tf):
    x = x_ref[...]
    h = _rms(x, gpre_ref[...]).astype(BF16)
    acc = jnp.zeros(x.shape, F32)
    for f0 in range(0, d_ff, tf):
        gate = jnp.dot(h, wup_ref[:, f0:f0 + tf], preferred_element_type=F32)
        up = jnp.dot(h, wup_ref[:, d_ff + f0:d_ff + f0 + tf], preferred_element_type=F32)
        act = (gate * _sigmoid(gate) * up).astype(BF16)
        acc = acc + jnp.dot(act, wdn_ref[f0:f0 + tf, :], preferred_element_type=F32)
    o_ref[...] = x + 0.5 * _rms(acc, gpost_ref[...])


def _ffn(x, gpre, wup, wdn, gpost, tm):
    n, d = x.shape
    d_ff = wdn.shape[0]
    tf = d_ff // 2 if (d_ff // 2) % LANES == 0 else d_ff
    tm = _row_tile(n, tm)
    return pl.pallas_call(
        functools.partial(_ffn_kernel, d_ff=d_ff, tf=tf),
        grid=(n // tm,),
        in_specs=[pl.BlockSpec((tm, d), lambda i: (i, 0)), _full((1, d)), _full(wup.shape), _full(wdn.shape),
                  _full((1, d))],
        out_specs=pl.BlockSpec((tm, d), lambda i: (i, 0)),
        out_shape=jax.ShapeDtypeStruct((n, d), F32),
        compiler_params=_cparams(("parallel",)),
        name="ffn",
    )(x, gpre, wup, wdn, gpost)


def _proj_kernel(*refs, n_out, norm, tn):
    x_ref, g_ref = refs[0], refs[1]
    w_refs = refs[2:2 + n_out]
    o_refs = refs[2 + n_out:]
    x = x_ref[...]
    h = (_rms(x, g_ref[...]) if norm else x).astype(BF16)
    for w_ref, o_ref in zip(w_refs, o_refs):
        n = w_ref.shape[1]
        step = tn if n % tn == 0 else n
        for n0 in range(0, n, step):
            o_ref[:, n0:n0 + step] = jnp.dot(h, w_ref[:, n0:n0 + step], preferred_element_type=F32)


def _proj(x, g, ws, norm, tm, name):
    n, d = x.shape
    tm = _row_tile(n, tm)
    return pl.pallas_call(
        functools.partial(_proj_kernel, n_out=len(ws), norm=norm, tn=512),
        grid=(n // tm,),
        in_specs=[pl.BlockSpec((tm, d), lambda i: (i, 0)), _full((1, d))] + [_full(w.shape) for w in ws],
        out_specs=[pl.BlockSpec((tm, w.shape[1]), lambda i: (i, 0)) for w in ws],
        out_shape=[jax.ShapeDtypeStruct((n, w.shape[1]), F32) for w in ws],
        compiler_params=_cparams(("parallel",)),
        name=name,
    )(x, g, *ws)


def _out_kernel(x_ref, a_ref, w_ref, g_ref, o_ref):
    y = jnp.dot(a_ref[...].astype(BF16), w_ref[...], preferred_element_type=F32)
    o_ref[...] = x_ref[...] + _rms(y, g_ref[...])


def _out_proj(x, a, w, g, tm):
    n, d = x.shape
    tm = _row_tile(n, tm)
    return pl.pallas_call(
        _out_kernel,
        grid=(n // tm,),
        in_specs=[pl.BlockSpec((tm, d), lambda i: (i, 0)), pl.BlockSpec((tm, a.shape[1]), lambda i: (i, 0)),
                  _full(w.shape), _full((1, d))],
        out_specs=pl.BlockSpec((tm, d), lambda i: (i, 0)),
        out_shape=jax.ShapeDtypeStruct((n, d), F32),
        compiler_params=_cparams(("parallel",)),
        name="out_proj",
    )(x, a, w, g)


def _merge_kernel(x_ref, oa_ref, ob_ref, oc_ref, gt_ref, wa_ref, wb_ref, wc_ref, wo_ref, g_ref, o_ref, *, d):
    m = jnp.zeros((x_ref.shape[0], d), F32)
    for j, (o_r, w_r) in enumerate(((oa_ref, wa_ref), (ob_ref, wb_ref), (oc_ref, wc_ref))):
        p = jnp.dot(o_r[...].astype(BF16), w_r[...], preferred_element_type=F32)
        m = m + _sigmoid(gt_ref[:, j * d:(j + 1) * d]) * p
    y = jnp.dot(m.astype(BF16), wo_ref[...], preferred_element_type=F32)
    o_ref[...] = x_ref[...] + _rms(y, g_ref[...])


def _merge(x, oa, ob, oc, gates, wa, wb, wc, wo, g, tm):
    n, d = x.shape
    tm = _row_tile(n, tm)
    row = lambda w: pl.BlockSpec((tm, w), lambda i: (i, 0))
    return pl.pallas_call(
        functools.partial(_merge_kernel, d=d),
        grid=(n // tm,),
        in_specs=[row(d), row(oa.shape[1]), row(ob.shape[1]), row(oc.shape[1]), row(gates.shape[1]),
                  _full(wa.shape), _full(wb.shape), _full(wc.shape), _full(wo.shape), _full((1, d))],
        out_specs=row(d),
        out_shape=jax.ShapeDtypeStruct((n, d), F32),
        compiler_params=_cparams(("parallel",)),
        name="merge",
    )(x, oa, ob, oc, gates, wa, wb, wc, wo, g)


def _xattn_kernel(q_ref, mk_ref, mv_ref, o_ref, *, heads):
    hd = q_ref.shape[2] // heads
    for h in range(heads):
        sl = slice(h * hd, (h + 1) * hd)
        lg = _mm_nt(q_ref[0, :, sl], mk_ref[0, :, sl]) * hd ** -0.5
        lg = lg - jnp.max(lg, axis=-1, keepdims=True)
        p = jnp.exp(lg)
        p = p / jnp.sum(p, axis=-1, keepdims=True)
        o_ref[0, :, sl] = _mm(p, mv_ref[0, :, sl])


def _xattn(q, mem_k, mem_v, k_blk, v_blk, tq):
    b, t, d = q.shape
    s = mem_k.shape[1]
    tq = _row_tile(t, tq)
    mem_specs = [pl.BlockSpec((1, s, d), lambda i, j: (i, 0, k_blk)),
                 pl.BlockSpec((1, s, d), lambda i, j: (i, 0, v_blk))]
    return pl.pallas_call(
        functools.partial(_xattn_kernel, heads=MEM_HEADS),
        grid=(b, t // tq),
        in_specs=[pl.BlockSpec((1, tq, d), lambda i, j: (i, j, 0))] + mem_specs,
        out_specs=pl.BlockSpec((1, tq, d), lambda i, j: (i, j, 0)),
        out_shape=jax.ShapeDtypeStruct((b, t, d), F32),
        compiler_params=_cparams(("parallel", "parallel")),
        name="xattn",
    )(q, mem_k, mem_v)


def _rwkv_prep_kernel(c_ref, sh_ref, mu_ref, w0_ref, wup_ref, a0_ref, aup_ref, gup_ref, kk_ref, ka_ref, hs_ref,
                      r_o, lw_o, k_o, v_o, kk_o, b_o, g_o, carry_ref, *, t_valid, d_r):
    j = pl.program_id(1)
    tm = c_ref.shape[1]

    @pl.when(j == 0)
    def _():
        carry_ref[...] = sh_ref[0]

    c = c_ref[0]
    row = lax.broadcasted_iota(I32, c.shape, 0)
    prev = jnp.where(row == 0, carry_ref[...], pltpu.roll(c, 1, axis=0))
    carry_ref[...] = c[tm - 1:tm, :]
    csh = c + (prev - c) * mu_ref[...]
    r, k, v = csh[:, 0:d_r], csh[:, d_r:2 * d_r], csh[:, 2 * d_r:3 * d_r]
    o = 3 * d_r
    n_w, n_a, n_g = wup_ref.shape[0], aup_ref.shape[0], gup_ref.shape[0]
    xw, xa, xg = csh[:, o:o + n_w], csh[:, o + n_w:o + n_w + n_a], csh[:, o + n_w + n_a:o + n_w + n_a + n_g]
    w_log = -_softplus(-(w0_ref[...] + _mm(jnp.tanh(xw), wup_ref[...]))) - 0.5
    lw = -jnp.exp(w_log)
    a_lr = _sigmoid(a0_ref[...] + _mm(xa, aup_ref[...]))
    g = _mm(_sigmoid(xg), gup_ref[...])
    kk = k * kk_ref[...]
    ss = _mm_exact_rhs(kk * kk, hs_ref[...])
    kk = kk / jnp.maximum(jnp.sqrt(ss), 1e-12)
    k_mod = k * (1.0 + (a_lr - 1.0) * ka_ref[...])
    b = kk * a_lr
    if t_valid < tm:
        ok = (lax.broadcasted_iota(I32, (tm, d_r), 0) + j * tm) < t_valid
        z = lambda t: jnp.where(ok, t, 0.0)
        lw, k_mod, v, kk, b = z(lw), z(k_mod), z(v), z(kk), z(b)
    for h in range(RWKV_HEADS):
        sl = slice(h * RWKV_HEAD, (h + 1) * RWKV_HEAD)
        for o_ref, val in ((r_o, r), (lw_o, lw), (k_o, k_mod), (v_o, v), (kk_o, kk), (b_o, b), (g_o, g)):
            o_ref[0, h] = val[:, sl]


def _rwkv_prep(c_rw, shift, p, t_valid, tm):
    b, t, cols = c_rw.shape
    d_r = p["rwkv_w0"].shape[1]
    tm = _row_tile(t, tm)
    hm = jax.ShapeDtypeStruct((b, RWKV_HEADS, t, RWKV_HEAD), F32)
    hm_spec = pl.BlockSpec((1, RWKV_HEADS, tm, RWKV_HEAD), lambda i, j: (i, 0, j, 0))
    params = [p["rwkv_mu"], p["rwkv_w0"], p["rwkv_w_up"], p["rwkv_a0"], p["rwkv_a_up"], p["rwkv_g_up"],
              p["rwkv_k_k"], p["rwkv_k_a"], p["head_sum"]]
    return pl.pallas_call(
        functools.partial(_rwkv_prep_kernel, t_valid=t_valid, d_r=d_r),
        grid=(b, t // tm),
        in_specs=[pl.BlockSpec((1, tm, cols), lambda i, j: (i, j, 0)),
                  pl.BlockSpec((1, 1, cols), lambda i, j: (i, 0, 0))] + [_full(a.shape) for a in params],
        out_specs=[hm_spec] * 7,
        out_shape=[hm] * 7,
        scratch_shapes=[pltpu.VMEM((1, cols), F32)],
        compiler_params=_cparams(("parallel", "arbitrary")),
        name="rwkv_prep",
    )(c_rw, shift, *params)


def _rwkv_kernel(r_ref, lw_ref, k_ref, v_ref, kk_ref, b_ref, g_ref, s0_ref, lnw_ref, lnb_ref, rk_ref,
                 o_ref, so_ref, s_ref, *, chunk):
    t_idx = pl.program_id(1)
    nh, C, N = RWKV_GROUP, chunk, RWKV_HEAD
    R = nh * C
    n_seq, n_heads = r_ref.shape[0], r_ref.shape[1]

    @pl.when(t_idx == 0)
    def _():
        s_ref[...] = s0_ref[...]

    row = lax.broadcasted_iota(I32, (R, R), 0)
    col = lax.broadcasted_iota(I32, (R, R), 1)
    same = (row // C) == (col // C)
    incl = jnp.logical_and(same, col <= row)
    strict = jnp.logical_and(same, col < row)
    incl01 = jnp.where(incl, 1.0, 0.0)
    eye_r = jnp.where(row == col, 1.0, 0.0)
    levels = []
    m_blk = 1
    while m_blk < C:
        levels.append(jnp.logical_and(row // (2 * m_blk) == col // (2 * m_blk),
                                      jnp.logical_and((row // m_blk) % 2 == 1, (col // m_blk) % 2 == 0)))
        m_blk *= 2
    eye = jnp.where(lax.broadcasted_iota(I32, (N, N), 0) == lax.broadcasted_iota(I32, (N, N), 1), 1.0, 0.0)

    chains = [(bi, h0) for bi in range(n_seq) for h0 in range(0, n_heads, nh)]
    each = lambda f, *cols: [f(*vals) for vals in zip(*cols)]
    ld = lambda ref: [ref[bi, h0:h0 + nh].reshape(R, N) for bi, h0 in chains]
    r, lw, k, v, kk, b, g = (ld(x) for x in (r_ref, lw_ref, k_ref, v_ref, kk_ref, b_ref, g_ref))
    cum = each(lambda t: _mm_exact_lhs(incl01, t), lw)
    e_neg = each(lambda c: jnp.exp(-c), cum)
    at = each(lambda kk_, c, l: -kk_ * jnp.exp(c - l), kk, cum, lw)
    rt = each(lambda r_, c: r_ * jnp.exp(c), r, cum)
    bt = each(lambda b_, e: b_ * e, b, e_neg)
    kt = each(lambda k_, e: k_ * e, k, e_neg)
    gram = each(lambda a_, r_, b_, k_: _mm_nt(jnp.concatenate([a_, r_], axis=0), jnp.concatenate([b_, k_], axis=0)),
                at, rt, bt, kt)
    a_ab = each(lambda gm: jnp.where(strict, gm[:R, :R], 0.0), gram)
    a_ak = each(lambda gm: jnp.where(strict, gm[:R, R:], 0.0), gram)
    a_rb = each(lambda gm: jnp.where(incl, gm[R:, :R], 0.0), gram)
    a_rk = each(lambda gm: jnp.where(incl, gm[R:, R:], 0.0), gram)
    x = each(lambda a_, ak, v_: jnp.concatenate([a_, _mm(ak, v_)], axis=1), at, a_ak, v)
    d_inv = [eye_r for _ in chains]
    for li, lower_left in enumerate(levels):
        a_off = each(lambda ab: jnp.where(lower_left, ab, 0.0), a_ab)
        if li == 0:
            d_inv = each(lambda d, ao: d + ao, d_inv, a_off)
        else:
            half = each(lambda ao, d: _mm(ao, d), a_off, d_inv)
            d_inv = each(lambda d, hf: d + _mm(d, hf), d_inv, half)
    x = each(lambda d, x_: _mm(d, x_), d_inv, x)
    ax = each(lambda rb, x_: _mm(rb, x_), a_rb, x)
    qt = each(lambda r_, ax_: r_ + ax_[:, :N], rt, ax)
    y0 = each(lambda ax_, rk_, v_: ax_[:, N:] + _mm(rk_, v_), ax, a_rk, v)
    for ci, (bi, h0) in enumerate(chains):
        outs = []
        for h in range(nh):
            sl = slice(h * C, (h + 1) * C)
            p_c = jnp.exp(cum[ci][h * C + C - 1:h * C + C, :])
            m = (eye + _mm_tn(x[ci][sl, :N], bt[ci][sl])) * p_c
            s_loc = _mm_tn(jnp.concatenate([x[ci][sl, N:], v[ci][sl]], axis=0),
                           jnp.concatenate([bt[ci][sl], kt[ci][sl]], axis=0)) * p_c
            s = s_ref[bi, h0 + h]
            y = _mm_nt(qt[ci][sl], s) + y0[ci][sl]
            s_ref[bi, h0 + h] = _mm(s, m) + s_loc
            mu = jnp.mean(y, axis=-1, keepdims=True)
            yc = y - mu
            var = jnp.mean(yc * yc, axis=-1, keepdims=True)
            yn = yc * lax.rsqrt(var + RWKV_LN_EPS) * lnw_ref[h0 + h] + lnb_ref[h0 + h]
            yn = yn + jnp.sum(r[ci][sl] * k[ci][sl] * rk_ref[h0 + h], axis=-1, keepdims=True) * v[ci][sl]
            outs.append(yn * g[ci][sl])
        o_ref[bi, :, h0 * N:(h0 + nh) * N] = jnp.concatenate(outs, axis=1)

    @pl.when(t_idx == pl.num_programs(1) - 1)
    def _():
        so_ref[...] = s_ref[...]


def _rwkv(streams, s0, lnw, lnb, rk, chunk, n_seq):
    b, nh, t, n = streams[0].shape
    n_seq = _row_tile(b, n_seq)
    st_spec = pl.BlockSpec((n_seq, nh, chunk, n), lambda i, j: (i, 0, j, 0))
    s_spec = pl.BlockSpec((n_seq, nh, n, n), lambda i, j: (i, 0, 0, 0))
    p_spec = _full((nh, 1, n))
    return pl.pallas_call(
        functools.partial(_rwkv_kernel, chunk=chunk),
        grid=(b // n_seq, t // chunk),
        in_specs=[st_spec] * 7 + [s_spec, p_spec, p_spec, p_spec],
        out_specs=[pl.BlockSpec((n_seq, chunk, nh * n), lambda i, j: (i, j, 0)), s_spec],
        out_shape=[jax.ShapeDtypeStruct((b, t, nh * n), F32), jax.ShapeDtypeStruct((b, nh, n, n), F32)],
        scratch_shapes=[pltpu.VMEM((n_seq, nh, n, n), F32)],
        compiler_params=_cparams(("parallel", "arbitrary")),
        name="rwkv",
    )(*streams, s0, lnw, lnb, rk)


def _hgrn_kernel(c_ref, lb_ref, nw_ref, s0_ref, o_ref, so_ref, st_ref, cum_ref, kh_ref, *, chunk, t_valid):
    j = pl.program_id(1)
    tb = c_ref.shape[1]
    d = lb_ref.shape[1]
    K = HGRN_EXPAND
    c = chunk

    @pl.when(j == 0)
    def _():
        for h in range(HGRN_HEADS):
            st_ref[h] = s0_ref[0, h].T

    z = c_ref[0, :, d:2 * d]
    lb = lb_ref[...]
    ls = -_softplus(-z)
    x1 = jnp.log(jnp.maximum(lb, LB_FLOOR))
    x2 = jnp.log1p(-lb) + ls
    logf = jnp.maximum(x1, x2) + jnp.log1p(jnp.exp(-jnp.abs(x1 - x2)))
    kh = (1.0 - lb) * _sigmoid(-z)
    if t_valid < tb:
        ok = (lax.broadcasted_iota(I32, (tb, d), 0) + j * tb) < t_valid
        logf = jnp.where(ok, logf, 0.0)
        kh = jnp.where(ok, kh, 0.0)
    row = lax.broadcasted_iota(I32, (tb, tb), 0)
    col = lax.broadcasted_iota(I32, (tb, tb), 1)
    tri = jnp.where(jnp.logical_and(row // c == col // c, col <= row), 1.0, 0.0)
    cum_ref[...] = _mm_exact_lhs(tri, logf)
    kh_ref[...] = kh

    ones = jnp.ones((K, K), BF16)
    rr = lax.broadcasted_iota(I32, (c * c, K), 0)
    causal = (rr % c) <= (rr // c)
    sel = jnp.where(lax.broadcasted_iota(I32, (c, c * c), 1) // c == lax.broadcasted_iota(I32, (c, c * c), 0),
                    1.0, 0.0).astype(BF16)

    heads = range(HGRN_HEADS)
    each = lambda f, *cols: [f(*vals) for vals in zip(*cols)]

    def body(ci, carry):
        r0 = pl.multiple_of(ci * c, c)
        rows = pl.ds(r0, c)
        hq = [c_ref[0, rows, h * K:(h + 1) * K] for h in heads]
        q = each(lambda t: t * _sigmoid(t), hq)
        v = [c_ref[0, rows, 2 * d + h * K:2 * d + (h + 1) * K] for h in heads]
        hg = [c_ref[0, rows, 3 * d + h * K:3 * d + (h + 1) * K] for h in heads]
        bq = [cum_ref[rows, h * K:(h + 1) * K] for h in heads]
        kq = [kh_ref[rows, h * K:(h + 1) * K] for h in heads]
        e = each(lambda q_, k_, b_: jnp.concatenate(
            [q_[t:t + 1] * k_ * jnp.exp(jnp.minimum(b_[t:t + 1] - b_, 0.0)) for t in range(c)], axis=0),
            q, kq, bq)
        att = each(lambda e_: jnp.dot(e_.astype(BF16), ones, preferred_element_type=F32), e)
        w = each(lambda a_, v_: jnp.where(causal, a_, 0.0) * jnp.concatenate([v_] * c, axis=0), att, v)
        st = [st_ref[h] for h in heads]
        o = each(lambda w_, q_, b_, s_: jnp.dot(sel, w_.astype(BF16), preferred_element_type=F32)
                 + _mm_nt(q_ * jnp.exp(b_), s_), w, q, bq, st)
        upd = each(lambda s_, b_, v_, k_: s_ * jnp.exp(b_[c - 1:c]) + _mm_tn(v_, k_ * jnp.exp(b_[c - 1:c] - b_)),
                   st, bq, v, kq)
        for h in heads:
            st_ref[h] = upd[h]
            on = o[h] * lax.rsqrt(jnp.mean(o[h] * o[h], axis=-1, keepdims=True) + NORM_EPS)
            o_ref[0, rows, h * K:(h + 1) * K] = on * nw_ref[:, h * K:(h + 1) * K] * (hg[h] * _sigmoid(hg[h]))
        return carry

    lax.fori_loop(0, tb // c, body, 0, unroll=min(4, tb // c))

    @pl.when(j == pl.num_programs(1) - 1)
    def _():
        for h in range(HGRN_HEADS):
            so_ref[0, h] = st_ref[h].T


def _hgrn(c_hg, lb, nw, s0, chunk, t_valid, tb):
    b, t, cols = c_hg.shape
    d = cols // 4
    tb = _row_tile(t, tb)
    s_spec = pl.BlockSpec((1, HGRN_HEADS, HGRN_EXPAND, HGRN_EXPAND), lambda i, j: (i, 0, 0, 0))
    return pl.pallas_call(
        functools.partial(_hgrn_kernel, chunk=chunk, t_valid=t_valid),
        grid=(b, t // tb),
        in_specs=[pl.BlockSpec((1, tb, cols), lambda i, j: (i, j, 0)), _full((1, d)), _full((1, d)), s_spec],
        out_specs=[pl.BlockSpec((1, tb, d), lambda i, j: (i, j, 0)), s_spec],
        out_shape=[jax.ShapeDtypeStruct((b, t, d), F32), jax.ShapeDtypeStruct(s0.shape, F32)],
        scratch_shapes=[pltpu.VMEM((HGRN_HEADS, HGRN_EXPAND, HGRN_EXPAND), F32), pltpu.VMEM((tb, d), F32),
                        pltpu.VMEM((tb, d), F32)],
        compiler_params=_cparams(("parallel", "arbitrary")),
        name="hgrn",
    )(c_hg, lb, nw, s0)


def _ordered_key(s):
    bits = pltpu.bitcast(s + 0.0, I32)
    return bits ^ ((bits >> 31) & 0x7FFFFFFF)


def _kth_largest(count_ge, shape, k, two_bits=False):
    c0 = count_ge(jnp.zeros(shape, I32))
    thr = jnp.where(c0 >= k, 0, INT_MIN).astype(I32)

    def body(i, thr):
        cand = thr | jnp.left_shift(jnp.int32(1), 30 - i)
        return jnp.where(count_ge(cand) >= k, cand, thr)

    if not two_bits:
        return lax.fori_loop(0, 31, body, thr)

    def body2(i, thr):
        hi = jnp.left_shift(jnp.int32(1), 30 - 2 * i)
        lo = jnp.left_shift(jnp.int32(1), 29 - 2 * i)
        c_lo, c_hi, c_both = thr | lo, thr | hi, thr | hi | lo
        n_lo, n_hi, n_both = count_ge(c_lo), count_ge(c_hi), count_ge(c_both)
        return jnp.where(n_both >= k, c_both, jnp.where(n_hi >= k, c_hi, jnp.where(n_lo >= k, c_lo, thr)))

    return body(30, lax.fori_loop(0, 15, body2, thr))


DSA_GROUP = 4


def _dsa_prompt_kernel(q_ref, qi_ref, kw_ref, k_ref, v_ref, ki_ref, bias_ref, o_ref,
                       key_ref, m_ref, acc_ref, *, topk):
    i = pl.program_id(1)
    QB, HD = Q_BLOCK, ATT_HEAD_DIM
    zeros64 = jnp.zeros((HD, QB), F32)

    q_t = q_ref[0].T
    rep = ATT_HEADS // ATT_KV_HEADS
    tiles = []
    for h in range(ATT_HEADS):
        qh = q_t[h * HD:(h + 1) * HD]
        tiles.append(jnp.concatenate([qh, zeros64] if h // rep == 0 else [zeros64, qh], axis=0))
    qs_t = (jnp.concatenate(tiles, axis=1) * (HD ** -0.5 * LOG2E)).astype(BF16)
    qi_t = qi_ref[0].T * IDX_DIM ** -0.5
    qi_pad = jnp.concatenate([jnp.concatenate([qi_t[h * IDX_DIM:(h + 1) * IDX_DIM], zeros64], axis=0)
                              for h in range(IDX_HEADS)], axis=1).astype(BF16)
    w_t = kw_ref[0].T
    w_rows = [_bf16_round(w_t[IDX_DIM + h:IDX_DIM + h + 1] * IDX_HEADS ** -0.5) for h in range(IDX_HEADS)]

    row = lax.broadcasted_iota(I32, (QB, QB), 0)
    col = lax.broadcasted_iota(I32, (QB, QB), 1)
    n_grp = (i + DSA_GROUP) // DSA_GROUP

    def score_group(gi, carry):
        for u in range(DSA_GROUP):
            j = gi * DSA_GROUP + u
            r0 = pl.multiple_of(j * QB, QB)
            dots = jnp.dot(ki_ref[0, pl.ds(r0, QB), :].astype(BF16), qi_pad, preferred_element_type=F32)
            s = jnp.zeros((QB, QB), F32)
            for h in range(IDX_HEADS):
                s = s + _bf16_round(jnp.maximum(dots[:, h * QB:(h + 1) * QB], 0.0)) * w_rows[h]
            vis = (j * QB + row) <= (i * QB + col)
            key = _ordered_key(jnp.where(vis, s, NEG_INF))
            key_ref[pl.ds(r0, QB), :] = jnp.where(j <= i, key, INT_MIN)
        return carry

    lax.fori_loop(0, n_grp, score_group, 0)

    def count_ge(cand):
        rows = DSA_GROUP * QB

        def grp(gi, acc):
            r0 = pl.multiple_of(gi * rows, rows)
            hit = jnp.where(key_ref[pl.ds(r0, rows), :] >= cand, 1, 0)
            return acc + jnp.sum(hit.reshape(rows // SUBLANES, SUBLANES, QB), axis=0)
        acc = lax.fori_loop(0, n_grp, grp, jnp.zeros((SUBLANES, QB), I32))
        return jnp.sum(acc, axis=0, keepdims=True)

    thr = _kth_largest(count_ge, (1, QB), topk)
    need = (topk - count_ge(thr + 1)).astype(F32)

    m_ref[...] = jnp.full(m_ref.shape, NEG_INF, F32)
    acc_ref[...] = jnp.zeros(acc_ref.shape, F32)
    tri = jnp.where(col <= row, 1.0, 0.0).astype(BF16)

    def attn_pair(jp, taken):
        lgs = []
        for u in range(2):
            j = 2 * jp + u
            r0 = pl.multiple_of(j * QB, QB)
            key = key_ref[pl.ds(r0, QB), :]
            eq = key == thr
            prefix = jnp.dot(tri, jnp.where(eq, 1.0, 0.0).astype(BF16), preferred_element_type=F32)
            sel = jnp.logical_or(key > thr, jnp.logical_and(eq, taken + prefix <= need))
            taken = taken + prefix[QB - 1:QB, :]
            vis = (j * QB + row) <= (i * QB + col)
            madd = jnp.where(jnp.logical_and(sel, vis), 0.0, NEG_INF)
            lg = jnp.dot(k_ref[0, pl.ds(r0, QB), :].astype(BF16), qs_t, preferred_element_type=F32)
            lgs.append(lg + bias_ref[jnp.clip(i - j, 0, 2)] + jnp.concatenate([madd] * ATT_HEADS, axis=1))
        m_old = m_ref[...]
        m_new = jnp.maximum(m_old, jnp.maximum(jnp.max(lgs[0], axis=0, keepdims=True),
                                               jnp.max(lgs[1], axis=0, keepdims=True)))
        alpha = jnp.exp2(m_old - m_new)
        p = jnp.concatenate([jnp.exp2(lg - m_new).astype(BF16) for lg in lgs], axis=0)
        v_pair = v_ref[0, pl.ds(pl.multiple_of(jp * 2 * QB, 2 * QB), 2 * QB), :]
        v_aug = jnp.concatenate([v_pair.T.astype(BF16), jnp.ones((SUBLANES, 2 * QB), BF16)], axis=0)
        acc_ref[...] = alpha * acc_ref[...] + jnp.dot(v_aug, p, preferred_element_type=F32)
        m_ref[...] = m_new
        return taken

    lax.fori_loop(0, (i + 2) // 2, attn_pair, jnp.zeros((1, QB), F32))

    acc = acc_ref[...]
    out_t = (acc[:LANES] / acc[LANES:LANES + 1]).T
    o_ref[0] = jnp.concatenate(
        [out_t[h * QB:(h + 1) * QB, (h // rep) * HD:(h // rep + 1) * HD] for h in range(ATT_HEADS)], axis=1)


def _dsa_prompt(c_att, bias_t):
    b, t, _ = c_att.shape
    nq = t // Q_BLOCK
    assert nq % DSA_GROUP == 0, (t, Q_BLOCK, DSA_GROUP)
    t_keys =((nq + DSA_GROUP - 1) // DSA_GROUP) * DSA_GROUP * Q_BLOCK
    blk = lambda w, cb: pl.BlockSpec((1, Q_BLOCK, w), lambda bi, i: (bi, i, cb))
    allk = lambda cb: pl.BlockSpec((1, t, LANES), lambda bi, i: (bi, 0, cb))
    return pl.pallas_call(
        functools.partial(_dsa_prompt_kernel, topk=min(TOPK, t // 4)),
        grid=(b, nq),
        in_specs=[blk(512, ATT_Q0 // 512), blk(256, ATT_QI0 // 256), blk(LANES, ATT_KI0 // LANES),
                  allk(ATT_K0 // LANES), allk(ATT_V0 // LANES), allk(ATT_KI0 // LANES), _full(bias_t.shape)],
        out_specs=pl.BlockSpec((1, Q_BLOCK, ATT_HEADS * ATT_HEAD_DIM), lambda bi, i: (bi, i, 0)),
        out_shape=jax.ShapeDtypeStruct((b, t, ATT_HEADS * ATT_HEAD_DIM), F32),
        scratch_shapes=[pltpu.VMEM((t_keys, Q_BLOCK), I32), pltpu.VMEM((1, ATT_HEADS * Q_BLOCK), F32),
                        pltpu.VMEM((LANES + SUBLANES, ATT_HEADS * Q_BLOCK), F32)],
        compiler_params=_cparams(("parallel", "arbitrary")),
        name="dsa_prompt",
    )(c_att, c_att, c_att, c_att, c_att, c_att, bias_t)


def _dsa_sample_kernel(pt_ref, l_ref, kidx_hbm, k_hbm, v_hbm, c_ref, bias_ref, o_ref,
                       ki_buf, k_buf, v_buf, sems, key_ref, madd_ref, *, n_pages, tq, topk):
    b = pl.program_id(0)
    slot = b % 2
    layer = l_ref[0]
    HD = ATT_HEAD_DIM
    rep = ATT_HEADS // ATT_KV_HEADS
    n_past = n_pages * PAGE

    def page_copies(page, p, sl):
        cols = pl.ds(pl.multiple_of(p * PAGE, PAGE), PAGE)
        return (pltpu.make_async_copy(kidx_hbm.at[layer, page], ki_buf.at[sl, :, cols], sems.at[0, sl]),
                pltpu.make_async_copy(k_hbm.at[layer, page], k_buf.at[sl, :, cols], sems.at[1, sl]),
                pltpu.make_async_copy(v_hbm.at[layer, page], v_buf.at[sl, :, cols], sems.at[2, sl]))

    def start_gather(bi, sl):
        def body(p, carry):
            for cp in page_copies(pt_ref[bi * n_pages + p], p, sl):
                cp.start()
            return carry
        lax.fori_loop(0, n_pages, body, 0)

    def wait_gather(sl):
        def body(p, carry):
            for cp in page_copies(0, p, sl):
                cp.wait()
            return carry
        lax.fori_loop(0, n_pages, body, 0)

    @pl.when(b == 0)
    def _():
        start_gather(0, 0)

    @pl.when(b + 1 < pl.num_programs(0))
    def _():
        start_gather(b + 1, 1 - slot)

    wait_gather(slot)

    qi = c_ref[0, :, ATT_QI0:ATT_QI0 + IDX_HEADS * IDX_DIM] * IDX_DIM ** -0.5
    wi = _bf16_round(c_ref[0, :, ATT_WI0:ATT_WI0 + IDX_HEADS] * IDX_HEADS ** -0.5)
    qi_rows = jnp.concatenate([qi[:, h * IDX_DIM:(h + 1) * IDX_DIM] for h in range(IDX_HEADS)], axis=0)

    def scores(dots):
        d = _bf16_round(jnp.maximum(dots, 0.0))
        s = jnp.zeros((tq, dots.shape[1]), F32)
        for h in range(IDX_HEADS):
            s = s + d[h * tq:(h + 1) * tq] * wi[:, h:h + 1]
        return s

    causal_new = lax.broadcasted_iota(I32, (tq, PAGE), 1) <= lax.broadcasted_iota(I32, (tq, PAGE), 0)
    key_ref[:, 0:n_past] = _ordered_key(scores(_mm(qi_rows, ki_buf[slot])))
    ki_new = jnp.concatenate([c_ref[0, :, ATT_KI0:ATT_KI0 + IDX_DIM], jnp.zeros((PAGE - tq, IDX_DIM), F32)], axis=0)
    key_ref[:, n_past:n_past + PAGE] = _ordered_key(jnp.where(causal_new, scores(_mm_nt(qi_rows, ki_new)), NEG_INF))
    keys = key_ref[...]
    count_ge = lambda cand: jnp.sum(jnp.where(keys >= cand, 1, 0), axis=1, keepdims=True)
    thr = _kth_largest(count_ge, (tq, 1), topk, two_bits=True)
    excess =jnp.max(count_ge(thr) - topk)

    @pl.when(excess == 0)
    def _():
        madd_ref[...] = jnp.where(keys >= thr, 0.0, NEG_INF)

    @pl.when(excess > 0)
    def _():
        need = (topk - count_ge(thr + 1)).astype(F32)
        tri = jnp.where(lax.broadcasted_iota(I32, (PAGE, PAGE), 0) <= lax.broadcasted_iota(I32, (PAGE, PAGE), 1),
                        1.0, 0.0).astype(BF16)

        def blk(jb, taken):
            cols = pl.ds(pl.multiple_of(jb * PAGE, PAGE), PAGE)
            key = key_ref[:, cols]
            eq = key == thr
            prefix = jnp.dot(jnp.where(eq, 1.0, 0.0).astype(BF16), tri, preferred_element_type=F32)
            sel = jnp.logical_or(key > thr, jnp.logical_and(eq, taken + prefix <= need))
            madd_ref[:, cols] = jnp.where(sel, 0.0, NEG_INF)
            return taken + prefix[:, PAGE - 1:PAGE]

        lax.fori_loop(0, n_pages + 1, blk, jnp.zeros((tq, 1), F32))

    q = c_ref[0, :, ATT_Q0:ATT_Q0 + ATT_HEADS * HD] * HD ** -0.5
    z = jnp.zeros((tq, HD), F32)
    qs = jnp.concatenate([jnp.concatenate([q[:, h * HD:(h + 1) * HD], z] if h // rep == 0
                                          else [z, q[:, h * HD:(h + 1) * HD]], axis=1)
                          for h in range(ATT_HEADS)], axis=0)
    pad = jnp.zeros((PAGE - tq, LANES), F32)
    k_new = jnp.concatenate([c_ref[0, :, ATT_K0:ATT_K0 + LANES], pad], axis=0)
    v_new = jnp.concatenate([c_ref[0, :, ATT_V0:ATT_V0 + LANES], pad], axis=0)
    lg_past = _mm(qs, k_buf[slot])
    far = bias_ref[2][:, 0:1]
    madd_ref[:, n_past:n_past + PAGE] = jnp.where(causal_new, madd_ref[:, n_past:n_past + PAGE], NEG_INF)
    madd = madd_ref[...]
    lg = jnp.concatenate([lg_past[:, :n_past - PAGE] + far, lg_past[:, n_past - PAGE:] + bias_ref[1],
                          _mm_nt(qs, k_new) + bias_ref[0]], axis=1) + jnp.concatenate([madd] * ATT_HEADS, axis=0)
    pr = jnp.exp(lg - jnp.max(lg, axis=-1, keepdims=True))
    out = (_mm_nt(pr[:, :n_past], v_buf[slot]) + _mm(pr[:, n_past:], v_new)) / jnp.sum(pr, axis=-1, keepdims=True)
    o_ref[0] = jnp.concatenate(
        [out[h * tq:(h + 1) * tq, (h // rep) * HD:(h // rep + 1) * HD] for h in range(ATT_HEADS)], axis=1)


def _dsa_sample(c_att, t_valid, layer, page_table, kidx_pool, k_pool, v_pool, bias_s):
    b, tq, _ = c_att.shape
    n_pages = page_table.shape[1]
    pt = page_table.reshape(-1)
    n_past = n_pages * PAGE
    n_keys = n_past + PAGE
    hbm = pl.BlockSpec(memory_space=pl.ANY)
    return pl.pallas_call(
        functools.partial(_dsa_sample_kernel, n_pages=n_pages, tq=tq, topk=min(TOPK, (n_past + t_valid) // 4)),
        grid_spec=pltpu.PrefetchScalarGridSpec(
            num_scalar_prefetch=2,
            grid=(b,),
            in_specs=[hbm, hbm, hbm, pl.BlockSpec((1, tq, ATT_W), lambda bi, pt_ref, l_ref: (bi, 0, 0)),
                      pl.BlockSpec(bias_s.shape, lambda bi, pt_ref, l_ref: (0, 0, 0))],
            out_specs=pl.BlockSpec((1, tq, ATT_HEADS * ATT_HEAD_DIM), lambda bi, pt_ref, l_ref: (bi, 0, 0)),
            scratch_shapes=[pltpu.VMEM((2, IDX_DIM, n_past), F32), pltpu.VMEM((2, LANES, n_past), F32),
                            pltpu.VMEM((2, LANES, n_past), F32), pltpu.SemaphoreType.DMA((3, 2)),
                            pltpu.VMEM((tq, n_keys), I32), pltpu.VMEM((tq, n_keys), F32)],
        ),
        out_shape=jax.ShapeDtypeStruct((b, tq, ATT_HEADS * ATT_HEAD_DIM), F32),
        compiler_params=_cparams(("arbitrary",)),
        name="dsa_sample",
    )(pt, layer, kidx_pool, k_pool, v_pool, c_att, bias_s)


def _t5_bucket(dist):
    n = jnp.maximum(dist, 0)
    max_exact = NUM_BUCKETS // 2
    nf = jnp.maximum(n, 1).astype(F32)
    large = max_exact + (jnp.log(nf / max_exact) / math.log(MAX_DISTANCE / max_exact)
                         * (NUM_BUCKETS - max_exact)).astype(I32)
    large = jnp.minimum(large, NUM_BUCKETS - 1)
    return jnp.where(n < max_exact, n, large)


def _bias_tiles(rel_bias):
    r = jnp.arange(Q_BLOCK)
    d0 = r[:, None] - r[None, :]
    dist = jnp.stack([d0, d0 + Q_BLOCK, jnp.full_like(d0, 2 * Q_BLOCK)])
    onehot = (_t5_bucket(dist)[..., None] == jnp.arange(NUM_BUCKETS)).astype(F32)
    return jnp.einsum("cqkn,nh->chqk", onehot, rel_bias.astype(F32), precision=lax.Precision.HIGHEST)


def _group_layer(x, w, lb, mem, mem_blocks, shift, s_rwkv0, s_hgrn0, attend, cfg):
    b, t, d = x.shape
    n = b * t
    tm, t_valid = cfg["tm"], cfg["t_valid"]
    x2 = x.reshape(n, d)
    x2 = _ffn(x2, w["ffn1_norm_pre"], w["ffn1_w_up"], w["ffn1_w_down"], w["ffn1_norm_post"], cfg["tm_ffn"])
    c_rw, c_att, c_hg, gates = _proj(x2, w["mix_norm_pre"], [w["w_in_rw"], w["w_in_att"], w["w_in_hg"], w["w_in_gate"]],
                                     True, tm, "mix_in")
    c_rw = c_rw.reshape(b, t, -1)
    c_att = c_att.reshape(b, t, -1)
    streams = _rwkv_prep(c_rw, shift, w, t_valid, cfg["tm_prep"])
    o_a, s_rwkv = _rwkv(streams, s_rwkv0, w["rwkv_ln_w"], w["rwkv_ln_b"], w["rwkv_r_k"], cfg["rwkv_chunk"],
                         cfg["rwkv_seqs"])
    o_b = attend(c_att)
    o_c, s_hgrn = _hgrn(c_hg.reshape(b, t, -1), lb, w["hgrn_norm_w"], s_hgrn0, cfg["hgrn_chunk"], t_valid, cfg["hgrn_tb"])
    x2 = _merge(x2, o_a.reshape(n, -1), o_b.reshape(n, -1), o_c.reshape(n, -1), gates,
                w["mix_w_proj_a"], w["mix_w_proj_b"], w["mix_w_proj_c"], w["mix_w_out"], w["mix_norm_post"], tm)
    (q,) = _proj(x2, w["cross_norm_pre"], [w["cross_wq"]], True, tm, "cross_q")
    o = _xattn(q.reshape(b, t, d), mem[0], mem[1], mem_blocks[0], mem_blocks[1], cfg["tq_x"])
    x2 = _out_proj(x2, o.reshape(n, d), w["cross_wo"], w["cross_norm_post"], tm)
    x2 = _ffn(x2, w["ffn2_norm_pre"], w["ffn2_w_up"], w["ffn2_w_down"], w["ffn2_norm_post"], cfg["tm_ffn"])
    nkv = ATT_KV_HEADS * ATT_HEAD_DIM
    state = (c_rw[:, t_valid - 1], s_rwkv, s_hgrn,
             c_att[:, :t_valid, ATT_K0:ATT_K0 + nkv].reshape(b, t_valid, ATT_KV_HEADS, ATT_HEAD_DIM),
             c_att[:, :t_valid, ATT_V0:ATT_V0 + nkv].reshape(b, t_valid, ATT_KV_HEADS, ATT_HEAD_DIM),
             c_att[:, :t_valid, ATT_KI0:ATT_KI0 + IDX_DIM])
    return x2.reshape(b, t, d), state


PROMPT_CFG = dict(tm=512, tm_ffn=512, tm_prep=256, rwkv_chunk=64, rwkv_seqs=4, hgrn_chunk=16, hgrn_tb=256, tq_x=512)
SAMPLE_PAD = 8
SAMPLE_CFG = dict(tm=256, tm_ffn=256, tm_prep=SAMPLE_PAD, rwkv_chunk=SAMPLE_PAD, rwkv_seqs=4, hgrn_chunk=SAMPLE_PAD,
                  hgrn_tb=SAMPLE_PAD, tq_x=SAMPLE_PAD)


def kernel(x_prompt, x_sample, cache_k, cache_v, cache_kidx, cache_mem_k, cache_mem_v, state_rwkv, state_rwkv_shift, state_hgrn, page_table, mem_prompt, rel_bias, hgrn_lb_logits, ffn1_norm_pre, ffn1_norm_post, ffn1_w_up, ffn1_w_down, mix_norm_pre, mix_norm_post, mix_w_in, rwkv_mu, rwkv_w0, rwkv_w_up, rwkv_a0, rwkv_a_up, rwkv_g_up, rwkv_k_k, rwkv_k_a, rwkv_r_k, rwkv_ln_w, rwkv_ln_b, hgrn_norm_w, mix_w_proj_a, mix_w_proj_b, mix_w_proj_c, mix_w_out, cross_norm_pre, cross_norm_post, cross_wq, cross_wk, cross_wv, cross_wo, ffn2_norm_pre, ffn2_norm_post, ffn2_w_up, ffn2_w_down):
    depth, d = ffn1_norm_pre.shape
    bp, tp, _ = x_prompt.shape
    bs, ts, _ = x_sample.shape
    d_r = rwkv_w0.shape[1]
    rw_cols = rwkv_mu.shape[1]
    d_h = hgrn_norm_w.shape[1]
    n_att = ATT_WI0 + IDX_HEADS - ATT_Q0
    bf = lambda a: a.astype(BF16)
    vec = lambda a: a[:, None, :]

    o1 = rw_cols + n_att
    w_att = jnp.pad(mix_w_in[:, :, rw_cols:o1], ((0, 0), (0, 0), (0, ATT_W - n_att)))
    head_sum = (np.arange(d_r)[:, None] // RWKV_HEAD == np.arange(d_r)[None, :] // RWKV_HEAD).astype(np.float32)
    per_head = lambda a: a.reshape(depth, RWKV_HEADS, 1, RWKV_HEAD)
    weights = dict(
        ffn1_norm_pre=vec(ffn1_norm_pre), ffn1_norm_post=vec(ffn1_norm_post), ffn1_w_up=bf(ffn1_w_up), ffn1_w_down=bf(ffn1_w_down),
        ffn2_norm_pre=vec(ffn2_norm_pre), ffn2_norm_post=vec(ffn2_norm_post), ffn2_w_up=bf(ffn2_w_up), ffn2_w_down=bf(ffn2_w_down),
        mix_norm_pre=vec(mix_norm_pre), mix_norm_post=vec(mix_norm_post),
        w_in_rw=bf(mix_w_in[:, :, :rw_cols]), w_in_att=bf(w_att),
        w_in_hg=bf(mix_w_in[:, :, o1:o1 + 4 * d_h]), w_in_gate=bf(mix_w_in[:, :, o1 + 4 * d_h:]),
        rwkv_mu=vec(rwkv_mu), rwkv_w0=vec(rwkv_w0), rwkv_w_up=bf(rwkv_w_up), rwkv_a0=vec(rwkv_a0), rwkv_a_up=bf(rwkv_a_up),
        rwkv_g_up=bf(rwkv_g_up), rwkv_k_k=vec(rwkv_k_k), rwkv_k_a=vec(rwkv_k_a),
        rwkv_r_k=rwkv_r_k[:, :, None, :], rwkv_ln_w=per_head(rwkv_ln_w), rwkv_ln_b=per_head(rwkv_ln_b),
        hgrn_norm_w=vec(hgrn_norm_w),
        mix_w_proj_a=bf(mix_w_proj_a), mix_w_proj_b=bf(mix_w_proj_b), mix_w_proj_c=bf(mix_w_proj_c), mix_w_out=bf(mix_w_out),
        cross_norm_pre=vec(cross_norm_pre), cross_norm_post=vec(cross_norm_post), cross_wq=bf(cross_wq), cross_wo=bf(cross_wo),
        cross_wkv=bf(jnp.concatenate([cross_wk, cross_wv], axis=-1)),
    )
    p_lb = jax.nn.softmax(hgrn_lb_logits.astype(F32), axis=0)
    lower_bounds = vec(jnp.cumsum(p_lb, axis=0) - p_lb[0:1])

    bias = _bias_tiles(rel_bias)
    bias_t = jnp.transpose(bias * LOG2E, (0, 3, 1, 2)).reshape(3, Q_BLOCK, ATT_HEADS * Q_BLOCK)
    bias_s = bias[:, :, :SAMPLE_PAD, :].reshape(3, ATT_HEADS * SAMPLE_PAD, Q_BLOCK)

    pad_t = SAMPLE_PAD - ts
    xs0 = jnp.pad(x_sample, ((0, 0), (0, pad_t), (0, 0)))
    n_pool = cache_k.shape[1]
    nkv = ATT_KV_HEADS * ATT_HEAD_DIM
    k_pool = jnp.transpose(cache_k, (0, 1, 3, 4, 2)).reshape(depth, n_pool, nkv, PAGE)
    v_pool = jnp.transpose(cache_v, (0, 1, 3, 4, 2)).reshape(depth, n_pool, nkv, PAGE)
    kidx_pool = jnp.swapaxes(cache_kidx, 2, 3)
    mem_tokens = mem_prompt.shape[1]
    mem2 = mem_prompt.reshape(bp * mem_tokens, d)
    ones_d = jnp.ones((1, d), F32)
    zero_shift = jnp.zeros((bp, 1, rw_cols), F32)
    zero_rwkv = jnp.zeros((bp, RWKV_HEADS, RWKV_HEAD, RWKV_HEAD), F32)
    zero_hgrn = jnp.zeros((bp, HGRN_HEADS, HGRN_EXPAND, HGRN_EXPAND), F32)
    prompt_cfg = dict(PROMPT_CFG, t_valid=tp)
    sample_cfg = dict(SAMPLE_CFG, t_valid=ts)

    def layer(carry, per_layer):
        xp, xs = carry
        w, lb, mem_ks, mem_vs, s_rw, s_sh, s_hg, li = per_layer
        w = dict(w, head_sum=jnp.asarray(head_sum, BF16))
        (mem_kv,) = _proj(mem2, ones_d, [w["cross_wkv"]], False, 256, "mem_kv")
        mem_kv = mem_kv.reshape(bp, mem_tokens, 2 * d)
        xp, st_p = _group_layer(xp, w, lb, (mem_kv, mem_kv), (0, 1), zero_shift, zero_rwkv, zero_hgrn,
                                lambda c: _dsa_prompt(c, bias_t), prompt_cfg)
        attend_s = lambda c: _dsa_sample(c, ts, li, page_table, kidx_pool, k_pool, v_pool, bias_s)
        xs, st_s = _group_layer(xs, w, lb, (mem_ks, mem_vs), (0, 0), s_sh, s_rw, s_hg, attend_s, sample_cfg)
        mem_k = mem_kv[:, :, :d].reshape(bp, mem_tokens, MEM_HEADS, d // MEM_HEADS)
        mem_v = mem_kv[:, :, d:].reshape(bp, mem_tokens, MEM_HEADS, d // MEM_HEADS)
        return (xp, xs), (st_p, (mem_k, mem_v), st_s)

    per_layer = (weights, lower_bounds,
                 cache_mem_k.reshape(depth, bs, mem_tokens, d), cache_mem_v.reshape(depth, bs, mem_tokens, d),
                 state_rwkv, state_rwkv_shift[:, :, None, :], state_hgrn,
                 jnp.arange(depth, dtype=I32)[:, None])
    (xp, xs), (st_p, (mem_k, mem_v), st_s) = lax.scan(layer, (x_prompt, xs0), per_layer)
    sh_p, rw_p, hg_p, k_p, v_p, ki_p = st_p
    sh_s, rw_s, hg_s, k_s, v_s, ki_s = st_s
    return (xp, xs[:, :ts], k_p, v_p, ki_p, mem_k, mem_v, rw_p, sh_p, hg_p, k_s, v_s, ki_s, rw_s, sh_s, hg_s)
```

```python
import functools
import math

import jax
import jax.numpy as jnp
import numpy as np
from jax import lax
from jax.experimental import pallas as pl
from jax.experimental.pallas import tpu as pltpu

F32 = jnp.float32
BF16 = jnp.bfloat16
I32 = jnp.int32

LANES = 128
SUBLANES = 8
VMEM_LIMIT_BYTES = 56 * 1024 * 1024

NORM_EPS = 1e-6
RWKV_LN_EPS = 64e-5
LB_FLOOR = 1e-30
NEG_INF = -1e30
INT_MIN = int(np.iinfo(np.int32).min)
LOG2E = math.log2(math.e)

RWKV_HEAD = 64
RWKV_HEADS = 8
RWKV_GROUP = 4
HGRN_HEADS = 4
HGRN_EXPAND = 128
ATT_HEADS = 8
ATT_KV_HEADS = 2
ATT_HEAD_DIM = 64
IDX_HEADS = 4
IDX_DIM = 64
TOPK = 256
Q_BLOCK = 128
MEM_HEADS = 4
NUM_BUCKETS = 32
MAX_DISTANCE = 128
PAGE = 128

ATT_Q0, ATT_K0, ATT_V0, ATT_QI0, ATT_KI0, ATT_WI0, ATT_W = 0, 512, 640, 768, 1024, 1088, 1152


def _cparams(sem):
    return pltpu.CompilerParams(dimension_semantics=sem, vmem_limit_bytes=VMEM_LIMIT_BYTES)


def _mm(a, b):
    return jnp.dot(a.astype(BF16), b.astype(BF16), preferred_element_type=F32)


def _mm_nt(a, b):
    return lax.dot_general(a.astype(BF16), b.astype(BF16), (((1,), (1,)), ((), ())),
                           preferred_element_type=F32)


def _mm_tn(a, b):
    return lax.dot_general(a.astype(BF16), b.astype(BF16), (((0,), (0,)), ((), ())),
                           preferred_element_type=F32)


def _bf16_round(x):
    return x.astype(BF16).astype(F32)


def _split3(x):
    hi = x.astype(BF16)
    r1 = x - hi.astype(F32)
    mid = r1.astype(BF16)
    lo = (r1 - mid.astype(F32)).astype(BF16)
    return hi, mid, lo


def _mm_exact_lhs(a01, x):
    a = a01.astype(BF16)
    hi, mid, lo = _split3(x)
    d = lambda p: jnp.dot(a, p, preferred_element_type=F32)
    return d(hi) + d(mid) + d(lo)


def _mm_exact_rhs(x, b01):
    b = b01.astype(BF16)
    hi, mid, lo = _split3(x)
    d = lambda p: jnp.dot(p, b, preferred_element_type=F32)
    return d(hi) + d(mid) + d(lo)


def _rms(x, g):
    return x * lax.rsqrt(jnp.mean(x * x, axis=-1, keepdims=True) + NORM_EPS) * g


def _sigmoid(x):
    return jax.nn.sigmoid(x)


def _softplus(x):
    return jnp.maximum(x, 0.0) + jnp.log1p(jnp.exp(-jnp.abs(x)))


def _full(shape):
    nd = len(shape)
    return pl.BlockSpec(shape, lambda *_: (0,) * nd)


def _row_tile(n, want):
    t = min(n, want)
    assert n % t == 0, (n, t)
    return t


def _ffn_kernel(x_ref, gpre_ref, wup_ref, wdn_ref, gpost_ref, o_ref, *, d_ff, tf):
    x = x_ref[...]
    h = _rms(x, gpre_ref[...]).astype(BF16)
    acc = jnp.zeros(x.shape, F32)
    for f0 in range(0, d_ff, tf):
        gate = jnp.dot(h, wup_ref[:, f0:f0 + tf], preferred_element_type=F32)
        up = jnp.dot(h, wup_ref[:, d_ff + f0:d_ff + f0 + tf], preferred_element_type=F32)
        act = (gate * _sigmoid(gate) * up).astype(BF16)
        acc = acc + jnp.dot(act, wdn_ref[f0:f0 + tf, :], preferred_element_type=F32)
    o_ref[...] = x + 0.5 * _rms(acc, gpost_ref[...])


def _ffn(x, gpre, wup, wdn, gpost, tm):
    n, d = x.shape
    d_ff = wdn.shape[0]
    tf = d_ff // 2 if (d_ff // 2) % LANES == 0 else d_ff
    tm = _row_tile(n, tm)
    return pl.pallas_call(
        functools.partial(_ffn_kernel, d_ff=d_ff, tf=tf),
        grid=(n // tm,),
        in_specs=[pl.BlockSpec((tm, d), lambda i: (i, 0)), _full((1, d)), _full(wup.shape), _full(wdn.shape),
                  _full((1, d))],
        out_specs=pl.BlockSpec((tm, d), lambda i: (i, 0)),
        out_shape=jax.ShapeDtypeStruct((n, d), F32),
        compiler_params=_cparams(("parallel",)),
        name="ffn",
    )(x, gpre, wup, wdn, gpost)


def _proj_kernel(*refs, n_out, norm, tn):
    x_ref, g_ref = refs[0], refs[1]
    w_refs = refs[2:2 + n_out]
    o_refs = refs[2 + n_out:]
    x = x_ref[...]
    h = (_rms(x, g_ref[...]) if norm else x).astype(BF16)
    for w_ref, o_ref in zip(w_refs, o_refs):
        n = w_ref.shape[1]
        step = tn if n % tn == 0 else n
        for n0 in range(0, n, step):
            o_ref[:, n0:n0 + step] = jnp.dot(h, w_ref[:, n0:n0 + step], preferred_element_type=F32)


def _proj(x, g, ws, norm, tm, name):
    n, d = x.shape
    tm = _row_tile(n, tm)
    return pl.pallas_call(
        functools.partial(_proj_kernel, n_out=len(ws), norm=norm, tn=512),
        grid=(n // tm,),
        in_specs=[pl.BlockSpec((tm, d), lambda i: (i, 0)), _full((1, d))] + [_full(w.shape) for w in ws],
        out_specs=[pl.BlockSpec((tm, w.shape[1]), lambda i: (i, 0)) for w in ws],
        out_shape=[jax.ShapeDtypeStruct((n, w.shape[1]), F32) for w in ws],
        compiler_params=_cparams(("parallel",)),
        name=name,
    )(x, g, *ws)


def _out_kernel(x_ref, a_ref, w_ref, g_ref, o_ref):
    y = jnp.dot(a_ref[...].astype(BF16), w_ref[...], preferred_element_type=F32)
    o_ref[...] = x_ref[...] + _rms(y, g_ref[...])


def _out_proj(x, a, w, g, tm):
    n, d = x.shape
    tm = _row_tile(n, tm)
    return pl.pallas_call(
        _out_kernel,
        grid=(n // tm,),
        in_specs=[pl.BlockSpec((tm, d), lambda i: (i, 0)), pl.BlockSpec((tm, a.shape[1]), lambda i: (i, 0)),
                  _full(w.shape), _full((1, d))],
        out_specs=pl.BlockSpec((tm, d), lambda i: (i, 0)),
        out_shape=jax.ShapeDtypeStruct((n, d), F32),
        compiler_params=_cparams(("parallel",)),
        name="out_proj",
    )(x, a, w, g)


def _merge_kernel(x_ref, oa_ref, ob_ref, oc_ref, gt_ref, wa_ref, wb_ref, wc_ref, wo_ref, g_ref, o_ref, *, d):
    m = jnp.zeros((x_ref.shape[0], d), F32)
    for j, (o_r, w_r) in enumerate(((oa_ref, wa_ref), (ob_ref, wb_ref), (oc_ref, wc_ref))):
        p = jnp.dot(o_r[...].astype(BF16), w_r[...], preferred_element_type=F32)
        m = m + _sigmoid(gt_ref[:, j * d:(j + 1) * d]) * p
    y = jnp.dot(m.astype(BF16), wo_ref[...], preferred_element_type=F32)
    o_ref[...] = x_ref[...] + _rms(y, g_ref[...])


def _merge(x, oa, ob, oc, gates, wa, wb, wc, wo, g, tm):
    n, d = x.shape
    tm = _row_tile(n, tm)
    row = lambda w: pl.BlockSpec((tm, w), lambda i: (i, 0))
    return pl.pallas_call(
        functools.partial(_merge_kernel, d=d),
        grid=(n // tm,),
        in_specs=[row(d), row(oa.shape[1]), row(ob.shape[1]), row(oc.shape[1]), row(gates.shape[1]),
                  _full(wa.shape), _full(wb.shape), _full(wc.shape), _full(wo.shape), _full((1, d))],
        out_specs=row(d),
        out_shape=jax.ShapeDtypeStruct((n, d), F32),
        compiler_params=_cparams(("parallel",)),
        name="merge",
    )(x, oa, ob, oc, gates, wa, wb, wc, wo, g)


def _xattn_kernel(q_ref, mk_ref, mv_ref, o_ref, *, heads):
    hd = q_ref.shape[2] // heads
    for h in range(heads):
        sl = slice(h * hd, (h + 1) * hd)
        lg = _mm_nt(q_ref[0, :, sl], mk_ref[0, :, sl]) * hd ** -0.5
        lg = lg - jnp.max(lg, axis=-1, keepdims=True)
        p = jnp.exp(lg)
        p = p / jnp.sum(p, axis=-1, keepdims=True)
        o_ref[0, :, sl] = _mm(p, mv_ref[0, :, sl])


def _xattn(q, mem_k, mem_v, k_blk, v_blk, tq):
    b, t, d = q.shape
    s = mem_k.shape[1]
    tq = _row_tile(t, tq)
    mem_specs = [pl.BlockSpec((1, s, d), lambda i, j: (i, 0, k_blk)),
                 pl.BlockSpec((1, s, d), lambda i, j: (i, 0, v_blk))]
    return pl.pallas_call(
        functools.partial(_xattn_kernel, heads=MEM_HEADS),
        grid=(b, t // tq),
        in_specs=[pl.BlockSpec((1, tq, d), lambda i, j: (i, j, 0))] + mem_specs,
        out_specs=pl.BlockSpec((1, tq, d), lambda i, j: (i, j, 0)),
        out_shape=jax.ShapeDtypeStruct((b, t, d), F32),
        compiler_params=_cparams(("parallel", "parallel")),
        name="xattn",
    )(q, mem_k, mem_v)


def _rwkv_prep_kernel(c_ref, sh_ref, mu_ref, w0_ref, wup_ref, a0_ref, aup_ref, gup_ref, kk_ref, ka_ref, hs_ref,
                      r_o, lw_o, k_o, v_o, kk_o, b_o, g_o, carry_ref, *, t_valid, d_r):
    j = pl.program_id(1)
    tm = c_ref.shape[1]

    @pl.when(j == 0)
    def _():
        carry_ref[...] = sh_ref[0]

    c = c_ref[0]
    row = lax.broadcasted_iota(I32, c.shape, 0)
    prev = jnp.where(row == 0, carry_ref[...], pltpu.roll(c, 1, axis=0))
    carry_ref[...] = c[tm - 1:tm, :]
    csh = c + (prev - c) * mu_ref[...]
    r, k, v = csh[:, 0:d_r], csh[:, d_r:2 * d_r], csh[:, 2 * d_r:3 * d_r]
    o = 3 * d_r
    n_w, n_a, n_g = wup_ref.shape[0], aup_ref.shape[0], gup_ref.shape[0]
    xw, xa, xg = csh[:, o:o + n_w], csh[:, o + n_w:o + n_w + n_a], csh[:, o + n_w + n_a:o + n_w + n_a + n_g]
    w_log = -_softplus(-(w0_ref[...] + _mm(jnp.tanh(xw), wup_ref[...]))) - 0.5
    lw = -jnp.exp(w_log)
    a_lr = _sigmoid(a0_ref[...] + _mm(xa, aup_ref[...]))
    g = _mm(_sigmoid(xg), gup_ref[...])
    kk = k * kk_ref[...]
    ss = _mm_exact_rhs(kk * kk, hs_ref[...])
    kk = kk / jnp.maximum(jnp.sqrt(ss), 1e-12)
    k_mod = k * (1.0 + (a_lr - 1.0) * ka_ref[...])
    b = kk * a_lr
    if t_valid < tm:
        ok = (lax.broadcasted_iota(I32, (tm, d_r), 0) + j * tm) < t_valid
        z = lambda t: jnp.where(ok, t, 0.0)
        lw, k_mod, v, kk, b = z(lw), z(k_mod), z(v), z(kk), z(b)
    for h in range(RWKV_HEADS):
        sl = slice(h * RWKV_HEAD, (h + 1) * RWKV_HEAD)
        for o_ref, val in ((r_o, r), (lw_o, lw), (k_o, k_mod), (v_o, v), (kk_o, kk), (b_o, b), (g_o, g)):
            o_ref[0, h] = val[:, sl]


def _rwkv_prep(c_rw, shift, p, t_valid, tm):
    b, t, cols = c_rw.shape
    d_r = p["rwkv_w0"].shape[1]
    tm = _row_tile(t, tm)
    hm = jax.ShapeDtypeStruct((b, RWKV_HEADS, t, RWKV_HEAD), F32)
    hm_spec = pl.BlockSpec((1, RWKV_HEADS, tm, RWKV_HEAD), lambda i, j: (i, 0, j, 0))
    params = [p["rwkv_mu"], p["rwkv_w0"], p["rwkv_w_up"], p["rwkv_a0"], p["rwkv_a_up"], p["rwkv_g_up"],
              p["rwkv_k_k"], p["rwkv_k_a"], p["head_sum"]]
    return pl.pallas_call(
        functools.partial(_rwkv_prep_kernel, t_valid=t_valid, d_r=d_r),
        grid=(b, t // tm),
        in_specs=[pl.BlockSpec((1, tm, cols), lambda i, j: (i, j, 0)),
                  pl.BlockSpec((1, 1, cols), lambda i, j: (i, 0, 0))] + [_full(a.shape) for a in params],
        out_specs=[hm_spec] * 7,
        out_shape=[hm] * 7,
        scratch_shapes=[pltpu.VMEM((1, cols), F32)],
        compiler_params=_cparams(("parallel", "arbitrary")),
        name="rwkv_prep",
    )(c_rw, shift, *params)


def _rwkv_kernel(r_ref, lw_ref, k_ref, v_ref, kk_ref, b_ref, g_ref, s0_ref, lnw_ref, lnb_ref, rk_ref,
                 o_ref, so_ref, s_ref, *, chunk):
    t_idx = pl.program_id(1)
    nh, C, N = RWKV_GROUP, chunk, RWKV_HEAD
    R = nh * C
    n_seq, n_heads = r_ref.shape[0], r_ref.shape[1]

    @pl.when(t_idx == 0)
    def _():
        s_ref[...] = s0_ref[...]

    row = lax.broadcasted_iota(I32, (R, R), 0)
    col = lax.broadcasted_iota(I32, (R, R), 1)
    same = (row // C) == (col // C)
    incl = jnp.logical_and(same, col <= row)
    strict = jnp.logical_and(same, col < row)
    incl01 = jnp.where(incl, 1.0, 0.0)
    eye_r = jnp.where(row == col, 1.0, 0.0)
    levels = []
    m_blk = 1
    while m_blk < C:
        levels.append(jnp.logical_and(row // (2 * m_blk) == col // (2 * m_blk),
                                      jnp.logical_and((row // m_blk) % 2 == 1, (col // m_blk) % 2 == 0)))
        m_blk *= 2
    eye = jnp.where(lax.broadcasted_iota(I32, (N, N), 0) == lax.broadcasted_iota(I32, (N, N), 1), 1.0, 0.0)

    chains = [(bi, h0) for bi in range(n_seq) for h0 in range(0, n_heads, nh)]
    each = lambda f, *cols: [f(*vals) for vals in zip(*cols)]
    ld = lambda ref: [ref[bi, h0:h0 + nh].reshape(R, N) for bi, h0 in chains]
    r, lw, k, v, kk, b, g = (ld(x) for x in (r_ref, lw_ref, k_ref, v_ref, kk_ref, b_ref, g_ref))
    cum = each(lambda t: _mm_exact_lhs(incl01, t), lw)
    e_neg = each(lambda c: jnp.exp(-c), cum)
    at = each(lambda kk_, c, l: -kk_ * jnp.exp(c - l), kk, cum, lw)
    rt = each(lambda r_, c: r_ * jnp.exp(c), r, cum)
    bt = each(lambda b_, e: b_ * e, b, e_neg)
    kt = each(lambda k_, e: k_ * e, k, e_neg)
    gram = each(lambda a_, r_, b_, k_: _mm_nt(jnp.concatenate([a_, r_], axis=0), jnp.concatenate([b_, k_], axis=0)),
                at, rt, bt, kt)
    a_ab = each(lambda gm: jnp.where(strict, gm[:R, :R], 0.0), gram)
    a_ak = each(lambda gm: jnp.where(strict, gm[:R, R:], 0.0), gram)
    a_rb = each(lambda gm: jnp.where(incl, gm[R:, :R], 0.0), gram)
    a_rk = each(lambda gm: jnp.where(incl, gm[R:, R:], 0.0), gram)
    x = each(lambda a_, ak, v_: jnp.concatenate([a_, _mm(ak, v_)], axis=1), at, a_ak, v)
    d_inv = [eye_r for _ in chains]
    for li, lower_left in enumerate(levels):
        a_off = each(lambda ab: jnp.where(lower_left, ab, 0.0), a_ab)
        if li == 0:
            d_inv = each(lambda d, ao: d + ao, d_inv, a_off)
        else:
            half = each(lambda ao, d: _mm(ao, d), a_off, d_inv)
            d_inv = each(lambda d, hf: d + _mm(d, hf), d_inv, half)
    x = each(lambda d, x_: _mm(d, x_), d_inv, x)
    ax = each(lambda rb, x_: _mm(rb, x_), a_rb, x)
    qt = each(lambda r_, ax_: r_ + ax_[:, :N], rt, ax)
    y0 = each(lambda ax_, rk_, v_: ax_[:, N:] + _mm(rk_, v_), ax, a_rk, v)
    for ci, (bi, h0) in enumerate(chains):
        outs = []
        for h in range(nh):
            sl = slice(h * C, (h + 1) * C)
            p_c = jnp.exp(cum[ci][h * C + C - 1:h * C + C, :])
            m = (eye + _mm_tn(x[ci][sl, :N], bt[ci][sl])) * p_c
            s_loc = _mm_tn(jnp.concatenate([x[ci][sl, N:], v[ci][sl]], axis=0),
                           jnp.concatenate([bt[ci][sl], kt[ci][sl]], axis=0)) * p_c
            s = s_ref[bi, h0 + h]
            y = _mm_nt(qt[ci][sl], s) + y0[ci][sl]
            s_ref[bi, h0 + h] = _mm(s, m) + s_loc
            mu = jnp.mean(y, axis=-1, keepdims=True)
            yc = y - mu
            var = jnp.mean(yc * yc, axis=-1, keepdims=True)
            yn = yc * lax.rsqrt(var + RWKV_LN_EPS) * lnw_ref[h0 + h] + lnb_ref[h0 + h]
            yn = yn + jnp.sum(r[ci][sl] * k[ci][sl] * rk_ref[h0 + h], axis=-1, keepdims=True) * v[ci][sl]
            outs.append(yn * g[ci][sl])
        o_ref[bi, :, h0 * N:(h0 + nh) * N] = jnp.concatenate(outs, axis=1)

    @pl.when(t_idx == pl.num_programs(1) - 1)
    def _():
        so_ref[...] = s_ref[...]


def _rwkv(streams, s0, lnw, lnb, rk, chunk, n_seq):
    b, nh, t, n = streams[0].shape
    n_seq = _row_tile(b, n_seq)
    st_spec = pl.BlockSpec((n_seq, nh, chunk, n), lambda i, j: (i, 0, j, 0))
    s_spec = pl.BlockSpec((n_seq, nh, n, n), lambda i, j: (i, 0, 0, 0))
    p_spec = _full((nh, 1, n))
    return pl.pallas_call(
        functools.partial(_rwkv_kernel, chunk=chunk),
        grid=(b // n_seq, t // chunk),
        in_specs=[st_spec] * 7 + [s_spec, p_spec, p_spec, p_spec],
        out_specs=[pl.BlockSpec((n_seq, chunk, nh * n), lambda i, j: (i, j, 0)), s_spec],
        out_shape=[jax.ShapeDtypeStruct((b, t, nh * n), F32), jax.ShapeDtypeStruct((b, nh, n, n), F32)],
        scratch_shapes=[pltpu.VMEM((n_seq, nh, n, n), F32)],
        compiler_params=_cparams(("parallel", "arbitrary")),
        name="rwkv",
    )(*streams, s0, lnw, lnb, rk)


def _hgrn_kernel(c_ref, lb_ref, nw_ref, s0_ref, o_ref, so_ref, st_ref, cum_ref, kh_ref, *, chunk, t_valid):
    j = pl.program_id(1)
    tb = c_ref.shape[1]
    d = lb_ref.shape[1]
    K = HGRN_EXPAND
    c = chunk

    @pl.when(j == 0)
    def _():
        for h in range(HGRN_HEADS):
            st_ref[h] = s0_ref[0, h].T

    z = c_ref[0, :, d:2 * d]
    lb = lb_ref[...]
    ls = -_softplus(-z)
    x1 = jnp.log(jnp.maximum(lb, LB_FLOOR))
    x2 = jnp.log1p(-lb) + ls
    logf = jnp.maximum(x1, x2) + jnp.log1p(jnp.exp(-jnp.abs(x1 - x2)))
    kh = (1.0 - lb) * _sigmoid(-z)
    if t_valid < tb:
        ok = (lax.broadcasted_iota(I32, (tb, d), 0) + j * tb) < t_valid
        logf = jnp.where(ok, logf, 0.0)
        kh = jnp.where(ok, kh, 0.0)
    row = lax.broadcasted_iota(I32, (tb, tb), 0)
    col = lax.broadcasted_iota(I32, (tb, tb), 1)
    tri = jnp.where(jnp.logical_and(row // c == col // c, col <= row), 1.0, 0.0)
    cum_ref[...] = _mm_exact_lhs(tri, logf)
    kh_ref[...] = kh

    ones = jnp.ones((K, K), BF16)
    rr = lax.broadcasted_iota(I32, (c * c, K), 0)
    causal = (rr % c) <= (rr // c)
    sel = jnp.where(lax.broadcasted_iota(I32, (c, c * c), 1) // c == lax.broadcasted_iota(I32, (c, c * c), 0),
                    1.0, 0.0).astype(BF16)

    heads = range(HGRN_HEADS)
    each = lambda f, *cols: [f(*vals) for vals in zip(*cols)]

    def body(ci, carry):
        r0 = pl.multiple_of(ci * c, c)
        rows = pl.ds(r0, c)
        hq = [c_ref[0, rows, h * K:(h + 1) * K] for h in heads]
        q = each(lambda t: t * _sigmoid(t), hq)
        v = [c_ref[0, rows, 2 * d + h * K:2 * d + (h + 1) * K] for h in heads]
        hg = [c_ref[0, rows, 3 * d + h * K:3 * d + (h + 1) * K] for h in heads]
        bq = [cum_ref[rows, h * K:(h + 1) * K] for h in heads]
        kq = [kh_ref[rows, h * K:(h + 1) * K] for h in heads]
        e = each(lambda q_, k_, b_: jnp.concatenate(
            [q_[t:t + 1] * k_ * jnp.exp(jnp.minimum(b_[t:t + 1] - b_, 0.0)) for t in range(c)], axis=0),
            q, kq, bq)
        att = each(lambda e_: jnp.dot(e_.astype(BF16), ones, preferred_element_type=F32), e)
        w = each(lambda a_, v_: jnp.where(causal, a_, 0.0) * jnp.concatenate([v_] * c, axis=0), att, v)
        st = [st_ref[h] for h in heads]
        o = each(lambda w_, q_, b_, s_: jnp.dot(sel, w_.astype(BF16), preferred_element_type=F32)
                 + _mm_nt(q_ * jnp.exp(b_), s_), w, q, bq, st)
        upd = each(lambda s_, b_, v_, k_: s_ * jnp.exp(b_[c - 1:c]) + _mm_tn(v_, k_ * jnp.exp(b_[c - 1:c] - b_)),
                   st, bq, v, kq)
        for h in heads:
            st_ref[h] = upd[h]
            on = o[h] * lax.rsqrt(jnp.mean(o[h] * o[h], axis=-1, keepdims=True) + NORM_EPS)
            o_ref[0, rows, h * K:(h + 1) * K] = on * nw_ref[:, h * K:(h + 1) * K] * (hg[h] * _sigmoid(hg[h]))
        return carry

    lax.fori_loop(0, tb // c, body, 0, unroll=min(4, tb // c))

    @pl.when(j == pl.num_programs(1) - 1)
    def _():
        for h in range(HGRN_HEADS):
            so_ref[0, h] = st_ref[h].T


def _hgrn(c_hg, lb, nw, s0, chunk, t_valid, tb):
    b, t, cols = c_hg.shape
    d = cols // 4
    tb = _row_tile(t, tb)
    s_spec = pl.BlockSpec((1, HGRN_HEADS, HGRN_EXPAND, HGRN_EXPAND), lambda i, j: (i, 0, 0, 0))
    return pl.pallas_call(
        functools.partial(_hgrn_kernel, chunk=chunk, t_valid=t_valid),
        grid=(b, t // tb),
        in_specs=[pl.BlockSpec((1, tb, cols), lambda i, j: (i, j, 0)), _full((1, d)), _full((1, d)), s_spec],
        out_specs=[pl.BlockSpec((1, tb, d), lambda i, j: (i, j, 0)), s_spec],
        out_shape=[jax.ShapeDtypeStruct((b, t, d), F32), jax.ShapeDtypeStruct(s0.shape, F32)],
        scratch_shapes=[pltpu.VMEM((HGRN_HEADS, HGRN_EXPAND, HGRN_EXPAND), F32), pltpu.VMEM((tb, d), F32),
                        pltpu.VMEM((tb, d), F32)],
        compiler_params=_cparams(("parallel", "arbitrary")),
        name="hgrn",
    )(c_hg, lb, nw, s0)


def _ordered_key(s):
    bits = pltpu.bitcast(s + 0.0, I32)
    return bits ^ ((bits >> 31) & 0x7FFFFFFF)


def _kth_largest(count_ge, shape, k, two_bits=False):
    c0 = count_ge(jnp.zeros(shape, I32))
    thr = jnp.where(c0 >= k, 0, INT_MIN).astype(I32)

    def body(i, thr):
        cand = thr | jnp.left_shift(jnp.int32(1), 30 - i)
        return jnp.where(count_ge(cand) >= k, cand, thr)

    if not two_bits:
        return lax.fori_loop(0, 31, body, thr)

    def body2(i, thr):
        hi = jnp.left_shift(jnp.int32(1), 30 - 2 * i)
        lo = jnp.left_shift(jnp.int32(1), 29 - 2 * i)
        c_lo, c_hi, c_both = thr | lo, thr | hi, thr | hi | lo
        n_lo, n_hi, n_both = count_ge(c_lo), count_ge(c_hi), count_ge(c_both)
        return jnp.where(n_both >= k, c_both, jnp.where(n_hi >= k, c_hi, jnp.where(n_lo >= k, c_lo, thr)))

    return body(30, lax.fori_loop(0, 15, body2, thr))


DSA_GROUP = 4
DSA_SEQS = 2


def _dsa_prompt_kernel(q_ref, qi_ref, kw_ref, k_ref, v_ref, ki_ref, bias_ref, o_ref,
                       key_ref, m_ref, acc_ref, *, topk):
    i = pl.program_id(1)
    QB, HD = Q_BLOCK, ATT_HEAD_DIM
    seqs = range(q_ref.shape[0])
    each = lambda f, *cols: [f(*vals) for vals in zip(*cols)]
    zeros64 = jnp.zeros((HD, QB), F32)
    rep = ATT_HEADS // ATT_KV_HEADS

    def stacked_queries(q_t):
        tiles = []
        for h in range(ATT_HEADS):
            qh = q_t[h * HD:(h + 1) * HD]
            tiles.append(jnp.concatenate([qh, zeros64] if h // rep == 0 else [zeros64, qh], axis=0))
        return (jnp.concatenate(tiles, axis=1) * (HD ** -0.5 * LOG2E)).astype(BF16)

    def padded_index_queries(qi_t):
        return jnp.concatenate([jnp.concatenate([qi_t[h * IDX_DIM:(h + 1) * IDX_DIM], zeros64], axis=0)
                                for h in range(IDX_HEADS)], axis=1).astype(BF16)

    qs_t = [stacked_queries(q_ref[s].T) for s in seqs]
    qi_pad = [padded_index_queries(qi_ref[s].T * IDX_DIM ** -0.5) for s in seqs]
    w_t = [kw_ref[s].T for s in seqs]
    w_rows = [[_bf16_round(w[IDX_DIM + h:IDX_DIM + h + 1] * IDX_HEADS ** -0.5) for h in range(IDX_HEADS)] for w in w_t]

    row = lax.broadcasted_iota(I32, (QB, QB), 0)
    col = lax.broadcasted_iota(I32, (QB, QB), 1)
    n_grp = (i + DSA_GROUP) // DSA_GROUP

    def score_group(gi, carry):
        for u in range(DSA_GROUP):
            j = gi * DSA_GROUP + u
            r0 = pl.multiple_of(j * QB, QB)
            vis = (j * QB + row) <= (i * QB + col)
            dots = [jnp.dot(ki_ref[s, pl.ds(r0, QB), :].astype(BF16), qi_pad[s], preferred_element_type=F32)
                    for s in seqs]
            for s in seqs:
                sc = jnp.zeros((QB, QB), F32)
                for h in range(IDX_HEADS):
                    sc = sc + _bf16_round(jnp.maximum(dots[s][:, h * QB:(h + 1) * QB], 0.0)) * w_rows[s][h]
                key = _ordered_key(jnp.where(vis, sc, NEG_INF))
                key_ref[s, pl.ds(r0, QB), :] = jnp.where(j <= i, key, INT_MIN)
        return carry

    lax.fori_loop(0, n_grp, score_group, 0)

    n_seq = q_ref.shape[0]

    def count_ge(cand):
        rows = DSA_GROUP * QB

        def grp(gi, accs):
            r0 = pl.multiple_of(gi * rows, rows)
            out = []
            for s in seqs:
                hit = jnp.where(key_ref[s, pl.ds(r0, rows), :] >= cand[s], 1, 0)
                out.append(accs[s] + jnp.sum(hit.reshape(rows // SUBLANES, SUBLANES, QB), axis=0))
            return tuple(out)
        accs = lax.fori_loop(0, n_grp, grp, tuple(jnp.zeros((SUBLANES, QB), I32) for _ in seqs))
        return jnp.stack([jnp.sum(a, axis=0, keepdims=True) for a in accs], axis=0)

    thr_all = _kth_largest(count_ge, (n_seq, 1, QB), topk)
    need_all = (topk - count_ge(thr_all + 1)).astype(F32)
    thr = [thr_all[s] for s in seqs]
    need = [need_all[s] for s in seqs]

    m_ref[...] = jnp.full(m_ref.shape, NEG_INF, F32)
    acc_ref[...] = jnp.zeros(acc_ref.shape, F32)
    tri = jnp.where(col <= row, 1.0, 0.0).astype(BF16)
    ones_rows = jnp.ones((SUBLANES, 2 * QB), BF16)

    def attn_pair(jp, taken):
        taken = list(taken)
        lgs = [[] for _ in seqs]
        for u in range(2):
            j = 2 * jp + u
            r0 = pl.multiple_of(j * QB, QB)
            vis = (j * QB + row) <= (i * QB + col)
            bias = bias_ref[jnp.clip(i - j, 0, 2)]
            keys = [key_ref[s, pl.ds(r0, QB), :] for s in seqs]
            eqs = each(lambda k_, t_: k_ == t_, keys, thr)
            prefix = each(lambda e_: jnp.dot(tri, jnp.where(e_, 1.0, 0.0).astype(BF16), preferred_element_type=F32), eqs)
            lg = [jnp.dot(k_ref[s, pl.ds(r0, QB), :].astype(BF16), qs_t[s], preferred_element_type=F32) for s in seqs]
            for s in seqs:
                sel = jnp.logical_or(keys[s] > thr[s], jnp.logical_and(eqs[s], taken[s] + prefix[s] <= need[s]))
                taken[s] = taken[s] + prefix[s][QB - 1:QB, :]
                madd = jnp.where(jnp.logical_and(sel, vis), 0.0, NEG_INF)
                lgs[s].append(lg[s] + bias + jnp.concatenate([madd] * ATT_HEADS, axis=1))
        m_old = [m_ref[s] for s in seqs]
        m_new = each(lambda mo, l2: jnp.maximum(mo, jnp.maximum(jnp.max(l2[0], axis=0, keepdims=True),
                                                                jnp.max(l2[1], axis=0, keepdims=True))), m_old, lgs)
        p = each(lambda l2, mn: jnp.concatenate([jnp.exp2(l_ - mn).astype(BF16) for l_ in l2], axis=0), lgs, m_new)
        rows2 = pl.ds(pl.multiple_of(jp * 2 * QB, 2 * QB), 2 * QB)
        v_aug = [jnp.concatenate([v_ref[s, rows2, :].T.astype(BF16), ones_rows], axis=0) for s in seqs]
        pv = each(lambda va, p_: jnp.dot(va, p_, preferred_element_type=F32), v_aug, p)
        for s in seqs:
            acc_ref[s] = jnp.exp2(m_old[s] - m_new[s]) * acc_ref[s] + pv[s]
            m_ref[s] = m_new[s]
        return tuple(taken)

    lax.fori_loop(0, (i + 2) // 2, attn_pair, tuple(jnp.zeros((1, QB), F32) for _ in seqs))

    for s in seqs:
        acc = acc_ref[s]
        out_t = (acc[:LANES] / acc[LANES:LANES + 1]).T
        o_ref[s] = jnp.concatenate(
            [out_t[h * QB:(h + 1) * QB, (h // rep) * HD:(h // rep + 1) * HD] for h in range(ATT_HEADS)], axis=1)


def _dsa_prompt(c_att, bias_t):
    b, t, _ = c_att.shape
    nq = t // Q_BLOCK
    assert nq % DSA_GROUP == 0, (t, Q_BLOCK, DSA_GROUP)
    t_keys =((nq + DSA_GROUP - 1) // DSA_GROUP) * DSA_GROUP * Q_BLOCK
    ns = _row_tile(b, DSA_SEQS)
    blk = lambda w, cb: pl.BlockSpec((ns, Q_BLOCK, w), lambda bi, i: (bi, i, cb))
    allk = lambda cb: pl.BlockSpec((ns, t, LANES), lambda bi, i: (bi, 0, cb))
    return pl.pallas_call(
        functools.partial(_dsa_prompt_kernel, topk=min(TOPK, t // 4)),
        grid=(b // ns, nq),
        in_specs=[blk(512, ATT_Q0 // 512), blk(256, ATT_QI0 // 256), blk(LANES, ATT_KI0 // LANES),
                  allk(ATT_K0 // LANES), allk(ATT_V0 // LANES), allk(ATT_KI0 // LANES), _full(bias_t.shape)],
        out_specs=pl.BlockSpec((ns, Q_BLOCK, ATT_HEADS * ATT_HEAD_DIM), lambda bi, i: (bi, i, 0)),
        out_shape=jax.ShapeDtypeStruct((b, t, ATT_HEADS * ATT_HEAD_DIM), F32),
        scratch_shapes=[pltpu.VMEM((ns, t_keys, Q_BLOCK), I32), pltpu.VMEM((ns, 1, ATT_HEADS * Q_BLOCK), F32),
                        pltpu.VMEM((ns, LANES + SUBLANES, ATT_HEADS * Q_BLOCK), F32)],
        compiler_params=_cparams(("parallel", "arbitrary")),
        name="dsa_prompt",
    )(c_att, c_att, c_att, c_att, c_att, c_att, bias_t)


def _dsa_sample_kernel(pt_ref, l_ref, kidx_hbm, k_hbm, v_hbm, c_ref, bias_ref, o_ref,
                       ki_buf, k_buf, v_buf, sems, key_ref, madd_ref, *, n_pages, tq, topk):
    b = pl.program_id(0)
    slot = b % 2
    layer = l_ref[0]
    HD = ATT_HEAD_DIM
    rep = ATT_HEADS // ATT_KV_HEADS
    n_past = n_pages * PAGE

    def page_copies(page, p, sl):
        cols = pl.ds(pl.multiple_of(p * PAGE, PAGE), PAGE)
        return (pltpu.make_async_copy(kidx_hbm.at[layer, page], ki_buf.at[sl, :, cols], sems.at[0, sl]),
                pltpu.make_async_copy(k_hbm.at[layer, page], k_buf.at[sl, :, cols], sems.at[1, sl]),
                pltpu.make_async_copy(v_hbm.at[layer, page], v_buf.at[sl, :, cols], sems.at[2, sl]))

    def start_gather(bi, sl):
        def body(p, carry):
            for cp in page_copies(pt_ref[bi * n_pages + p], p, sl):
                cp.start()
            return carry
        lax.fori_loop(0, n_pages, body, 0)

    def wait_gather(sl):
        def body(p, carry):
            for cp in page_copies(0, p, sl):
                cp.wait()
            return carry
        lax.fori_loop(0, n_pages, body, 0)

    @pl.when(b == 0)
    def _():
        start_gather(0, 0)

    @pl.when(b + 1 < pl.num_programs(0))
    def _():
        start_gather(b + 1, 1 - slot)

    wait_gather(slot)

    qi = c_ref[0, :, ATT_QI0:ATT_QI0 + IDX_HEADS * IDX_DIM] * IDX_DIM ** -0.5
    wi = _bf16_round(c_ref[0, :, ATT_WI0:ATT_WI0 + IDX_HEADS] * IDX_HEADS ** -0.5)
    qi_rows = jnp.concatenate([qi[:, h * IDX_DIM:(h + 1) * IDX_DIM] for h in range(IDX_HEADS)], axis=0)

    def scores(dots):
        d = _bf16_round(jnp.maximum(dots, 0.0))
        s = jnp.zeros((tq, dots.shape[1]), F32)
        for h in range(IDX_HEADS):
            s = s + d[h * tq:(h + 1) * tq] * wi[:, h:h + 1]
        return s

    causal_new = lax.broadcasted_iota(I32, (tq, PAGE), 1) <= lax.broadcasted_iota(I32, (tq, PAGE), 0)
    key_ref[:, 0:n_past] = _ordered_key(scores(_mm(qi_rows, ki_buf[slot])))
    ki_new = jnp.concatenate([c_ref[0, :, ATT_KI0:ATT_KI0 + IDX_DIM], jnp.zeros((PAGE - tq, IDX_DIM), F32)], axis=0)
    key_ref[:, n_past:n_past + PAGE] = _ordered_key(jnp.where(causal_new, scores(_mm_nt(qi_rows, ki_new)), NEG_INF))
    keys = key_ref[...]
    count_ge = lambda cand: jnp.sum(jnp.where(keys >= cand, 1, 0), axis=1, keepdims=True)
    thr = _kth_largest(count_ge, (tq, 1), topk, two_bits=True)
    excess =jnp.max(count_ge(thr) - topk)

    @pl.when(excess == 0)
    def _():
        madd_ref[...] = jnp.where(keys >= thr, 0.0, NEG_INF)

    @pl.when(excess > 0)
    def _():
        need = (topk - count_ge(thr + 1)).astype(F32)
        tri = jnp.where(lax.broadcasted_iota(I32, (PAGE, PAGE), 0) <= lax.broadcasted_iota(I32, (PAGE, PAGE), 1),
                        1.0, 0.0).astype(BF16)

        def blk(jb, taken):
            cols = pl.ds(pl.multiple_of(jb * PAGE, PAGE), PAGE)
            key = key_ref[:, cols]
            eq = key == thr
            prefix = jnp.dot(jnp.where(eq, 1.0, 0.0).astype(BF16), tri, preferred_element_type=F32)
            sel = jnp.logical_or(key > thr, jnp.logical_and(eq, taken + prefix <= need))
            madd_ref[:, cols] = jnp.where(sel, 0.0, NEG_INF)
            return taken + prefix[:, PAGE - 1:PAGE]

        lax.fori_loop(0, n_pages + 1, blk, jnp.zeros((tq, 1), F32))

    q = c_ref[0, :, ATT_Q0:ATT_Q0 + ATT_HEADS * HD] * HD ** -0.5
    z = jnp.zeros((tq, HD), F32)
    qs = jnp.concatenate([jnp.concatenate([q[:, h * HD:(h + 1) * HD], z] if h // rep == 0
                                          else [z, q[:, h * HD:(h + 1) * HD]], axis=1)
                          for h in range(ATT_HEADS)], axis=0)
    pad = jnp.zeros((PAGE - tq, LANES), F32)
    k_new = jnp.concatenate([c_ref[0, :, ATT_K0:ATT_K0 + LANES], pad], axis=0)
    v_new = jnp.concatenate([c_ref[0, :, ATT_V0:ATT_V0 + LANES], pad], axis=0)
    lg_past = _mm(qs, k_buf[slot])
    far = bias_ref[2][:, 0:1]
    madd_ref[:, n_past:n_past + PAGE] = jnp.where(causal_new, madd_ref[:, n_past:n_past + PAGE], NEG_INF)
    madd = madd_ref[...]
    lg = jnp.concatenate([lg_past[:, :n_past - PAGE] + far, lg_past[:, n_past - PAGE:] + bias_ref[1],
                          _mm_nt(qs, k_new) + bias_ref[0]], axis=1) + jnp.concatenate([madd] * ATT_HEADS, axis=0)
    pr = jnp.exp(lg - jnp.max(lg, axis=-1, keepdims=True))
    out = (_mm_nt(pr[:, :n_past], v_buf[slot]) + _mm(pr[:, n_past:], v_new)) / jnp.sum(pr, axis=-1, keepdims=True)
    o_ref[0] = jnp.concatenate(
        [out[h * tq:(h + 1) * tq, (h // rep) * HD:(h // rep + 1) * HD] for h in range(ATT_HEADS)], axis=1)


def _dsa_sample(c_att, t_valid, layer, page_table, kidx_pool, k_pool, v_pool, bias_s):
    b, tq, _ = c_att.shape
    n_pages = page_table.shape[1]
    pt = page_table.reshape(-1)
    n_past = n_pages * PAGE
    n_keys = n_past + PAGE
    hbm = pl.BlockSpec(memory_space=pl.ANY)
    return pl.pallas_call(
        functools.partial(_dsa_sample_kernel, n_pages=n_pages, tq=tq, topk=min(TOPK, (n_past + t_valid) // 4)),
        grid_spec=pltpu.PrefetchScalarGridSpec(
            num_scalar_prefetch=2,
            grid=(b,),
            in_specs=[hbm, hbm, hbm, pl.BlockSpec((1, tq, ATT_W), lambda bi, pt_ref, l_ref: (bi, 0, 0)),
                      pl.BlockSpec(bias_s.shape, lambda bi, pt_ref, l_ref: (0, 0, 0))],
            out_specs=pl.BlockSpec((1, tq, ATT_HEADS * ATT_HEAD_DIM), lambda bi, pt_ref, l_ref: (bi, 0, 0)),
            scratch_shapes=[pltpu.VMEM((2, IDX_DIM, n_past), F32), pltpu.VMEM((2, LANES, n_past), F32),
                            pltpu.VMEM((2, LANES, n_past), F32), pltpu.SemaphoreType.DMA((3, 2)),
                            pltpu.VMEM((tq, n_keys), I32), pltpu.VMEM((tq, n_keys), F32)],
        ),
        out_shape=jax.ShapeDtypeStruct((b, tq, ATT_HEADS * ATT_HEAD_DIM), F32),
        compiler_params=_cparams(("arbitrary",)),
        name="dsa_sample",
    )(pt, layer, kidx_pool, k_pool, v_pool, c_att, bias_s)


def _t5_bucket(dist):
    n = jnp.maximum(dist, 0)
    max_exact = NUM_BUCKETS // 2
    nf = jnp.maximum(n, 1).astype(F32)
    large = max_exact + (jnp.log(nf / max_exact) / math.log(MAX_DISTANCE / max_exact)
                         * (NUM_BUCKETS - max_exact)).astype(I32)
    large = jnp.minimum(large, NUM_BUCKETS - 1)
    return jnp.where(n < max_exact, n, large)


def _bias_tiles(rel_bias):
    r = jnp.arange(Q_BLOCK)
    d0 = r[:, None] - r[None, :]
    dist = jnp.stack([d0, d0 + Q_BLOCK, jnp.full_like(d0, 2 * Q_BLOCK)])
    onehot = (_t5_bucket(dist)[..., None] == jnp.arange(NUM_BUCKETS)).astype(F32)
    return jnp.einsum("cqkn,nh->chqk", onehot, rel_bias.astype(F32), precision=lax.Precision.HIGHEST)


def _group_layer(x, w, lb, mem, mem_blocks, shift, s_rwkv0, s_hgrn0, attend, cfg):
    b, t, d = x.shape
    n = b * t
    tm, t_valid = cfg["tm"], cfg["t_valid"]
    x2 = x.reshape(n, d)
    x2 = _ffn(x2, w["ffn1_norm_pre"], w["ffn1_w_up"], w["ffn1_w_down"], w["ffn1_norm_post"], cfg["tm_ffn"])
    c_rw, c_att, c_hg, gates = _proj(x2, w["mix_norm_pre"], [w["w_in_rw"], w["w_in_att"], w["w_in_hg"], w["w_in_gate"]],
                                     True, tm, "mix_in")
    c_rw = c_rw.reshape(b, t, -1)
    c_att = c_att.reshape(b, t, -1)
    streams = _rwkv_prep(c_rw, shift, w, t_valid, cfg["tm_prep"])
    o_a, s_rwkv = _rwkv(streams, s_rwkv0, w["rwkv_ln_w"], w["rwkv_ln_b"], w["rwkv_r_k"], cfg["rwkv_chunk"],
                         cfg["rwkv_seqs"])
    o_b = attend(c_att)
    o_c, s_hgrn = _hgrn(c_hg.reshape(b, t, -1), lb, w["hgrn_norm_w"], s_hgrn0, cfg["hgrn_chunk"], t_valid, cfg["hgrn_tb"])
    x2 = _merge(x2, o_a.reshape(n, -1), o_b.reshape(n, -1), o_c.reshape(n, -1), gates,
                w["mix_w_proj_a"], w["mix_w_proj_b"], w["mix_w_proj_c"], w["mix_w_out"], w["mix_norm_post"], tm)
    (q,) = _proj(x2, w["cross_norm_pre"], [w["cross_wq"]], True, tm, "cross_q")
    o = _xattn(q.reshape(b, t, d), mem[0], mem[1], mem_blocks[0], mem_blocks[1], cfg["tq_x"])
    x2 = _out_proj(x2, o.reshape(n, d), w["cross_wo"], w["cross_norm_post"], tm)
    x2 = _ffn(x2, w["ffn2_norm_pre"], w["ffn2_w_up"], w["ffn2_w_down"], w["ffn2_norm_post"], cfg["tm_ffn"])
    nkv = ATT_KV_HEADS * ATT_HEAD_DIM
    state = (c_rw[:, t_valid - 1], s_rwkv, s_hgrn,
             c_att[:, :t_valid, ATT_K0:ATT_K0 + nkv].reshape(b, t_valid, ATT_KV_HEADS, ATT_HEAD_DIM),
             c_att[:, :t_valid, ATT_V0:ATT_V0 + nkv].reshape(b, t_valid, ATT_KV_HEADS, ATT_HEAD_DIM),
             c_att[:, :t_valid, ATT_KI0:ATT_KI0 + IDX_DIM])
    return x2.reshape(b, t, d), state


PROMPT_CFG = dict(tm=512, tm_ffn=512, tm_prep=256, rwkv_chunk=64, rwkv_seqs=4, hgrn_chunk=16, hgrn_tb=256, tq_x=512)
SAMPLE_PAD = 8
SAMPLE_CFG = dict(tm=256, tm_ffn=256, tm_prep=SAMPLE_PAD, rwkv_chunk=SAMPLE_PAD, rwkv_seqs=4, hgrn_chunk=SAMPLE_PAD,
                  hgrn_tb=SAMPLE_PAD, tq_x=SAMPLE_PAD)


def kernel(x_prompt, x_sample, cache_k, cache_v, cache_kidx, cache_mem_k, cache_mem_v, state_rwkv, state_rwkv_shift, state_hgrn, page_table, mem_prompt, rel_bias, hgrn_lb_logits, ffn1_norm_pre, ffn1_norm_post, ffn1_w_up, ffn1_w_down, mix_norm_pre, mix_norm_post, mix_w_in, rwkv_mu, rwkv_w0, rwkv_w_up, rwkv_a0, rwkv_a_up, rwkv_g_up, rwkv_k_k, rwkv_k_a, rwkv_r_k, rwkv_ln_w, rwkv_ln_b, hgrn_norm_w, mix_w_proj_a, mix_w_proj_b, mix_w_proj_c, mix_w_out, cross_norm_pre, cross_norm_post, cross_wq, cross_wk, cross_wv, cross_wo, ffn2_norm_pre, ffn2_norm_post, ffn2_w_up, ffn2_w_down):
    depth, d = ffn1_norm_pre.shape
    bp, tp, _ = x_prompt.shape
    bs, ts, _ = x_sample.shape
    d_r = rwkv_w0.shape[1]
    rw_cols = rwkv_mu.shape[1]
    d_h = hgrn_norm_w.shape[1]
    n_att = ATT_WI0 + IDX_HEADS - ATT_Q0
    bf = lambda a: a.astype(BF16)
    vec = lambda a: a[:, None, :]

    o1 = rw_cols + n_att
    w_att = jnp.pad(mix_w_in[:, :, rw_cols:o1], ((0, 0), (0, 0), (0, ATT_W - n_att)))
    head_sum = (np.arange(d_r)[:, None] // RWKV_HEAD == np.arange(d_r)[None, :] // RWKV_HEAD).astype(np.float32)
    per_head = lambda a: a.reshape(depth, RWKV_HEADS, 1, RWKV_HEAD)
    weights = dict(
        ffn1_norm_pre=vec(ffn1_norm_pre), ffn1_norm_post=vec(ffn1_norm_post), ffn1_w_up=bf(ffn1_w_up), ffn1_w_down=bf(ffn1_w_down),
        ffn2_norm_pre=vec(ffn2_norm_pre), ffn2_norm_post=vec(ffn2_norm_post), ffn2_w_up=bf(ffn2_w_up), ffn2_w_down=bf(ffn2_w_down),
        mix_norm_pre=vec(mix_norm_pre), mix_norm_post=vec(mix_norm_post),
        w_in_rw=bf(mix_w_in[:, :, :rw_cols]), w_in_att=bf(w_att),
        w_in_hg=bf(mix_w_in[:, :, o1:o1 + 4 * d_h]), w_in_gate=bf(mix_w_in[:, :, o1 + 4 * d_h:]),
        rwkv_mu=vec(rwkv_mu), rwkv_w0=vec(rwkv_w0), rwkv_w_up=bf(rwkv_w_up), rwkv_a0=vec(rwkv_a0), rwkv_a_up=bf(rwkv_a_up),
        rwkv_g_up=bf(rwkv_g_up), rwkv_k_k=vec(rwkv_k_k), rwkv_k_a=vec(rwkv_k_a),
        rwkv_r_k=rwkv_r_k[:, :, None, :], rwkv_ln_w=per_head(rwkv_ln_w), rwkv_ln_b=per_head(rwkv_ln_b),
        hgrn_norm_w=vec(hgrn_norm_w),
        mix_w_proj_a=bf(mix_w_proj_a), mix_w_proj_b=bf(mix_w_proj_b), mix_w_proj_c=bf(mix_w_proj_c), mix_w_out=bf(mix_w_out),
        cross_norm_pre=vec(cross_norm_pre), cross_norm_post=vec(cross_norm_post), cross_wq=bf(cross_wq), cross_wo=bf(cross_wo),
        cross_wkv=bf(jnp.concatenate([cross_wk, cross_wv], axis=-1)),
    )
    p_lb = jax.nn.softmax(hgrn_lb_logits.astype(F32), axis=0)
    lower_bounds = vec(jnp.cumsum(p_lb, axis=0) - p_lb[0:1])

    bias = _bias_tiles(rel_bias)
    bias_t = jnp.transpose(bias * LOG2E, (0, 3, 1, 2)).reshape(3, Q_BLOCK, ATT_HEADS * Q_BLOCK)
    bias_s = bias[:, :, :SAMPLE_PAD, :].reshape(3, ATT_HEADS * SAMPLE_PAD, Q_BLOCK)

    pad_t = SAMPLE_PAD - ts
    xs0 = jnp.pad(x_sample, ((0, 0), (0, pad_t), (0, 0)))
    n_pool = cache_k.shape[1]
    nkv = ATT_KV_HEADS * ATT_HEAD_DIM
    k_pool = jnp.transpose(cache_k, (0, 1, 3, 4, 2)).reshape(depth, n_pool, nkv, PAGE)
    v_pool = jnp.transpose(cache_v, (0, 1, 3, 4, 2)).reshape(depth, n_pool, nkv, PAGE)
    kidx_pool = jnp.swapaxes(cache_kidx, 2, 3)
    mem_tokens = mem_prompt.shape[1]
    mem2 = mem_prompt.reshape(bp * mem_tokens, d)
    ones_d = jnp.ones((1, d), F32)
    zero_shift = jnp.zeros((bp, 1, rw_cols), F32)
    zero_rwkv = jnp.zeros((bp, RWKV_HEADS, RWKV_HEAD, RWKV_HEAD), F32)
    zero_hgrn = jnp.zeros((bp, HGRN_HEADS, HGRN_EXPAND, HGRN_EXPAND), F32)
    prompt_cfg = dict(PROMPT_CFG, t_valid=tp)
    sample_cfg = dict(SAMPLE_CFG, t_valid=ts)

    def layer(carry, per_layer):
        xp, xs = carry
        w, lb, mem_ks, mem_vs, s_rw, s_sh, s_hg, li = per_layer
        w = dict(w, head_sum=jnp.asarray(head_sum, BF16))
        (mem_kv,) = _proj(mem2, ones_d, [w["cross_wkv"]], False, 256, "mem_kv")
        mem_kv = mem_kv.reshape(bp, mem_tokens, 2 * d)
        xp, st_p = _group_layer(xp, w, lb, (mem_kv, mem_kv), (0, 1), zero_shift, zero_rwkv, zero_hgrn,
                                lambda c: _dsa_prompt(c, bias_t), prompt_cfg)
        attend_s = lambda c: _dsa_sample(c, ts, li, page_table, kidx_pool, k_pool, v_pool, bias_s)
        xs, st_s = _group_layer(xs, w, lb, (mem_ks, mem_vs), (0, 0), s_sh, s_rw, s_hg, attend_s, sample_cfg)
        mem_k = mem_kv[:, :, :d].reshape(bp, mem_tokens, MEM_HEADS, d // MEM_HEADS)
        mem_v = mem_kv[:, :, d:].reshape(bp, mem_tokens, MEM_HEADS, d // MEM_HEADS)
        return (xp, xs), (st_p, (mem_k, mem_v), st_s)

    per_layer = (weights, lower_bounds,
                 cache_mem_k.reshape(depth, bs, mem_tokens, d), cache_mem_v.reshape(depth, bs, mem_tokens, d),
                 state_rwkv, state_rwkv_shift[:, :, None, :], state_hgrn,
                 jnp.arange(depth, dtype=I32)[:, None])
    (xp, xs), (st_p, (mem_k, mem_v), st_s) = lax.scan(layer, (x_prompt, xs0), per_layer)
    sh_p, rw_p, hg_p, k_p, v_p, ki_p = st_p
    sh_s, rw_s, hg_s, k_s, v_s, ki_s = st_s
    return (xp, xs[:, :ts], k_p, v_p, ki_p, mem_k, mem_v, rw_p, sh_p, hg_p, k_s, v_s, ki_s, rw_s, sh_s, hg_s)
```

```python
import functools
import math

import jax
import jax.numpy as jnp
import numpy as np
from jax import lax
from jax.experimental import pallas as pl
from jax.experimental.pallas import tpu as pltpu

F32 = jnp.float32
BF16 = jnp.bfloat16
I32 = jnp.int32

LANES = 128
SUBLANES = 8
VMEM_LIMIT_BYTES = 56 * 1024 * 1024

NORM_EPS = 1e-6
RWKV_LN_EPS = 64e-5
LB_FLOOR = 1e-30
NEG_INF = -1e30
INT_MIN = int(np.iinfo(np.int32).min)
LOG2E = math.log2(math.e)

RWKV_HEAD = 64
RWKV_HEADS = 8
RWKV_GROUP = 4
HGRN_HEADS = 4
HGRN_EXPAND = 128
ATT_HEADS = 8
ATT_KV_HEADS = 2
ATT_HEAD_DIM = 64
IDX_HEADS = 4
IDX_DIM = 64
TOPK = 256
Q_BLOCK = 128
MEM_HEADS = 4
NUM_BUCKETS = 32
MAX_DISTANCE = 128
PAGE = 128

ATT_Q0, ATT_K0, ATT_V0, ATT_QI0, ATT_KI0, ATT_WI0, ATT_W = 0, 512, 640, 768, 1024, 1088, 1152


def _cparams(sem):
    return pltpu.CompilerParams(dimension_semantics=sem, vmem_limit_bytes=VMEM_LIMIT_BYTES)


def _mm(a, b):
    return jnp.dot(a.astype(BF16), b.astype(BF16), preferred_element_type=F32)


def _mm_nt(a, b):
    return lax.dot_general(a.astype(BF16), b.astype(BF16), (((1,), (1,)), ((), ())),
                           preferred_element_type=F32)


def _mm_tn(a, b):
    return lax.dot_general(a.astype(BF16), b.astype(BF16), (((0,), (0,)), ((), ())),
                           preferred_element_type=F32)


def _bf16_round(x):
    return x.astype(BF16).astype(F32)


def _split3(x):
    hi = x.astype(BF16)
    r1 = x - hi.astype(F32)
    mid = r1.astype(BF16)
    lo = (r1 - mid.astype(F32)).astype(BF16)
    return hi, mid, lo


def _mm_exact_lhs(a01, x):
    a = a01.astype(BF16)
    hi, mid, lo = _split3(x)
    d = lambda p: jnp.dot(a, p, preferred_element_type=F32)
    return d(hi) + d(mid) + d(lo)


def _mm_exact_rhs(x, b01):
    b = b01.astype(BF16)
    hi, mid, lo = _split3(x)
    d = lambda p: jnp.dot(p, b, preferred_element_type=F32)
    return d(hi) + d(mid) + d(lo)


def _rms(x, g):
    return x * lax.rsqrt(jnp.mean(x * x, axis=-1, keepdims=True) + NORM_EPS) * g


def _sigmoid(x):
    return jax.nn.sigmoid(x)


def _softplus(x):
    return jnp.maximum(x, 0.0) + jnp.log1p(jnp.exp(-jnp.abs(x)))


def _full(shape):
    nd = len(shape)
    return pl.BlockSpec(shape, lambda *_: (0,) * nd)


def _row_tile(n, want):
    t = min(n, want)
    assert n % t == 0, (n, t)
    return t


def _ffn_kernel(x_ref, gpre_ref, wup_ref, wdn_ref, gpost_ref, o_ref, *, d_ff, tf):
    x = x_ref[...]
    h = _rms(x, gpre_ref[...]).astype(BF16)
    acc = jnp.zeros(x.shape, F32)
    for f0 in range(0, d_ff, tf):
        gate = jnp.dot(h, wup_ref[:, f0:f0 + tf], preferred_element_type=F32)
        up = jnp.dot(h, wup_ref[:, d_ff + f0:d_ff + f0 + tf], preferred_element_type=F32)
        act = (gate * _sigmoid(gate) * up).astype(BF16)
        acc = acc + jnp.dot(act, wdn_ref[f0:f0 + tf, :], preferred_element_type=F32)
    o_ref[...] = x + 0.5 * _rms(acc, gpost_ref[...])


def _ffn(x, gpre, wup, wdn, gpost, tm):
    n, d = x.shape
    d_ff = wdn.shape[0]
    tf = d_ff // 2 if (d_ff // 2) % LANES == 0 else d_ff
    tm = _row_tile(n, tm)
    return pl.pallas_call(
        functools.partial(_ffn_kernel, d_ff=d_ff, tf=tf),
        grid=(n // tm,),
        in_specs=[pl.BlockSpec((tm, d), lambda i: (i, 0)), _full((1, d)), _full(wup.shape), _full(wdn.shape),
                  _full((1, d))],
        out_specs=pl.BlockSpec((tm, d), lambda i: (i, 0)),
        out_shape=jax.ShapeDtypeStruct((n, d), F32),
        compiler_params=_cparams(("parallel",)),
        name="ffn",
    )(x, gpre, wup, wdn, gpost)


def _proj_kernel(*refs, n_out, norm, tn):
    x_ref, g_ref = refs[0], refs[1]
    w_refs = refs[2:2 + n_out]
    o_refs = refs[2 + n_out:]
    x = x_ref[...]
    h = (_rms(x, g_ref[...]) if norm else x).astype(BF16)
    for w_ref, o_ref in zip(w_refs, o_refs):
        n = w_ref.shape[1]
        step = tn if n % tn == 0 else n
        for n0 in range(0, n, step):
            o_ref[:, n0:n0 + step] = jnp.dot(h, w_ref[:, n0:n0 + step], preferred_element_type=F32)


def _proj(x, g, ws, norm, tm, name):
    n, d = x.shape
    tm = _row_tile(n, tm)
    return pl.pallas_call(
        functools.partial(_proj_kernel, n_out=len(ws), norm=norm, tn=512),
        grid=(n // tm,),
        in_specs=[pl.BlockSpec((tm, d), lambda i: (i, 0)), _full((1, d))] + [_full(w.shape) for w in ws],
        out_specs=[pl.BlockSpec((tm, w.shape[1]), lambda i: (i, 0)) for w in ws],
        out_shape=[jax.ShapeDtypeStruct((n, w.shape[1]), F32) for w in ws],
        compiler_params=_cparams(("parallel",)),
        name=name,
    )(x, g, *ws)


def _out_kernel(x_ref, a_ref, w_ref, g_ref, o_ref):
    y = jnp.dot(a_ref[...].astype(BF16), w_ref[...], preferred_element_type=F32)
    o_ref[...] = x_ref[...] + _rms(y, g_ref[...])


def _out_proj(x, a, w, g, tm):
    n, d = x.shape
    tm = _row_tile(n, tm)
    return pl.pallas_call(
        _out_kernel,
        grid=(n // tm,),
        in_specs=[pl.BlockSpec((tm, d), lambda i: (i, 0)), pl.BlockSpec((tm, a.shape[1]), lambda i: (i, 0)),
                  _full(w.shape), _full((1, d))],
        out_specs=pl.BlockSpec((tm, d), lambda i: (i, 0)),
        out_shape=jax.ShapeDtypeStruct((n, d), F32),
        compiler_params=_cparams(("parallel",)),
        name="out_proj",
    )(x, a, w, g)


def _merge_kernel(x_ref, oa_ref, ob_ref, oc_ref, gt_ref, wa_ref, wb_ref, wc_ref, wo_ref, g_ref, o_ref, *, d):
    m = jnp.zeros((x_ref.shape[0], d), F32)
    for j, (o_r, w_r) in enumerate(((oa_ref, wa_ref), (ob_ref, wb_ref), (oc_ref, wc_ref))):
        p = jnp.dot(o_r[...].astype(BF16), w_r[...], preferred_element_type=F32)
        m = m + _sigmoid(gt_ref[:, j * d:(j + 1) * d]) * p
    y = jnp.dot(m.astype(BF16), wo_ref[...], preferred_element_type=F32)
    o_ref[...] = x_ref[...] + _rms(y, g_ref[...])


def _merge(x, oa, ob, oc, gates, wa, wb, wc, wo, g, tm):
    n, d = x.shape
    tm = _row_tile(n, tm)
    row = lambda w: pl.BlockSpec((tm, w), lambda i: (i, 0))
    return pl.pallas_call(
        functools.partial(_merge_kernel, d=d),
        grid=(n // tm,),
        in_specs=[row(d), row(oa.shape[1]), row(ob.shape[1]), row(oc.shape[1]), row(gates.shape[1]),
                  _full(wa.shape), _full(wb.shape), _full(wc.shape), _full(wo.shape), _full((1, d))],
        out_specs=row(d),
        out_shape=jax.ShapeDtypeStruct((n, d), F32),
        compiler_params=_cparams(("parallel",)),
        name="merge",
    )(x, oa, ob, oc, gates, wa, wb, wc, wo, g)


def _xattn_kernel(q_ref, mk_ref, mv_ref, o_ref, *, heads):
    hd = q_ref.shape[2] // heads
    for h in range(heads):
        sl = slice(h * hd, (h + 1) * hd)
        lg = _mm_nt(q_ref[0, :, sl], mk_ref[0, :, sl]) * hd ** -0.5
        lg = lg - jnp.max(lg, axis=-1, keepdims=True)
        p = jnp.exp(lg)
        p = p / jnp.sum(p, axis=-1, keepdims=True)
        o_ref[0, :, sl] = _mm(p, mv_ref[0, :, sl])


def _xattn(q, mem_k, mem_v, k_blk, v_blk, tq):
    b, t, d = q.shape
    s = mem_k.shape[1]
    tq = _row_tile(t, tq)
    mem_specs = [pl.BlockSpec((1, s, d), lambda i, j: (i, 0, k_blk)),
                 pl.BlockSpec((1, s, d), lambda i, j: (i, 0, v_blk))]
    return pl.pallas_call(
        functools.partial(_xattn_kernel, heads=MEM_HEADS),
        grid=(b, t // tq),
        in_specs=[pl.BlockSpec((1, tq, d), lambda i, j: (i, j, 0))] + mem_specs,
        out_specs=pl.BlockSpec((1, tq, d), lambda i, j: (i, j, 0)),
        out_shape=jax.ShapeDtypeStruct((b, t, d), F32),
        compiler_params=_cparams(("parallel", "parallel")),
        name="xattn",
    )(q, mem_k, mem_v)


def _rwkv_prep_kernel(c_ref, sh_ref, mu_ref, w0_ref, wup_ref, a0_ref, aup_ref, gup_ref, kk_ref, ka_ref, hs_ref,
                      r_o, lw_o, k_o, v_o, kk_o, b_o, g_o, carry_ref, *, t_valid, d_r):
    j = pl.program_id(1)
    tm = c_ref.shape[1]

    @pl.when(j == 0)
    def _():
        carry_ref[...] = sh_ref[0]

    c = c_ref[0]
    row = lax.broadcasted_iota(I32, c.shape, 0)
    prev = jnp.where(row == 0, carry_ref[...], pltpu.roll(c, 1, axis=0))
    carry_ref[...] = c[tm - 1:tm, :]
    csh = c + (prev - c) * mu_ref[...]
    r, k, v = csh[:, 0:d_r], csh[:, d_r:2 * d_r], csh[:, 2 * d_r:3 * d_r]
    o = 3 * d_r
    n_w, n_a, n_g = wup_ref.shape[0], aup_ref.shape[0], gup_ref.shape[0]
    xw, xa, xg = csh[:, o:o + n_w], csh[:, o + n_w:o + n_w + n_a], csh[:, o + n_w + n_a:o + n_w + n_a + n_g]
    w_log = -_softplus(-(w0_ref[...] + _mm(jnp.tanh(xw), wup_ref[...]))) - 0.5
    lw = -jnp.exp(w_log)
    a_lr = _sigmoid(a0_ref[...] + _mm(xa, aup_ref[...]))
    g = _mm(_sigmoid(xg), gup_ref[...])
    kk = k * kk_ref[...]
    ss = _mm_exact_rhs(kk * kk, hs_ref[...])
    kk = kk / jnp.maximum(jnp.sqrt(ss), 1e-12)
    k_mod = k * (1.0 + (a_lr - 1.0) * ka_ref[...])
    b = kk * a_lr
    if t_valid < tm:
        ok = (lax.broadcasted_iota(I32, (tm, d_r), 0) + j * tm) < t_valid
        z = lambda t: jnp.where(ok, t, 0.0)
        lw, k_mod, v, kk, b = z(lw), z(k_mod), z(v), z(kk), z(b)
    for h in range(RWKV_HEADS):
        sl = slice(h * RWKV_HEAD, (h + 1) * RWKV_HEAD)
        for o_ref, val in ((r_o, r), (lw_o, lw), (k_o, k_mod), (v_o, v), (kk_o, kk), (b_o, b), (g_o, g)):
            o_ref[0, h] = val[:, sl]


def _rwkv_prep(c_rw, shift, p, t_valid, tm):
    b, t, cols = c_rw.shape
    d_r = p["rwkv_w0"].shape[1]
    tm = _row_tile(t, tm)
    hm = jax.ShapeDtypeStruct((b, RWKV_HEADS, t, RWKV_HEAD), F32)
    hm_spec = pl.BlockSpec((1, RWKV_HEADS, tm, RWKV_HEAD), lambda i, j: (i, 0, j, 0))
    params = [p["rwkv_mu"], p["rwkv_w0"], p["rwkv_w_up"], p["rwkv_a0"], p["rwkv_a_up"], p["rwkv_g_up"],
              p["rwkv_k_k"], p["rwkv_k_a"], p["head_sum"]]
    return pl.pallas_call(
        functools.partial(_rwkv_prep_kernel, t_valid=t_valid, d_r=d_r),
        grid=(b, t // tm),
        in_specs=[pl.BlockSpec((1, tm, cols), lambda i, j: (i, j, 0)),
                  pl.BlockSpec((1, 1, cols), lambda i, j: (i, 0, 0))] + [_full(a.shape) for a in params],
        out_specs=[hm_spec] * 7,
        out_shape=[hm] * 7,
        scratch_shapes=[pltpu.VMEM((1, cols), F32)],
        compiler_params=_cparams(("parallel", "arbitrary")),
        name="rwkv_prep",
    )(c_rw, shift, *params)


def _rwkv_kernel(r_ref, lw_ref, k_ref, v_ref, kk_ref, b_ref, g_ref, s0_ref, lnw_ref, lnb_ref, rk_ref,
                 o_ref, so_ref, s_ref, *, chunk):
    t_idx = pl.program_id(1)
    nh, C, N = RWKV_GROUP, chunk, RWKV_HEAD
    R = nh * C
    n_seq, n_heads = r_ref.shape[0], r_ref.shape[1]

    @pl.when(t_idx == 0)
    def _():
        s_ref[...] = s0_ref[...]

    row = lax.broadcasted_iota(I32, (R, R), 0)
    col = lax.broadcasted_iota(I32, (R, R), 1)
    same = (row // C) == (col // C)
    incl = jnp.logical_and(same, col <= row)
    strict = jnp.logical_and(same, col < row)
    incl01 = jnp.where(incl, 1.0, 0.0)
    eye_r = jnp.where(row == col, 1.0, 0.0)
    levels = []
    m_blk = 1
    while m_blk < C:
        levels.append(jnp.logical_and(row // (2 * m_blk) == col // (2 * m_blk),
                                      jnp.logical_and((row // m_blk) % 2 == 1, (col // m_blk) % 2 == 0)))
        m_blk *= 2
    eye = jnp.where(lax.broadcasted_iota(I32, (N, N), 0) == lax.broadcasted_iota(I32, (N, N), 1), 1.0, 0.0)

    chains = [(bi, h0) for bi in range(n_seq) for h0 in range(0, n_heads, nh)]
    each = lambda f, *cols: [f(*vals) for vals in zip(*cols)]
    ld = lambda ref: [ref[bi, h0:h0 + nh].reshape(R, N) for bi, h0 in chains]
    r, lw, k, v, kk, b, g = (ld(x) for x in (r_ref, lw_ref, k_ref, v_ref, kk_ref, b_ref, g_ref))
    cum = each(lambda t: _mm_exact_lhs(incl01, t), lw)
    e_neg = each(lambda c: jnp.exp(-c), cum)
    at = each(lambda kk_, c, l: -kk_ * jnp.exp(c - l), kk, cum, lw)
    rt = each(lambda r_, c: r_ * jnp.exp(c), r, cum)
    bt = each(lambda b_, e: b_ * e, b, e_neg)
    kt = each(lambda k_, e: k_ * e, k, e_neg)
    gram = each(lambda a_, r_, b_, k_: _mm_nt(jnp.concatenate([a_, r_], axis=0), jnp.concatenate([b_, k_], axis=0)),
                at, rt, bt, kt)
    a_ab = each(lambda gm: jnp.where(strict, gm[:R, :R], 0.0), gram)
    a_ak = each(lambda gm: jnp.where(strict, gm[:R, R:], 0.0), gram)
    a_rb = each(lambda gm: jnp.where(incl, gm[R:, :R], 0.0), gram)
    a_rk = each(lambda gm: jnp.where(incl, gm[R:, R:], 0.0), gram)
    x = each(lambda a_, ak, v_: jnp.concatenate([a_, _mm(ak, v_)], axis=1), at, a_ak, v)
    d_inv = [eye_r for _ in chains]
    for li, lower_left in enumerate(levels):
        a_off = each(lambda ab: jnp.where(lower_left, ab, 0.0), a_ab)
        if li == 0:
            d_inv = each(lambda d, ao: d + ao, d_inv, a_off)
        else:
            half = each(lambda ao, d: _mm(ao, d), a_off, d_inv)
            d_inv = each(lambda d, hf: d + _mm(d, hf), d_inv, half)
    x = each(lambda d, x_: _mm(d, x_), d_inv, x)
    ax = each(lambda rb, x_: _mm(rb, x_), a_rb, x)
    qt = each(lambda r_, ax_: r_ + ax_[:, :N], rt, ax)
    y0 = each(lambda ax_, rk_, v_: ax_[:, N:] + _mm(rk_, v_), ax, a_rk, v)
    for ci, (bi, h0) in enumerate(chains):
        outs = []
        for h in range(nh):
            sl = slice(h * C, (h + 1) * C)
            p_c = jnp.exp(cum[ci][h * C + C - 1:h * C + C, :])
            m = (eye + _mm_tn(x[ci][sl, :N], bt[ci][sl])) * p_c
            s_loc = _mm_tn(jnp.concatenate([x[ci][sl, N:], v[ci][sl]], axis=0),
                           jnp.concatenate([bt[ci][sl], kt[ci][sl]], axis=0)) * p_c
            s = s_ref[bi, h0 + h]
            y = _mm_nt(qt[ci][sl], s) + y0[ci][sl]
            s_ref[bi, h0 + h] = _mm(s, m) + s_loc
            mu = jnp.mean(y, axis=-1, keepdims=True)
            yc = y - mu
            var = jnp.mean(yc * yc, axis=-1, keepdims=True)
            yn = yc * lax.rsqrt(var + RWKV_LN_EPS) * lnw_ref[h0 + h] + lnb_ref[h0 + h]
            yn = yn + jnp.sum(r[ci][sl] * k[ci][sl] * rk_ref[h0 + h], axis=-1, keepdims=True) * v[ci][sl]
            outs.append(yn * g[ci][sl])
        o_ref[bi, :, h0 * N:(h0 + nh) * N] = jnp.concatenate(outs, axis=1)

    @pl.when(t_idx == pl.num_programs(1) - 1)
    def _():
        so_ref[...] = s_ref[...]


def _rwkv(streams, s0, lnw, lnb, rk, chunk, n_seq):
    b, nh, t, n = streams[0].shape
    n_seq = _row_tile(b, n_seq)
    st_spec = pl.BlockSpec((n_seq, nh, chunk, n), lambda i, j: (i, 0, j, 0))
    s_spec = pl.BlockSpec((n_seq, nh, n, n), lambda i, j: (i, 0, 0, 0))
    p_spec = _full((nh, 1, n))
    return pl.pallas_call(
        functools.partial(_rwkv_kernel, chunk=chunk),
        grid=(b // n_seq, t // chunk),
        in_specs=[st_spec] * 7 + [s_spec, p_spec, p_spec, p_spec],
        out_specs=[pl.BlockSpec((n_seq, chunk, nh * n), lambda i, j: (i, j, 0)), s_spec],
        out_shape=[jax.ShapeDtypeStruct((b, t, nh * n), F32), jax.ShapeDtypeStruct((b, nh, n, n), F32)],
        scratch_shapes=[pltpu.VMEM((n_seq, nh, n, n), F32)],
        compiler_params=_cparams(("parallel", "arbitrary")),
        name="rwkv",
    )(*streams, s0, lnw, lnb, rk)


def _hgrn_kernel(c_ref, lb_ref, nw_ref, s0_ref, o_ref, so_ref, st_ref, cum_ref, kh_ref, *, chunk, t_valid):
    j = pl.program_id(1)
    tb = c_ref.shape[1]
    d = lb_ref.shape[1]
    K = HGRN_EXPAND
    c = chunk

    @pl.when(j == 0)
    def _():
        for h in range(HGRN_HEADS):
            st_ref[h] = s0_ref[0, h].T

    z = c_ref[0, :, d:2 * d]
    lb = lb_ref[...]
    ls = -_softplus(-z)
    x1 = jnp.log(jnp.maximum(lb, LB_FLOOR))
    x2 = jnp.log1p(-lb) + ls
    logf = jnp.maximum(x1, x2) + jnp.log1p(jnp.exp(-jnp.abs(x1 - x2)))
    kh = (1.0 - lb) * _sigmoid(-z)
    if t_valid < tb:
        ok = (lax.broadcasted_iota(I32, (tb, d), 0) + j * tb) < t_valid
        logf = jnp.where(ok, logf, 0.0)
        kh = jnp.where(ok, kh, 0.0)
    row = lax.broadcasted_iota(I32, (tb, tb), 0)
    col = lax.broadcasted_iota(I32, (tb, tb), 1)
    tri = jnp.where(jnp.logical_and(row // c == col // c, col <= row), 1.0, 0.0)
    cum_ref[...] = _mm_exact_lhs(tri, logf)
    kh_ref[...] = kh

    ones = jnp.ones((K, K), BF16)
    rr = lax.broadcasted_iota(I32, (c * c, K), 0)
    causal = (rr % c) <= (rr // c)
    sel = jnp.where(lax.broadcasted_iota(I32, (c, c * c), 1) // c == lax.broadcasted_iota(I32, (c, c * c), 0),
                    1.0, 0.0).astype(BF16)

    heads = range(HGRN_HEADS)
    each = lambda f, *cols: [f(*vals) for vals in zip(*cols)]

    def body(ci, carry):
        r0 = pl.multiple_of(ci * c, c)
        rows = pl.ds(r0, c)
        hq = [c_ref[0, rows, h * K:(h + 1) * K] for h in heads]
        q = each(lambda t: t * _sigmoid(t), hq)
        v = [c_ref[0, rows, 2 * d + h * K:2 * d + (h + 1) * K] for h in heads]
        hg = [c_ref[0, rows, 3 * d + h * K:3 * d + (h + 1) * K] for h in heads]
        bq = [cum_ref[rows, h * K:(h + 1) * K] for h in heads]
        kq = [kh_ref[rows, h * K:(h + 1) * K] for h in heads]
        e = each(lambda q_, k_, b_: jnp.concatenate(
            [q_[t:t + 1] * k_ * jnp.exp(jnp.minimum(b_[t:t + 1] - b_, 0.0)) for t in range(c)], axis=0),
            q, kq, bq)
        att = each(lambda e_: jnp.dot(e_.astype(BF16), ones, preferred_element_type=F32), e)
        w = each(lambda a_, v_: jnp.where(causal, a_, 0.0) * jnp.concatenate([v_] * c, axis=0), att, v)
        st = [st_ref[h] for h in heads]
        o = each(lambda w_, q_, b_, s_: jnp.dot(sel, w_.astype(BF16), preferred_element_type=F32)
                 + _mm_nt(q_ * jnp.exp(b_), s_), w, q, bq, st)
        upd = each(lambda s_, b_, v_, k_: s_ * jnp.exp(b_[c - 1:c]) + _mm_tn(v_, k_ * jnp.exp(b_[c - 1:c] - b_)),
                   st, bq, v, kq)
        for h in heads:
            st_ref[h] = upd[h]
            on = o[h] * lax.rsqrt(jnp.mean(o[h] * o[h], axis=-1, keepdims=True) + NORM_EPS)
            o_ref[0, rows, h * K:(h + 1) * K] = on * nw_ref[:, h * K:(h + 1) * K] * (hg[h] * _sigmoid(hg[h]))
        return carry

    lax.fori_loop(0, tb // c, body, 0, unroll=min(4, tb // c))

    @pl.when(j == pl.num_programs(1) - 1)
    def _():
        for h in range(HGRN_HEADS):
            so_ref[0, h] = st_ref[h].T


def _hgrn(c_hg, lb, nw, s0, chunk, t_valid, tb):
    b, t, cols = c_hg.shape
    d = cols // 4
    tb = _row_tile(t, tb)
    s_spec = pl.BlockSpec((1, HGRN_HEADS, HGRN_EXPAND, HGRN_EXPAND), lambda i, j: (i, 0, 0, 0))
    return pl.pallas_call(
        functools.partial(_hgrn_kernel, chunk=chunk, t_valid=t_valid),
        grid=(b, t // tb),
        in_specs=[pl.BlockSpec((1, tb, cols), lambda i, j: (i, j, 0)), _full((1, d)), _full((1, d)), s_spec],
        out_specs=[pl.BlockSpec((1, tb, d), lambda i, j: (i, j, 0)), s_spec],
        out_shape=[jax.ShapeDtypeStruct((b, t, d), F32), jax.ShapeDtypeStruct(s0.shape, F32)],
        scratch_shapes=[pltpu.VMEM((HGRN_HEADS, HGRN_EXPAND, HGRN_EXPAND), F32), pltpu.VMEM((tb, d), F32),
                        pltpu.VMEM((tb, d), F32)],
        compiler_params=_cparams(("parallel", "arbitrary")),
        name="hgrn",
    )(c_hg, lb, nw, s0)


def _ordered_key(s):
    bits = pltpu.bitcast(s + 0.0, I32)
    return bits ^ ((bits >> 31) & 0x7FFFFFFF)


def _kth_largest(count_ge, shape, k, two_bits=False):
    c0 = count_ge(jnp.zeros(shape, I32))
    thr = jnp.where(c0 >= k, 0, INT_MIN).astype(I32)

    def body(i, thr):
        cand = thr | jnp.left_shift(jnp.int32(1), 30 - i)
        return jnp.where(count_ge(cand) >= k, cand, thr)

    if not two_bits:
        return lax.fori_loop(0, 31, body, thr)

    def body2(i, thr):
        hi = jnp.left_shift(jnp.int32(1), 30 - 2 * i)
        lo = jnp.left_shift(jnp.int32(1), 29 - 2 * i)
        c_lo, c_hi, c_both = thr | lo, thr | hi, thr | hi | lo
        n_lo, n_hi, n_both = count_ge(c_lo), count_ge(c_hi), count_ge(c_both)
        return jnp.where(n_both >= k, c_both, jnp.where(n_hi >= k, c_hi, jnp.where(n_lo >= k, c_lo, thr)))

    return body(30, lax.fori_loop(0, 15, body2, thr))


DSA_GROUP = 4
DSA_SEQS = 2


def _dsa_prompt_kernel(q_ref, qi_ref, kw_ref, k_ref, v_ref, ki_ref, bias_ref, o_ref,
                       key_ref, m_ref, acc_ref, *, topk):
    i = pl.program_id(1)
    QB, HD = Q_BLOCK, ATT_HEAD_DIM
    seqs = range(q_ref.shape[0])
    each = lambda f, *cols: [f(*vals) for vals in zip(*cols)]
    zeros64 = jnp.zeros((HD, QB), F32)
    rep = ATT_HEADS // ATT_KV_HEADS

    def stacked_queries(q_t):
        tiles = []
        for h in range(ATT_HEADS):
            qh = q_t[h * HD:(h + 1) * HD]
            tiles.append(jnp.concatenate([qh, zeros64] if h // rep == 0 else [zeros64, qh], axis=0))
        return (jnp.concatenate(tiles, axis=1) * (HD ** -0.5 * LOG2E)).astype(BF16)

    def padded_index_queries(qi_t):
        return jnp.concatenate([jnp.concatenate([qi_t[h * IDX_DIM:(h + 1) * IDX_DIM], zeros64], axis=0)
                                for h in range(IDX_HEADS)], axis=1).astype(BF16)

    qs_t = [stacked_queries(q_ref[s].T) for s in seqs]
    qi_pad = [padded_index_queries(qi_ref[s].T * IDX_DIM ** -0.5) for s in seqs]
    w_t = [kw_ref[s].T for s in seqs]
    w_rows = [[_bf16_round(w[IDX_DIM + h:IDX_DIM + h + 1] * IDX_HEADS ** -0.5) for h in range(IDX_HEADS)] for w in w_t]

    row = lax.broadcasted_iota(I32, (QB, QB), 0)
    col = lax.broadcasted_iota(I32, (QB, QB), 1)
    n_grp = (i + DSA_GROUP) // DSA_GROUP

    def score_group(gi, carry):
        for u in range(DSA_GROUP):
            j = gi * DSA_GROUP + u
            r0 = pl.multiple_of(j * QB, QB)
            vis = (j * QB + row) <= (i * QB + col)
            dots = [jnp.dot(ki_ref[s, pl.ds(r0, QB), :].astype(BF16), qi_pad[s], preferred_element_type=F32)
                    for s in seqs]
            for s in seqs:
                sc = jnp.zeros((QB, QB), F32)
                for h in range(IDX_HEADS):
                    sc = sc + _bf16_round(jnp.maximum(dots[s][:, h * QB:(h + 1) * QB], 0.0)) * w_rows[s][h]
                key = _ordered_key(jnp.where(vis, sc, NEG_INF))
                key_ref[s, pl.ds(r0, QB), :] = jnp.where(j <= i, key, INT_MIN)
        return carry

    lax.fori_loop(0, n_grp, score_group, 0)

    n_seq = q_ref.shape[0]

    def count_ge(cand):
        rows = DSA_GROUP * QB

        def grp(gi, accs):
            r0 = pl.multiple_of(gi * rows, rows)
            out = []
            for s in seqs:
                hit = jnp.where(key_ref[s, pl.ds(r0, rows), :] >= cand[s], 1, 0)
                out.append(accs[s] + jnp.sum(hit.reshape(rows // SUBLANES, SUBLANES, QB), axis=0))
            return tuple(out)
        accs = lax.fori_loop(0, n_grp, grp, tuple(jnp.zeros((SUBLANES, QB), I32) for _ in seqs))
        return jnp.stack([jnp.sum(a, axis=0, keepdims=True) for a in accs], axis=0)

    thr_all = _kth_largest(count_ge, (n_seq, 1, QB), topk)
    need_all = (topk - count_ge(thr_all + 1)).astype(F32)
    thr = [thr_all[s] for s in seqs]
    need = [need_all[s] for s in seqs]

    m_ref[...] = jnp.full(m_ref.shape, NEG_INF, F32)
    acc_ref[...] = jnp.zeros(acc_ref.shape, F32)
    tri = jnp.where(col <= row, 1.0, 0.0).astype(BF16)
    ones_rows = jnp.ones((SUBLANES, 2 * QB), BF16)

    def attn_pair(jp, taken):
        taken = list(taken)
        lgs = [[] for _ in seqs]
        for u in range(2):
            j = 2 * jp + u
            r0 = pl.multiple_of(j * QB, QB)
            vis = (j * QB + row) <= (i * QB + col)
            bias = bias_ref[jnp.clip(i - j, 0, 2)]
            keys = [key_ref[s, pl.ds(r0, QB), :] for s in seqs]
            eqs = each(lambda k_, t_: k_ == t_, keys, thr)
            prefix = each(lambda e_: jnp.dot(tri, jnp.where(e_, 1.0, 0.0).astype(BF16), preferred_element_type=F32), eqs)
            lg = [jnp.dot(k_ref[s, pl.ds(r0, QB), :].astype(BF16), qs_t[s], preferred_element_type=F32) for s in seqs]
            for s in seqs:
                sel = jnp.logical_or(keys[s] > thr[s], jnp.logical_and(eqs[s], taken[s] + prefix[s] <= need[s]))
                taken[s] = taken[s] + prefix[s][QB - 1:QB, :]
                madd = jnp.where(jnp.logical_and(sel, vis), 0.0, NEG_INF)
                lgs[s].append(lg[s] + bias + jnp.concatenate([madd] * ATT_HEADS, axis=1))
        m_old = [m_ref[s] for s in seqs]
        m_new = each(lambda mo, l2: jnp.maximum(mo, jnp.maximum(jnp.max(l2[0], axis=0, keepdims=True),
                                                                jnp.max(l2[1], axis=0, keepdims=True))), m_old, lgs)
        p = each(lambda l2, mn: jnp.concatenate([jnp.exp2(l_ - mn).astype(BF16) for l_ in l2], axis=0), lgs, m_new)
        rows2 = pl.ds(pl.multiple_of(jp * 2 * QB, 2 * QB), 2 * QB)
        v_aug = [jnp.concatenate([v_ref[s, rows2, :].T.astype(BF16), ones_rows], axis=0) for s in seqs]
        pv = each(lambda va, p_: jnp.dot(va, p_, preferred_element_type=F32), v_aug, p)
        for s in seqs:
            acc_ref[s] = jnp.exp2(m_old[s] - m_new[s]) * acc_ref[s] + pv[s]
            m_ref[s] = m_new[s]
        return tuple(taken)

    lax.fori_loop(0, (i + 2) // 2, attn_pair, tuple(jnp.zeros((1, QB), F32) for _ in seqs))

    for s in seqs:
        acc = acc_ref[s]
        out_t = (acc[:LANES] / acc[LANES:LANES + 1]).T
        o_ref[s] = jnp.concatenate(
            [out_t[h * QB:(h + 1) * QB, (h // rep) * HD:(h // rep + 1) * HD] for h in range(ATT_HEADS)], axis=1)


def _dsa_prompt(c_att, bias_t):
    b, t, _ = c_att.shape
    nq = t // Q_BLOCK
    assert nq % DSA_GROUP == 0, (t, Q_BLOCK, DSA_GROUP)
    t_keys =((nq + DSA_GROUP - 1) // DSA_GROUP) * DSA_GROUP * Q_BLOCK
    ns = _row_tile(b, DSA_SEQS)
    blk = lambda w, cb: pl.BlockSpec((ns, Q_BLOCK, w), lambda bi, i: (bi, i, cb))
    allk = lambda cb: pl.BlockSpec((ns, t, LANES), lambda bi, i: (bi, 0, cb))
    return pl.pallas_call(
        functools.partial(_dsa_prompt_kernel, topk=min(TOPK, t // 4)),
        grid=(b // ns, nq),
        in_specs=[blk(512, ATT_Q0 // 512), blk(256, ATT_QI0 // 256), blk(LANES, ATT_KI0 // LANES),
                  allk(ATT_K0 // LANES), allk(ATT_V0 // LANES), allk(ATT_KI0 // LANES), _full(bias_t.shape)],
        out_specs=pl.BlockSpec((ns, Q_BLOCK, ATT_HEADS * ATT_HEAD_DIM), lambda bi, i: (bi, i, 0)),
        out_shape=jax.ShapeDtypeStruct((b, t, ATT_HEADS * ATT_HEAD_DIM), F32),
        scratch_shapes=[pltpu.VMEM((ns, t_keys, Q_BLOCK), I32), pltpu.VMEM((ns, 1, ATT_HEADS * Q_BLOCK), F32),
                        pltpu.VMEM((ns, LANES + SUBLANES, ATT_HEADS * Q_BLOCK), F32)],
        compiler_params=_cparams(("parallel", "arbitrary")),
        name="dsa_prompt",
    )(c_att, c_att, c_att, c_att, c_att, c_att, bias_t)


def _dsa_sample_kernel(pt_ref, l_ref, kidx_hbm, k_hbm, v_hbm, c_ref, bias_ref, o_ref,
                       ki_buf, k_buf, v_buf, sems, key_ref, madd_ref, *, n_pages, tq, topk):
    b = pl.program_id(0)
    slot = b % 2
    layer = l_ref[0]
    HD = ATT_HEAD_DIM
    rep = ATT_HEADS // ATT_KV_HEADS
    n_past = n_pages * PAGE

    def page_copies(page, p, sl):
        cols = pl.ds(pl.multiple_of(p * PAGE, PAGE), PAGE)
        return (pltpu.make_async_copy(kidx_hbm.at[layer, page], ki_buf.at[sl, :, cols], sems.at[0, sl]),
                pltpu.make_async_copy(k_hbm.at[layer, page], k_buf.at[sl, :, cols], sems.at[1, sl]),
                pltpu.make_async_copy(v_hbm.at[layer, page], v_buf.at[sl, :, cols], sems.at[2, sl]))

    def start_gather(bi, sl):
        def body(p2, carry):
            for u in range(2):
                p = 2 * p2 + u
                for cp in page_copies(pt_ref[bi * n_pages + p], p, sl):
                    cp.start(priority=u)
            return carry
        lax.fori_loop(0, n_pages // 2, body, 0)

    def wait_gather(sl):
        def body(p, carry):
            for cp in page_copies(0, p, sl):
                cp.wait()
            return carry
        lax.fori_loop(0, n_pages, body, 0)

    @pl.when(b == 0)
    def _():
        start_gather(0, 0)

    @pl.when(b + 1 < pl.num_programs(0))
    def _():
        start_gather(b + 1, 1 - slot)

    wait_gather(slot)

    qi = c_ref[0, :, ATT_QI0:ATT_QI0 + IDX_HEADS * IDX_DIM] * IDX_DIM ** -0.5
    wi = _bf16_round(c_ref[0, :, ATT_WI0:ATT_WI0 + IDX_HEADS] * IDX_HEADS ** -0.5)
    qi_rows = jnp.concatenate([qi[:, h * IDX_DIM:(h + 1) * IDX_DIM] for h in range(IDX_HEADS)], axis=0)

    def scores(dots):
        d = _bf16_round(jnp.maximum(dots, 0.0))
        s = jnp.zeros((tq, dots.shape[1]), F32)
        for h in range(IDX_HEADS):
            s = s + d[h * tq:(h + 1) * tq] * wi[:, h:h + 1]
        return s

    causal_new = lax.broadcasted_iota(I32, (tq, PAGE), 1) <= lax.broadcasted_iota(I32, (tq, PAGE), 0)
    key_ref[:, 0:n_past] = _ordered_key(scores(_mm(qi_rows, ki_buf[slot])))
    ki_new = jnp.concatenate([c_ref[0, :, ATT_KI0:ATT_KI0 + IDX_DIM], jnp.zeros((PAGE - tq, IDX_DIM), F32)], axis=0)
    key_ref[:, n_past:n_past + PAGE] = _ordered_key(jnp.where(causal_new, scores(_mm_nt(qi_rows, ki_new)), NEG_INF))
    keys = key_ref[...]
    count_ge = lambda cand: jnp.sum(jnp.where(keys >= cand, 1, 0), axis=1, keepdims=True)
    thr = _kth_largest(count_ge, (tq, 1), topk, two_bits=True)
    excess =jnp.max(count_ge(thr) - topk)

    @pl.when(excess == 0)
    def _():
        madd_ref[...] = jnp.where(keys >= thr, 0.0, NEG_INF)

    @pl.when(excess > 0)
    def _():
        need = (topk - count_ge(thr + 1)).astype(F32)
        tri = jnp.where(lax.broadcasted_iota(I32, (PAGE, PAGE), 0) <= lax.broadcasted_iota(I32, (PAGE, PAGE), 1),
                        1.0, 0.0).astype(BF16)

        def blk(jb, taken):
            cols = pl.ds(pl.multiple_of(jb * PAGE, PAGE), PAGE)
            key = key_ref[:, cols]
            eq = key == thr
            prefix = jnp.dot(jnp.where(eq, 1.0, 0.0).astype(BF16), tri, preferred_element_type=F32)
            sel = jnp.logical_or(key > thr, jnp.logical_and(eq, taken + prefix <= need))
            madd_ref[:, cols] = jnp.where(sel, 0.0, NEG_INF)
            return taken + prefix[:, PAGE - 1:PAGE]

        lax.fori_loop(0, n_pages + 1, blk, jnp.zeros((tq, 1), F32))

    q = c_ref[0, :, ATT_Q0:ATT_Q0 + ATT_HEADS * HD] * HD ** -0.5
    z = jnp.zeros((tq, HD), F32)
    qs = jnp.concatenate([jnp.concatenate([q[:, h * HD:(h + 1) * HD], z] if h // rep == 0
                                          else [z, q[:, h * HD:(h + 1) * HD]], axis=1)
                          for h in range(ATT_HEADS)], axis=0)
    pad = jnp.zeros((PAGE - tq, LANES), F32)
    k_new = jnp.concatenate([c_ref[0, :, ATT_K0:ATT_K0 + LANES], pad], axis=0)
    v_new = jnp.concatenate([c_ref[0, :, ATT_V0:ATT_V0 + LANES], pad], axis=0)
    lg_past = _mm(qs, k_buf[slot])
    far = bias_ref[2][:, 0:1]
    madd_ref[:, n_past:n_past + PAGE] = jnp.where(causal_new, madd_ref[:, n_past:n_past + PAGE], NEG_INF)
    madd = madd_ref[...]
    lg = jnp.concatenate([lg_past[:, :n_past - PAGE] + far, lg_past[:, n_past - PAGE:] + bias_ref[1],
                          _mm_nt(qs, k_new) + bias_ref[0]], axis=1) + jnp.concatenate([madd] * ATT_HEADS, axis=0)
    pr = jnp.exp(lg - jnp.max(lg, axis=-1, keepdims=True))
    out = (_mm_nt(pr[:, :n_past], v_buf[slot]) + _mm(pr[:, n_past:], v_new)) / jnp.sum(pr, axis=-1, keepdims=True)
    o_ref[0] = jnp.concatenate(
        [out[h * tq:(h + 1) * tq, (h // rep) * HD:(h // rep + 1) * HD] for h in range(ATT_HEADS)], axis=1)


def _dsa_sample(c_att, t_valid, layer, page_table, kidx_pool, k_pool, v_pool, bias_s):
    b, tq, _ = c_att.shape
    n_pages = page_table.shape[1]
    pt = page_table.reshape(-1)
    assert n_pages % 2 == 0, n_pages
    n_past = n_pages * PAGE
    n_keys = n_past + PAGE
    hbm = pl.BlockSpec(memory_space=pl.ANY)
    return pl.pallas_call(
        functools.partial(_dsa_sample_kernel, n_pages=n_pages, tq=tq, topk=min(TOPK, (n_past + t_valid) // 4)),
        grid_spec=pltpu.PrefetchScalarGridSpec(
            num_scalar_prefetch=2,
            grid=(b,),
            in_specs=[hbm, hbm, hbm, pl.BlockSpec((1, tq, ATT_W), lambda bi, pt_ref, l_ref: (bi, 0, 0)),
                      pl.BlockSpec(bias_s.shape, lambda bi, pt_ref, l_ref: (0, 0, 0))],
            out_specs=pl.BlockSpec((1, tq, ATT_HEADS * ATT_HEAD_DIM), lambda bi, pt_ref, l_ref: (bi, 0, 0)),
            scratch_shapes=[pltpu.VMEM((2, IDX_DIM, n_past), F32), pltpu.VMEM((2, LANES, n_past), F32),
                            pltpu.VMEM((2, LANES, n_past), F32), pltpu.SemaphoreType.DMA((3, 2)),
                            pltpu.VMEM((tq, n_keys), I32), pltpu.VMEM((tq, n_keys), F32)],
        ),
        out_shape=jax.ShapeDtypeStruct((b, tq, ATT_HEADS * ATT_HEAD_DIM), F32),
        compiler_params=_cparams(("arbitrary",)),
        name="dsa_sample",
    )(pt, layer, kidx_pool, k_pool, v_pool, c_att, bias_s)


def _t5_bucket(dist):
    n = jnp.maximum(dist, 0)
    max_exact = NUM_BUCKETS // 2
    nf = jnp.maximum(n, 1).astype(F32)
    large = max_exact + (jnp.log(nf / max_exact) / math.log(MAX_DISTANCE / max_exact)
                         * (NUM_BUCKETS - max_exact)).astype(I32)
    large = jnp.minimum(large, NUM_BUCKETS - 1)
    return jnp.where(n < max_exact, n, large)


def _bias_tiles(rel_bias):
    r = jnp.arange(Q_BLOCK)
    d0 = r[:, None] - r[None, :]
    dist = jnp.stack([d0, d0 + Q_BLOCK, jnp.full_like(d0, 2 * Q_BLOCK)])
    onehot = (_t5_bucket(dist)[..., None] == jnp.arange(NUM_BUCKETS)).astype(F32)
    return jnp.einsum("cqkn,nh->chqk", onehot, rel_bias.astype(F32), precision=lax.Precision.HIGHEST)


def _group_layer(x, w, lb, mem, mem_blocks, shift, s_rwkv0, s_hgrn0, attend, cfg):
    b, t, d = x.shape
    n = b * t
    tm, t_valid = cfg["tm"], cfg["t_valid"]
    x2 = x.reshape(n, d)
    x2 = _ffn(x2, w["ffn1_norm_pre"], w["ffn1_w_up"], w["ffn1_w_down"], w["ffn1_norm_post"], cfg["tm_ffn"])
    c_rw, c_att, c_hg, gates = _proj(x2, w["mix_norm_pre"], [w["w_in_rw"], w["w_in_att"], w["w_in_hg"], w["w_in_gate"]],
                                     True, tm, "mix_in")
    c_rw = c_rw.reshape(b, t, -1)
    c_att = c_att.reshape(b, t, -1)
    streams = _rwkv_prep(c_rw, shift, w, t_valid, cfg["tm_prep"])
    o_a, s_rwkv = _rwkv(streams, s_rwkv0, w["rwkv_ln_w"], w["rwkv_ln_b"], w["rwkv_r_k"], cfg["rwkv_chunk"],
                         cfg["rwkv_seqs"])
    o_b = attend(c_att)
    o_c, s_hgrn = _hgrn(c_hg.reshape(b, t, -1), lb, w["hgrn_norm_w"], s_hgrn0, cfg["hgrn_chunk"], t_valid, cfg["hgrn_tb"])
    x2 = _merge(x2, o_a.reshape(n, -1), o_b.reshape(n, -1), o_c.reshape(n, -1), gates,
                w["mix_w_proj_a"], w["mix_w_proj_b"], w["mix_w_proj_c"], w["mix_w_out"], w["mix_norm_post"], tm)
    (q,) = _proj(x2, w["cross_norm_pre"], [w["cross_wq"]], True, tm, "cross_q")
    o = _xattn(q.reshape(b, t, d), mem[0], mem[1], mem_blocks[0], mem_blocks[1], cfg["tq_x"])
    x2 = _out_proj(x2, o.reshape(n, d), w["cross_wo"], w["cross_norm_post"], tm)
    x2 = _ffn(x2, w["ffn2_norm_pre"], w["ffn2_w_up"], w["ffn2_w_down"], w["ffn2_norm_post"], cfg["tm_ffn"])
    nkv = ATT_KV_HEADS * ATT_HEAD_DIM
    state = (c_rw[:, t_valid - 1], s_rwkv, s_hgrn,
             c_att[:, :t_valid, ATT_K0:ATT_K0 + nkv].reshape(b, t_valid, ATT_KV_HEADS, ATT_HEAD_DIM),
             c_att[:, :t_valid, ATT_V0:ATT_V0 + nkv].reshape(b, t_valid, ATT_KV_HEADS, ATT_HEAD_DIM),
             c_att[:, :t_valid, ATT_KI0:ATT_KI0 + IDX_DIM])
    return x2.reshape(b, t, d), state


PROMPT_CFG = dict(tm=512, tm_ffn=512, tm_prep=256, rwkv_chunk=64, rwkv_seqs=4, hgrn_chunk=16, hgrn_tb=256, tq_x=512)
SAMPLE_PAD = 8
SAMPLE_CFG = dict(tm=256, tm_ffn=256, tm_prep=SAMPLE_PAD, rwkv_chunk=SAMPLE_PAD, rwkv_seqs=4, hgrn_chunk=SAMPLE_PAD,
                  hgrn_tb=SAMPLE_PAD, tq_x=SAMPLE_PAD)


def kernel(x_prompt, x_sample, cache_k, cache_v, cache_kidx, cache_mem_k, cache_mem_v, state_rwkv, state_rwkv_shift, state_hgrn, page_table, mem_prompt, rel_bias, hgrn_lb_logits, ffn1_norm_pre, ffn1_norm_post, ffn1_w_up, ffn1_w_down, mix_norm_pre, mix_norm_post, mix_w_in, rwkv_mu, rwkv_w0, rwkv_w_up, rwkv_a0, rwkv_a_up, rwkv_g_up, rwkv_k_k, rwkv_k_a, rwkv_r_k, rwkv_ln_w, rwkv_ln_b, hgrn_norm_w, mix_w_proj_a, mix_w_proj_b, mix_w_proj_c, mix_w_out, cross_norm_pre, cross_norm_post, cross_wq, cross_wk, cross_wv, cross_wo, ffn2_norm_pre, ffn2_norm_post, ffn2_w_up, ffn2_w_down):
    depth, d = ffn1_norm_pre.shape
    bp, tp, _ = x_prompt.shape
    bs, ts, _ = x_sample.shape
    d_r = rwkv_w0.shape[1]
    rw_cols = rwkv_mu.shape[1]
    d_h = hgrn_norm_w.shape[1]
    n_att = ATT_WI0 + IDX_HEADS - ATT_Q0
    bf = lambda a: a.astype(BF16)
    vec = lambda a: a[:, None, :]

    o1 = rw_cols + n_att
    w_att = jnp.pad(mix_w_in[:, :, rw_cols:o1], ((0, 0), (0, 0), (0, ATT_W - n_att)))
    head_sum = (np.arange(d_r)[:, None] // RWKV_HEAD == np.arange(d_r)[None, :] // RWKV_HEAD).astype(np.float32)
    per_head = lambda a: a.reshape(depth, RWKV_HEADS, 1, RWKV_HEAD)
    weights = dict(
        ffn1_norm_pre=vec(ffn1_norm_pre), ffn1_norm_post=vec(ffn1_norm_post), ffn1_w_up=bf(ffn1_w_up), ffn1_w_down=bf(ffn1_w_down),
        ffn2_norm_pre=vec(ffn2_norm_pre), ffn2_norm_post=vec(ffn2_norm_post), ffn2_w_up=bf(ffn2_w_up), ffn2_w_down=bf(ffn2_w_down),
        mix_norm_pre=vec(mix_norm_pre), mix_norm_post=vec(mix_norm_post),
        w_in_rw=bf(mix_w_in[:, :, :rw_cols]), w_in_att=bf(w_att),
        w_in_hg=bf(mix_w_in[:, :, o1:o1 + 4 * d_h]), w_in_gate=bf(mix_w_in[:, :, o1 + 4 * d_h:]),
        rwkv_mu=vec(rwkv_mu), rwkv_w0=vec(rwkv_w0), rwkv_w_up=bf(rwkv_w_up), rwkv_a0=vec(rwkv_a0), rwkv_a_up=bf(rwkv_a_up),
        rwkv_g_up=bf(rwkv_g_up), rwkv_k_k=vec(rwkv_k_k), rwkv_k_a=vec(rwkv_k_a),
        rwkv_r_k=rwkv_r_k[:, :, None, :], rwkv_ln_w=per_head(rwkv_ln_w), rwkv_ln_b=per_head(rwkv_ln_b),
        hgrn_norm_w=vec(hgrn_norm_w),
        mix_w_proj_a=bf(mix_w_proj_a), mix_w_proj_b=bf(mix_w_proj_b), mix_w_proj_c=bf(mix_w_proj_c), mix_w_out=bf(mix_w_out),
        cross_norm_pre=vec(cross_norm_pre), cross_norm_post=vec(cross_norm_post), cross_wq=bf(cross_wq), cross_wo=bf(cross_wo),
        cross_wkv=bf(jnp.concatenate([cross_wk, cross_wv], axis=-1)),
    )
    p_lb = jax.nn.softmax(hgrn_lb_logits.astype(F32), axis=0)
    lower_bounds = vec(jnp.cumsum(p_lb, axis=0) - p_lb[0:1])

    bias = _bias_tiles(rel_bias)
    bias_t = jnp.transpose(bias * LOG2E, (0, 3, 1, 2)).reshape(3, Q_BLOCK, ATT_HEADS * Q_BLOCK)
    bias_s = bias[:, :, :SAMPLE_PAD, :].reshape(3, ATT_HEADS * SAMPLE_PAD, Q_BLOCK)

    pad_t = SAMPLE_PAD - ts
    xs0 = jnp.pad(x_sample, ((0, 0), (0, pad_t), (0, 0)))
    n_pool = cache_k.shape[1]
    nkv = ATT_KV_HEADS * ATT_HEAD_DIM
    k_pool = jnp.transpose(cache_k, (0, 1, 3, 4, 2)).reshape(depth, n_pool, nkv, PAGE)
    v_pool = jnp.transpose(cache_v, (0, 1, 3, 4, 2)).reshape(depth, n_pool, nkv, PAGE)
    kidx_pool = jnp.swapaxes(cache_kidx, 2, 3)
    mem_tokens = mem_prompt.shape[1]
    mem2 = mem_prompt.reshape(bp * mem_tokens, d)
    ones_d = jnp.ones((1, d), F32)
    zero_shift = jnp.zeros((bp, 1, rw_cols), F32)
    zero_rwkv = jnp.zeros((bp, RWKV_HEADS, RWKV_HEAD, RWKV_HEAD), F32)
    zero_hgrn = jnp.zeros((bp, HGRN_HEADS, HGRN_EXPAND, HGRN_EXPAND), F32)
    prompt_cfg = dict(PROMPT_CFG, t_valid=tp)
    sample_cfg = dict(SAMPLE_CFG, t_valid=ts)

    def layer(carry, per_layer):
        xp, xs = carry
        w, lb, mem_ks, mem_vs, s_rw, s_sh, s_hg, li = per_layer
        w = dict(w, head_sum=jnp.asarray(head_sum, BF16))
        (mem_kv,) = _proj(mem2, ones_d, [w["cross_wkv"]], False, 256, "mem_kv")
        mem_kv = mem_kv.reshape(bp, mem_tokens, 2 * d)
        xp, st_p = _group_layer(xp, w, lb, (mem_kv, mem_kv), (0, 1), zero_shift, zero_rwkv, zero_hgrn,
                                lambda c: _dsa_prompt(c, bias_t), prompt_cfg)
        attend_s = lambda c: _dsa_sample(c, ts, li, page_table, kidx_pool, k_pool, v_pool, bias_s)
        xs, st_s = _group_layer(xs, w, lb, (mem_ks, mem_vs), (0, 0), s_sh, s_rw, s_hg, attend_s, sample_cfg)
        mem_k = mem_kv[:, :, :d].reshape(bp, mem_tokens, MEM_HEADS, d // MEM_HEADS)
        mem_v = mem_kv[:, :, d:].reshape(bp, mem_tokens, MEM_HEADS, d // MEM_HEADS)
        return (xp, xs), (st_p, (mem_k, mem_v), st_s)

    per_layer = (weights, lower_bounds,
                 cache_mem_k.reshape(depth, bs, mem_tokens, d), cache_mem_v.reshape(depth, bs, mem_tokens, d),
                 state_rwkv, state_rwkv_shift[:, :, None, :], state_hgrn,
                 jnp.arange(depth, dtype=I32)[:, None])
    (xp, xs), (st_p, (mem_k, mem_v), st_s) = lax.scan(layer, (x_prompt, xs0), per_layer)
    sh_p, rw_p, hg_p, k_p, v_p, ki_p = st_p
    sh_s, rw_s, hg_s, k_s, v_s, ki_s = st_s
    return (xp, xs[:, :ts], k_p, v_p, ki_p, mem_k, mem_v, rw_p, sh_p, hg_p, k_s, v_s, ki_s, rw_s, sh_s, hg_s)
```
